```python
import math
import jax, jax.numpy as jnp
from jax import lax
import numpy as np

D_MODEL = 1024
BATCH = 16
SEQ = 2048
DEPTH = 2
DEC_BATCH = 32
DEC_SEQ = 2048
PAST_LEN = 128

GRID_W = 64
NA_HEADS = 8
NA_HEAD_DIM = 64
NA_W = NA_HEADS * NA_HEAD_DIM
NA_WIN_R = 8
NA_WIN_C = 16
NA_QBLK_C = 16
NA_KBAND_C = 32
ML_HEADS = 4
ML_HEAD_DIM = 128
ML_W = ML_HEADS * ML_HEAD_DIM
ML_CHUNK = 128
EVEN_MIX_W = NA_W + ML_W
EVEN_SPLITS = tuple(int(s) for s in np.cumsum([NA_W] * 3 + [ML_W] * 4))
EVEN_IN = 3 * NA_W + 4 * ML_W + 4 * ML_HEADS
DA_HEADS = 16
DA_HEAD_DIM = 64
DA_W = DA_HEADS * DA_HEAD_DIM
DA_BRANCHES = ((128, 1), (512, 4), (2048, 16))
DA_QBLK = 64
ROPE_THETA = 10000.0
N_EXPERTS = 16
EC_CAPACITY_FACTOR = 2
D_FF_EXPERT = 1408
LN_EPS = 1e-5
DN_ALPHA = (2 * DEPTH) ** 0.25
DN_BETA = (8 * DEPTH) ** -0.25
N_EVEN = (DEPTH + 1) // 2
N_ODD = DEPTH // 2

kernel_name = "hybrid_natten_mlstm_dilated_ec_encoder"


def layer_norm(x, g, b):
    xf = x.astype(jnp.float32)
    mu = xf.mean(-1, keepdims=True)
    var = jnp.square(xf - mu).mean(-1, keepdims=True)
    return ((xf - mu) * lax.rsqrt(var + LN_EPS) * g + b).astype(x.dtype)


def to_heads(a, h):
    B, T, _ = a.shape
    return a.reshape(B, T, h, -1).transpose(0, 2, 1, 3)


def from_heads(a):
    B, H, T, d = a.shape
    return a.transpose(0, 2, 1, 3).reshape(B, T, H * d)


def rope(x):
    T, d = x.shape[2], x.shape[3]
    inv = ROPE_THETA ** (-jnp.arange(0, d, 2, dtype=jnp.float32) / d)
    ang = jnp.arange(T, dtype=jnp.float32)[:, None] * inv[None, :]
    cos, sin = jnp.cos(ang), jnp.sin(ang)
    xf = x.astype(jnp.float32)
    x1, x2 = xf[..., : d // 2], xf[..., d // 2:]
    return jnp.concatenate([x1 * cos - x2 * sin, x1 * sin + x2 * cos], -1).astype(x.dtype)


def neighbourhood_attention(q, k, v, rpb):
    B, H, T, dh = q.shape
    rows = T // GRID_W
    wr = min(NA_WIN_R, rows)
    n_cb = GRID_W // NA_QBLK_C
    qg = (q * dh ** -0.5).reshape(B, H, rows, n_cb, NA_QBLK_C, dh)
    kg = k.reshape(B, H, rows, GRID_W, dh)
    vg = v.reshape(B, H, rows, GRID_W, dh)
    band_start = np.clip(np.arange(n_cb) * NA_QBLK_C - NA_WIN_C // 2, 0, GRID_W - NA_KBAND_C)
    key_col = band_start[:, None] + np.arange(NA_KBAND_C)[None, :]
    q_col = np.arange(GRID_W).reshape(n_cb, NA_QBLK_C)
    win_c0 = np.clip(q_col - NA_WIN_C // 2, 0, GRID_W - NA_WIN_C)
    col_valid = (key_col[:, None, :] >= win_c0[:, :, None]) & (key_col[:, None, :] < win_c0[:, :, None] + NA_WIN_C)
    dc = np.clip(key_col[:, None, :] - q_col[:, :, None] + NA_WIN_C - 1, 0, 2 * NA_WIN_C - 2)
    rpb_c = rpb[:, :, dc].astype(jnp.float32)
    kb = kg[:, :, :, key_col]
    vb = vg[:, :, :, key_col]
    mask = col_valid[None, None, :, :, None, :]

    def one_row(r):
        rs = jnp.clip(r - wr // 2, 0, rows - wr)
        k_r = lax.dynamic_slice_in_dim(kb, rs, wr, axis=2)
        v_r = lax.dynamic_slice_in_dim(vb, rs, wr, axis=2)
        q_r = lax.dynamic_index_in_dim(qg, r, axis=2, keepdims=False)
        s = jnp.einsum('bhcqd,bhwckd->bhcqwk', q_r, k_r, preferred_element_type=jnp.float32)
        dr = rs + jnp.arange(wr) - r + NA_WIN_R - 1
        bias = jnp.take(rpb_c, dr, axis=1).transpose(0, 2, 3, 1, 4)
        s = jnp.where(mask, s + bias[None], -jnp.inf)
        p = jax.nn.softmax(s.reshape(B, H, n_cb, NA_QBLK_C, wr * NA_KBAND_C), axis=-1).reshape(s.shape)
        return jnp.einsum('bhcqwk,bhwckd->bhcqd', p.astype(v.dtype), v_r)

    out = lax.map(one_row, jnp.arange(rows))
    return out.transpose(1, 2, 0, 3, 4, 5).reshape(B, H, T, dh)


def mlstm_direction(q, k, v, log_i, log_f):
    B, H, T, d = q.shape
    L = ML_CHUNK
    nc = T // L

    def chunks(a):
        return jnp.moveaxis(a.reshape(B, H, nc, L, *a.shape[3:]), 2, 0)

    lower = np.tril(np.ones((L, L), dtype=bool))

    def step(carry, inp):
        C, n, m = carry
        qt, kt, vt, it, ft = inp
        b = jnp.cumsum(ft, axis=-1)
        log_d = jnp.where(lower, b[..., :, None] - b[..., None, :] + it[..., None, :], -jnp.inf)
        log_inter = b + m[..., None]
        m_t = jnp.maximum(log_inter, log_d.max(-1))
        w_inter = jnp.exp(log_inter - m_t)
        s = jnp.einsum('bhtd,bhsd->bhts', qt, kt) * jnp.exp(log_d - m_t[..., None])
        num = w_inter[..., None] * jnp.einsum('bhtd,bhde->bhte', qt, C) + jnp.einsum('bhts,bhse->bhte', s, vt)
        den = w_inter * jnp.einsum('bhtd,bhd->bht', qt, n) + s.sum(-1)
        h = num / jnp.maximum(jnp.abs(den), jnp.exp(-m_t))[..., None]
        b_last = b[..., -1]
        log_w = b_last[..., None] - b + it
        m_new = jnp.maximum(b_last + m, log_w.max(-1))
        w_s = jnp.exp(log_w - m_new[..., None])
        decay = jnp.exp(b_last + m - m_new)
        kw = kt * w_s[..., None]
        C_new = decay[..., None, None] * C + jnp.einsum('bhsd,bhse->bhde', kw, vt)
        n_new = decay[..., None] * n + kw.sum(2)
        return (C_new, n_new, m_new), h

    init = (jnp.zeros((B, H, d, d), jnp.float32), jnp.zeros((B, H, d), jnp.float32), jnp.zeros((B, H), jnp.float32))
    _, h = lax.scan(step, init, (chunks(q), chunks(k), chunks(v), chunks(log_i), chunks(log_f)))
    return jnp.moveaxis(h, 0, 2).reshape(B, H, T, d)


def mlstm_bidirectional(q, k, v, i_fwd, i_bwd, f_fwd, f_bwd):
    h_f = mlstm_direction(q, k, v, i_fwd, jax.nn.log_sigmoid(f_fwd))
    fl = lambda a: jnp.flip(a, axis=2)
    h_b = fl(mlstm_direction(fl(q), fl(k), fl(v), fl(i_bwd), fl(jax.nn.log_sigmoid(f_bwd))))
    return h_f + h_b


def even_mixer(x, w_in, gate_bias, rpb, norm_g, w_out):
    B, T, _ = x.shape
    proj = x @ w_in
    qa, ka, va, qb, kb, vb, ob, gates = jnp.split(proj, EVEN_SPLITS, axis=-1)
    ya = from_heads(neighbourhood_attention(to_heads(qa, NA_HEADS), to_heads(ka, NA_HEADS), to_heads(va, NA_HEADS), rpb))
    g = (gates + gate_bias).astype(jnp.float32).reshape(B, T, 4, ML_HEADS).transpose(2, 0, 3, 1)
    qm = to_heads(qb, ML_HEADS).astype(jnp.float32)
    km = to_heads(kb, ML_HEADS).astype(jnp.float32) * ML_HEAD_DIM ** -0.5
    vm = to_heads(vb, ML_HEADS).astype(jnp.float32)
    h = mlstm_bidirectional(qm, km, vm, g[0], g[1], g[2], g[3])
    mu = h.mean(-1, keepdims=True)
    var = jnp.square(h - mu).mean(-1, keepdims=True)
    h = from_heads((h - mu) * lax.rsqrt(var + LN_EPS)) * norm_g
    yb = (jax.nn.sigmoid(ob.astype(jnp.float32)) * h).astype(x.dtype)
    return jnp.concatenate([ya, yb], axis=-1) @ w_out


def dilated_branch(q, k, v, window, dil):
    B, H, T, dh = q.shape
    half = window // (2 * dil)
    n = T // dil
    Q = DA_QBLK
    nb = -(-n // Q)
    nh = -(-half // Q)
    KB = (2 * nh + 1) * Q
    pad_q = nb * Q - n
    sub = lambda a: a.reshape(B, H, n, dil, dh).transpose(0, 1, 3, 2, 4)
    qp = jnp.pad(sub(q), ((0, 0),) * 3 + ((0, pad_q), (0, 0)))
    kp = jnp.pad(sub(k), ((0, 0),) * 3 + ((nh * Q, pad_q + nh * Q), (0, 0)))
    vp = jnp.pad(sub(v), ((0, 0),) * 3 + ((nh * Q, pad_q + nh * Q), (0, 0)))

    def block(j):
        q_j = lax.dynamic_slice_in_dim(qp, j * Q, Q, axis=3)
        k_j = lax.dynamic_slice_in_dim(kp, j * Q, KB, axis=3)
        v_j = lax.dynamic_slice_in_dim(vp, j * Q, KB, axis=3)
        qi = j * Q + jnp.arange(Q)
        ki = j * Q - nh * Q + jnp.arange(KB)
        valid = (jnp.abs(ki[None, :] - qi[:, None]) <= half) & (ki[None, :] >= 0) & (ki[None, :] < n)
        s = jnp.einsum('bhgqd,bhgkd->bhgqk', q_j, k_j, preferred_element_type=jnp.float32)
        s = jnp.where(valid, s, -jnp.inf)
        lse = jax.nn.logsumexp(s, axis=-1)
        o = jnp.einsum('bhgqk,bhgkd->bhgqd', jnp.exp(s - lse[..., None]).astype(v.dtype), v_j)
        return o, lse

    o, lse = lax.map(block, jnp.arange(nb))
    o = jnp.moveaxis(o, 0, 3).reshape(B, H, dil, nb * Q, dh)[:, :, :, :n]
    lse = jnp.moveaxis(lse, 0, 3).reshape(B, H, dil, nb * Q)[:, :, :, :n]
    return o.transpose(0, 1, 3, 2, 4).reshape(B, H, T, dh), lse.transpose(0, 1, 3, 2).reshape(B, H, T)


def odd_mixer(x, w_in, w_out):
    q, k, v = jnp.split(x @ w_in, 3, axis=-1)
    q = rope(to_heads(q, DA_HEADS)) * DA_HEAD_DIM ** -0.5
    k = rope(to_heads(k, DA_HEADS))
    v = to_heads(v, DA_HEADS)
    res = [dilated_branch(q, k, v, w, d) for (w, d) in DA_BRANCHES]
    wts = jax.nn.softmax(jnp.stack([r[1] for r in res]), axis=0)
    o = jnp.einsum('gbht,gbhtd->bhtd', wts, jnp.stack([r[0] for r in res]).astype(jnp.float32))
    return from_heads(o.astype(x.dtype)) @ w_out


def expert_choice_ffn(x, w_router, w_gate, w_up, w_down):
    B, T, D = x.shape
    N = B * T
    cap = EC_CAPACITY_FACTOR * N // N_EXPERTS
    xt = x.reshape(N, D)
    aff = jax.nn.softmax((xt @ w_router).astype(jnp.float32), axis=-1)
    g, idx = lax.top_k(aff.T, cap)
    xe = xt[idx]
    h = jax.nn.silu(jnp.einsum('ecd,edf->ecf', xe, w_gate)) * jnp.einsum('ecd,edf->ecf', xe, w_up)
    ye = jnp.einsum('ecf,efd->ecd', h, w_down) * g[..., None].astype(x.dtype)
    y = jnp.zeros_like(xt).at[idx.reshape(-1)].add(ye.reshape(-1, D))
    return y.reshape(B, T, D)


def setup_inputs(seed: int = 0) -> dict:
    key = jax.random.key(seed)
    ks = jax.random.split(key, 20)
    nrm = lambda k, shape, s: jax.random.normal(k, shape, jnp.float32) * s
    f_bias = jnp.tile(jnp.linspace(3.0, 6.0, ML_HEADS, dtype=jnp.float32), 2)
    gate_base = jnp.concatenate([jnp.zeros((2 * ML_HEADS,), jnp.float32), f_bias])
    return {
        "x_prompt": nrm(ks[0], (BATCH, SEQ, D_MODEL), 1.0),
        "x_sample": nrm(ks[1], (DEC_BATCH, DEC_SEQ, D_MODEL), 1.0),
        "even_w_in": nrm(ks[2], (N_EVEN, D_MODEL, EVEN_IN), D_MODEL ** -0.5),
        "ml_gate_bias": gate_base[None, :] + nrm(ks[3], (N_EVEN, 4 * ML_HEADS), 0.1),
        "na_rpb": nrm(ks[4], (N_EVEN, NA_HEADS, 2 * NA_WIN_R - 1, 2 * NA_WIN_C - 1), 0.3),
        "ml_norm_g": 1.0 + nrm(ks[5], (N_EVEN, ML_W), 0.05),
        "even_w_out": nrm(ks[6], (N_EVEN, EVEN_MIX_W, D_MODEL), EVEN_MIX_W ** -0.5 * DN_BETA),
        "da_w_in": nrm(ks[7], (N_ODD, D_MODEL, 3 * DA_W), D_MODEL ** -0.5),
        "da_w_out": nrm(ks[8], (N_ODD, DA_W, D_MODEL), DA_W ** -0.5 * DN_BETA),
        "ln_mix_g": 1.0 + nrm(ks[9], (DEPTH, D_MODEL), 0.05),
        "ln_mix_b": nrm(ks[10], (DEPTH, D_MODEL), 0.02),
        "ec_router": nrm(ks[11], (DEPTH, D_MODEL, N_EXPERTS), D_MODEL ** -0.5),
        "ec_w_gate": nrm(ks[12], (DEPTH, N_EXPERTS, D_MODEL, D_FF_EXPERT), D_MODEL ** -0.5),
        "ec_w_up": nrm(ks[13], (DEPTH, N_EXPERTS, D_MODEL, D_FF_EXPERT), D_MODEL ** -0.5),
        "ec_w_down": nrm(ks[14], (DEPTH, N_EXPERTS, D_FF_EXPERT, D_MODEL), D_FF_EXPERT ** -0.5 * DN_BETA),
        "ln_ffn_g": 1.0 + nrm(ks[15], (DEPTH, D_MODEL), 0.05),
        "ln_ffn_b": nrm(ks[16], (DEPTH, D_MODEL), 0.02),
    }


def reference(x_prompt, x_sample, even_w_in, ml_gate_bias, na_rpb, ml_norm_g, even_w_out, da_w_in, da_w_out,
              ln_mix_g, ln_mix_b, ec_router, ec_w_gate, ec_w_up, ec_w_down, ln_ffn_g, ln_ffn_b):
    def trunk(x):
        for layer in range(DEPTH):
            if layer % 2 == 0:
                e = layer // 2
                mix = even_mixer(x, even_w_in[e], ml_gate_bias[e], na_rpb[e], ml_norm_g[e], even_w_out[e])
            else:
                o = layer // 2
                mix = odd_mixer(x, da_w_in[o], da_w_out[o])
            x = layer_norm(DN_ALPHA * x + mix, ln_mix_g[layer], ln_mix_b[layer])
            ffn = expert_choice_ffn(x, ec_router[layer], ec_w_gate[layer], ec_w_up[layer], ec_w_down[layer])
            x = layer_norm(DN_ALPHA * x + ffn, ln_ffn_g[layer], ln_ffn_b[layer])
        return x

    y_prompt = trunk(x_prompt)
    y_sample = trunk(x_sample)
    return (y_prompt, y_sample)
```

```python
import functools

import numpy as np
import jax
import jax.numpy as jnp
from jax import lax
from jax.experimental import pallas as pl
from jax.experimental.pallas import tpu as pltpu

F32 = jnp.float32
BF16 = jnp.bfloat16

D_MODEL = 1024
SEQ = 2048
GRID_W = 64
GRID_ROWS = SEQ // GRID_W
NA_HEADS = 8
NA_W = 512
NA_WIN_R = 8
NA_WIN_C = 16
ML_HEADS = 4
ML_HEAD_DIM = 128
ML_W = 512
ML_CHUNK = 128
N_CHUNKS = SEQ // ML_CHUNK
DA_HEADS = 16
DA_W = 1024
DA_HALF = 64
ROPE_THETA = 10000.0
N_EXPERTS = 16
D_FF = 1408
LN_EPS = 1e-5
DEPTH = 2
DN_ALPHA = (2 * DEPTH) ** 0.25
LANES = 128
NEG = -1e30
VMEM_LIMIT = 56 * 1024 * 1024


def _cparams(*sem):
    return pltpu.CompilerParams(dimension_semantics=sem, vmem_limit_bytes=VMEM_LIMIT)


def _inproj_kernel(x_ref, w_ref, o_ref, *, n_chunk):
    xb = x_ref[...].astype(BF16)
    for c in range(0, w_ref.shape[1], n_chunk):
        o_ref[:, c:c + n_chunk] = jnp.dot(
            xb, w_ref[:, c:c + n_chunk], preferred_element_type=F32).astype(o_ref.dtype)


def _inproj_gates_kernel(x_ref, w_ref, wg_ref, o_ref, g_ref, *, n_chunk):
    xb = x_ref[...].astype(BF16)
    for c in range(0, w_ref.shape[1], n_chunk):
        o_ref[:, c:c + n_chunk] = jnp.dot(
            xb, w_ref[:, c:c + n_chunk], preferred_element_type=F32).astype(o_ref.dtype)
    g_ref[...] = jnp.dot(xb, wg_ref[...], preferred_element_type=F32)


def in_projection(x, w, wg=None, tm=512, n_chunk=512):
    m, k = x.shape
    n = w.shape[1]
    x_spec = pl.BlockSpec((tm, k), lambda i: (i, 0))
    w_spec = pl.BlockSpec((k, n), lambda i: (0, 0))
    o_spec = pl.BlockSpec((tm, n), lambda i: (i, 0))
    if wg is None:
        return pl.pallas_call(
            functools.partial(_inproj_kernel, n_chunk=n_chunk),
            grid=(m // tm,), in_specs=[x_spec, w_spec], out_specs=o_spec,
            out_shape=jax.ShapeDtypeStruct((m, n), BF16),
            compiler_params=_cparams("parallel"), name="in_projection")(x, w)
    return pl.pallas_call(
        functools.partial(_inproj_gates_kernel, n_chunk=n_chunk),
        grid=(m // tm,),
        in_specs=[x_spec, w_spec, pl.BlockSpec((k, LANES), lambda i: (0, 0))],
        out_specs=[o_spec, pl.BlockSpec((tm, LANES), lambda i: (i, 0))],
        out_shape=[jax.ShapeDtypeStruct((m, n), BF16), jax.ShapeDtypeStruct((m, LANES), F32)],
        compiler_params=_cparams("parallel"), name="in_projection_gates")(x, w, wg)


def _layer_norm_rows(acc, g, b):
    mu = jnp.mean(acc, axis=-1, keepdims=True)
    cen = acc - mu
    var = jnp.mean(cen * cen, axis=-1, keepdims=True)
    return cen * lax.rsqrt(var + LN_EPS) * g + b


def _outproj_ln_kernel(*refs, n_mix):
    x_ref = refs[0]
    mix_refs = refs[1:1 + n_mix]
    w_refs = refs[1 + n_mix:1 + 2 * n_mix]
    g_ref, b_ref, o_ref = refs[1 + 2 * n_mix:]
    acc = DN_ALPHA * x_ref[...]
    for m_ref, w_ref in zip(mix_refs, w_refs):
        acc = acc + jnp.dot(m_ref[...], w_ref[...], preferred_element_type=F32)
    o_ref[...] = _layer_norm_rows(acc, g_ref[...], b_ref[...])


def out_projection_ln(x, mixes, ws, g, b, tm=512):
    m, d = x.shape
    n_mix = len(mixes)
    row = lambda i: (i, 0)
    fixed = lambda i: (0, 0)
    in_specs = [pl.BlockSpec((tm, d), row)]
    in_specs += [pl.BlockSpec((tm, mx.shape[1]), row) for mx in mixes]
    in_specs += [pl.BlockSpec(w.shape, fixed) for w in ws]
    in_specs += [pl.BlockSpec((1, d), fixed), pl.BlockSpec((1, d), fixed)]
    return pl.pallas_call(
        functools.partial(_outproj_ln_kernel, n_mix=n_mix),
        grid=(m // tm,), in_specs=in_specs, out_specs=pl.BlockSpec((tm, d), row),
        out_shape=jax.ShapeDtypeStruct((m, d), F32),
        compiler_params=_cparams("parallel"), name="out_projection_ln",
    )(x, *mixes, *ws, g.reshape(1, d), b.reshape(1, d))


NA_KEYS = NA_WIN_R * GRID_W
NA_CASES = NA_WIN_R


def na_bias_table(rpb):
    j = np.arange(GRID_W)
    kc = np.arange(GRID_W)
    win_c0 = np.clip(j - NA_WIN_C // 2, 0, GRID_W - NA_WIN_C)
    valid = (kc[None, :] >= win_c0[:, None]) & (kc[None, :] < win_c0[:, None] + NA_WIN_C)
    dc = np.clip(kc[None, :] - j[:, None] + NA_WIN_C - 1, 0, 2 * NA_WIN_C - 2)
    dr = np.arange(NA_WIN_R)[None, :] + (NA_WIN_R - 1) - np.arange(NA_CASES)[:, None]
    t = rpb.astype(F32)[:, dr[:, :, None, None], dc[None, None, :, :]]
    t = jnp.where(valid[None, None, None], t, NEG)
    t = t.transpose(0, 1, 3, 2, 4).reshape(NA_HEADS // 2, 2, NA_CASES, GRID_W, NA_KEYS)
    return t.transpose(0, 2, 1, 3, 4).reshape(NA_HEADS // 2, NA_CASES, 2 * GRID_W, NA_KEYS)


def _na_kernel(q_ref, k_ref, v_ref, tbl_ref, o_ref):
    head0 = lax.broadcasted_iota(jnp.int32, (GRID_W, LANES), 1) < 64

    def row(r, carry):
        rs = jnp.clip(r - NA_WIN_R // 2, 0, GRID_ROWS - NA_WIN_R)
        q = q_ref[0, pl.ds(pl.multiple_of(r * GRID_W, GRID_W), GRID_W), :]
        zero = jnp.zeros_like(q)
        q2 = jnp.concatenate([jnp.where(head0, q, zero), jnp.where(head0, zero, q)], axis=0)
        kw = k_ref[0, pl.ds(pl.multiple_of(rs * GRID_W, GRID_W), NA_KEYS), :]
        vw = v_ref[0, pl.ds(pl.multiple_of(rs * GRID_W, GRID_W), NA_KEYS), :]
        s = lax.dot_general(q2, kw, (((1,), (1,)), ((), ())), preferred_element_type=F32)
        s = s * 0.125 + tbl_ref[0, r - rs]
        m = jnp.max(s, axis=-1, keepdims=True)
        p = jnp.exp(s - m)
        l = jnp.sum(p, axis=-1, keepdims=True)
        o = jnp.dot(p.astype(BF16), vw, preferred_element_type=F32) / l
        o_ref[0, pl.ds(pl.multiple_of(r * GRID_W, GRID_W), GRID_W), :] = jnp.where(
            head0, o[:GRID_W], o[GRID_W:]).astype(o_ref.dtype)
        return carry

    lax.fori_loop(0, GRID_ROWS, row, 0, unroll=2)


def neighbourhood_attention(proj, tbl):
    b = proj.shape[0]
    n_hp = NA_HEADS // 2
    blk = lambda off: pl.BlockSpec((1, SEQ, LANES), lambda hp, i, off=off: (i, 0, off + hp))
    return pl.pallas_call(
        _na_kernel, grid=(n_hp, b),
        in_specs=[blk(0), blk(n_hp), blk(2 * n_hp),
                  pl.BlockSpec((1, NA_CASES, 2 * GRID_W, NA_KEYS), lambda hp, i: (hp, 0, 0, 0))],
        out_specs=pl.BlockSpec((1, SEQ, LANES), lambda hp, i: (i, 0, hp)),
        out_shape=jax.ShapeDtypeStruct((b, SEQ, NA_W), BF16),
        compiler_params=_cparams("parallel", "parallel"), name="neighbourhood_attention",
    )(proj, proj, proj, tbl)


DA_BRANCH_DIL = (1, 4, 16)
DA_QB = 128


def rope_tables():
    lane = np.arange(LANES) % 64
    inv = ROPE_THETA ** (-(2.0 * (lane % 32)) / 64.0)
    ang = jnp.arange(SEQ, dtype=F32)[:, None] * jnp.asarray(inv, F32)[None, :]
    sign = jnp.asarray(np.where(lane < 32, -1.0, 1.0), F32)[None, :]
    return jnp.cos(ang), jnp.sin(ang) * sign


def _da_kernel(q_ref, k_ref, v_ref, cos_ref, sin_ref, o_ref, qs, ks, vs, acc_s, m_s, l_s):
    lane_t = lax.broadcasted_iota(jnp.int32, (SEQ, LANES), 1) % 64
    first_half = lane_t < 32

    def rope(x):
        swapped = jnp.where(first_half, pltpu.roll(x, 96, 1), pltpu.roll(x, 32, 1))
        return x * cos_ref[...] + swapped * sin_ref[...]

    qs[...] = rope(q_ref[0].astype(F32)) * 0.125
    ks[...] = rope(k_ref[0].astype(F32))
    vs[...] = v_ref[0].astype(F32)

    head0 = lax.broadcasted_iota(jnp.int32, (DA_QB, LANES), 1) < 64

    def block(g, dil, row0, krow0, nk, dq):
        qb = qs[pl.ds(row0, DA_QB, stride=dil), :].astype(BF16)
        kb = ks[pl.ds(krow0, nk, stride=dil), :].astype(BF16)
        vb = vs[pl.ds(krow0, nk, stride=dil), :].astype(BF16)
        zero = jnp.zeros_like(qb)
        q2 = jnp.concatenate([jnp.where(head0, qb, zero), jnp.where(head0, zero, qb)], axis=0)
        s = lax.dot_general(q2, kb, (((1,), (1,)), ((), ())), preferred_element_type=F32)
        qi = lax.broadcasted_iota(jnp.int32, (2 * DA_QB, nk), 0) % DA_QB + dq
        kj = lax.broadcasted_iota(jnp.int32, (2 * DA_QB, nk), 1)
        s = jnp.where(jnp.abs(qi - kj) <= DA_HALF, s, NEG)
        m = jnp.max(s, axis=-1, keepdims=True)
        p = jnp.exp(s - m)
        l = jnp.sum(p, axis=-1, keepdims=True)
        o = jnp.dot(p.astype(BF16), vb, preferred_element_type=F32)
        rows = pl.ds(row0, DA_QB, stride=dil)
        acc_s[g, rows, :] = jnp.where(head0, o[:DA_QB], o[DA_QB:])
        m_s[g, rows, :] = jnp.where(head0, m[:DA_QB], m[DA_QB:])
        l_s[g, rows, :] = jnp.where(head0, l[:DA_QB], l[DA_QB:])

    for g, dil in enumerate(DA_BRANCH_DIL):
        n_sub = SEQ // dil
        if n_sub == DA_QB:
            def body(r, carry, g=g, dil=dil):
                block(g, dil, r, r, DA_QB, 0)
                return carry
            lax.fori_loop(0, dil, body, 0)
        else:
            nb = n_sub // DA_QB
            nk = 2 * DA_QB

            def body(j, carry, g=g, dil=dil, nb=nb, nk=nk, n_sub=n_sub):
                r = j // nb
                q0 = (j % nb) * DA_QB
                k0 = jnp.clip(q0 - DA_HALF, 0, n_sub - nk)
                block(g, dil, r + dil * q0, r + dil * k0, nk, q0 - k0)
                return carry
            lax.fori_loop(0, dil * nb, body, 0)

    m_all = jnp.maximum(jnp.maximum(m_s[0], m_s[1]), m_s[2])
    num = jnp.zeros((SEQ, LANES), F32)
    den = jnp.zeros((SEQ, LANES), F32)
    for g in range(len(DA_BRANCH_DIL)):
        w = jnp.exp(m_s[g] - m_all)
        num = num + w * acc_s[g]
        den = den + w * l_s[g]
    o_ref[0] = (num / den).astype(o_ref.dtype)


def dilated_attention(proj, cos_t, sin_t):
    b = proj.shape[0]
    n_hp = DA_HEADS // 2
    blk = lambda off: pl.BlockSpec((1, SEQ, LANES), lambda i, hp, off=off: (i, 0, off + hp))
    tab = pl.BlockSpec((SEQ, LANES), lambda i, hp: (0, 0))
    nbr = len(DA_BRANCH_DIL)
    return pl.pallas_call(
        _da_kernel, grid=(b, n_hp),
        in_specs=[blk(0), blk(n_hp), blk(2 * n_hp), tab, tab],
        out_specs=pl.BlockSpec((1, SEQ, LANES), lambda i, hp: (i, 0, hp)),
        out_shape=jax.ShapeDtypeStruct((b, SEQ, DA_W), BF16),
        scratch_shapes=[pltpu.VMEM((SEQ, LANES), F32)] * 3 + [pltpu.VMEM((nbr, SEQ, LANES), F32)] * 3,
        compiler_params=_cparams("parallel", "parallel"), name="dilated_attention",
    )(proj, proj, proj, cos_t, sin_t)


ML_SCALE = ML_HEAD_DIM ** -0.5
GATE_I_FWD, GATE_I_BWD, GATE_F_FWD, GATE_F_BWD = 0, 4, 8, 12


def _exact_ones_matmul(ones_bf16, x):
    hi = x.astype(BF16)
    r1 = x - hi.astype(F32)
    mid = r1.astype(BF16)
    lo = (r1 - mid.astype(F32)).astype(BF16)
    dot = lambda t: jnp.dot(ones_bf16, t, preferred_element_type=F32)
    return dot(hi) + dot(mid) + dot(lo)


def _mlstm_chunk(q, k, v, cum_col, cum_row, i_col, i_row, total, mask, state):
    c_mat, n_vec, m_run = state
    log_d = jnp.where(mask, cum_col - cum_row + i_row, NEG)
    log_inter = cum_col + m_run
    m_t = jnp.maximum(log_inter, jnp.max(log_d, axis=-1, keepdims=True))
    w_inter = jnp.exp(log_inter - m_t)
    s = lax.dot_general(q, k, (((1,), (1,)), ((), ())), preferred_element_type=F32)
    s = s * ML_SCALE * jnp.exp(log_d - m_t)
    num = w_inter * jnp.dot(q, c_mat.astype(BF16), preferred_element_type=F32)
    num = num + jnp.dot(s.astype(BF16), v, preferred_element_type=F32)
    den = w_inter * jnp.sum(q.astype(F32) * n_vec, axis=-1, keepdims=True)
    den = den + jnp.sum(s, axis=-1, keepdims=True)
    h = num / jnp.maximum(jnp.abs(den), jnp.exp(-m_t))
    log_w = total - cum_col + i_col
    m_new = jnp.maximum(total + m_run, jnp.max(log_w, axis=0, keepdims=True))
    kw = k.astype(F32) * (jnp.exp(log_w - m_new) * ML_SCALE)
    decay = jnp.exp(total + m_run - m_new)
    c_new = decay * c_mat + lax.dot_general(
        kw.astype(BF16), v, (((0,), (0,)), ((), ())), preferred_element_type=F32)
    n_new = decay * n_vec + jnp.sum(kw, axis=0, keepdims=True)
    return h, (c_new, n_new, m_new)


def _mlstm_kernel(q_ref, k_ref, v_ref, og_ref, g_ref, gb_ref, ng_ref, out_ref, col_s, row_s, hf_s, hb_s):
    L = ML_CHUNK
    ri = lax.broadcasted_iota(jnp.int32, (L, L), 0)
    ci = lax.broadcasted_iota(jnp.int32, (L, L), 1)
    lower = ci <= ri
    upper = ci >= ri
    lower_m = jnp.where(lower, 1.0, 0.0).astype(BF16)
    upper_m = jnp.where(upper, 1.0, 0.0).astype(BF16)
    lane = lax.broadcasted_iota(jnp.int32, (L, LANES), 1)

    def prep(c, carry):
        rows = pl.ds(pl.multiple_of(c * L, L), L)
        g = g_ref[0, rows, :] + gb_ref[...]
        lf = jnp.minimum(g, 0.0) - jnp.log1p(jnp.exp(-jnp.abs(g)))
        cum_f = _exact_ones_matmul(lower_m, lf)
        suf_b = _exact_ones_matmul(upper_m, lf)
        colv = jnp.where(lane < GATE_F_FWD, g, jnp.where(lane < GATE_F_BWD, cum_f, suf_b))
        col_s[rows, :] = colv
        row_s[c] = colv.T
        return carry

    lax.fori_loop(0, N_CHUNKS, prep, 0)

    zero_state = (jnp.zeros((ML_HEAD_DIM, ML_HEAD_DIM), F32), jnp.zeros((1, ML_HEAD_DIM), F32),
                  jnp.zeros((1, 1), F32))
    for h in range(ML_HEADS):
        hl = slice(ML_HEAD_DIM * h, ML_HEAD_DIM * (h + 1))

        def body(c, carry, h=h, hl=hl):
            st_f, st_b = carry
            rows = pl.ds(pl.multiple_of(c * L, L), L)
            colv = col_s[rows, :]
            rowv = row_s[c]
            jf, ji = GATE_F_FWD + h, GATE_I_FWD + h
            h_f, st_f = _mlstm_chunk(
                q_ref[0, rows, hl], k_ref[0, rows, hl], v_ref[0, rows, hl],
                colv[:, jf:jf + 1], rowv[jf:jf + 1, :], colv[:, ji:ji + 1], rowv[ji:ji + 1, :],
                rowv[jf:jf + 1, L - 1:L], lower, st_f)
            hf_s[rows, hl] = h_f
            cb = N_CHUNKS - 1 - c
            rows_b = pl.ds(pl.multiple_of(cb * L, L), L)
            colv = col_s[rows_b, :]
            rowv = row_s[cb]
            jf, ji = GATE_F_BWD + h, GATE_I_BWD + h
            h_b, st_b = _mlstm_chunk(
                q_ref[0, rows_b, hl], k_ref[0, rows_b, hl], v_ref[0, rows_b, hl],
                colv[:, jf:jf + 1], rowv[jf:jf + 1, :], colv[:, ji:ji + 1], rowv[ji:ji + 1, :],
                rowv[jf:jf + 1, 0:1], upper, st_b)
            hb_s[rows_b, hl] = h_b
            return st_f, st_b

        lax.fori_loop(0, N_CHUNKS, body, (zero_state, zero_state))

    tr = 256
    for r0 in range(0, SEQ, tr):
        for h in range(ML_HEADS):
            hl = slice(ML_HEAD_DIM * h, ML_HEAD_DIM * (h + 1))
            hh = hf_s[r0:r0 + tr, hl] + hb_s[r0:r0 + tr, hl]
            mu = jnp.mean(hh, axis=-1, keepdims=True)
            cen = hh - mu
            var = jnp.mean(cen * cen, axis=-1, keepdims=True)
            hn = cen * lax.rsqrt(var + LN_EPS) * ng_ref[:, hl]
            og = og_ref[0, r0:r0 + tr, hl].astype(F32)
            out_ref[0, r0:r0 + tr, hl] = (hn / (1.0 + jnp.exp(-og))).astype(out_ref.dtype)


def mlstm_mixer(proj, gates, gate_bias, norm_g):
    b = proj.shape[0]
    c0 = (3 * NA_W) // ML_W
    blk = lambda off: pl.BlockSpec((1, SEQ, ML_W), lambda i, off=off: (i, 0, off))
    gb = jnp.pad(gate_bias.astype(F32), (0, LANES - gate_bias.shape[0])).reshape(1, LANES)
    return pl.pallas_call(
        _mlstm_kernel, grid=(b,),
        in_specs=[blk(c0), blk(c0 + 1), blk(c0 + 2), blk(c0 + 3),
                  pl.BlockSpec((1, SEQ, LANES), lambda i: (i, 0, 0)),
                  pl.BlockSpec((1, LANES), lambda i: (0, 0)),
                  pl.BlockSpec((1, ML_W), lambda i: (0, 0))],
        out_specs=pl.BlockSpec((1, SEQ, ML_W), lambda i: (i, 0, 0)),
        out_shape=jax.ShapeDtypeStruct((b, SEQ, ML_W), BF16),
        scratch_shapes=[pltpu.VMEM((SEQ, LANES), F32), pltpu.VMEM((N_CHUNKS, LANES, ML_CHUNK), F32),
                        pltpu.VMEM((SEQ, ML_W), F32), pltpu.VMEM((SEQ, ML_W), F32)],
        compiler_params=_cparams("parallel"), name="mlstm_mixer",
    )(proj, proj, proj, proj, gates, gb, norm_g.reshape(1, ML_W).astype(F32))


def _router_kernel(x_ref, w_ref, aff_ref):
    logits = lax.dot_general(w_ref[...], x_ref[...], (((1,), (1,)), ((), ())),
                             precision=lax.Precision.HIGHEST, preferred_element_type=F32)
    z = jnp.exp(logits - jnp.max(logits, axis=0, keepdims=True))
    aff_ref[...] = z / jnp.sum(z, axis=0, keepdims=True)


def router_affinities(x, w_router_t, tm=1024):
    n, d = x.shape
    return pl.pallas_call(
        _router_kernel, grid=(n // tm,),
        in_specs=[pl.BlockSpec((tm, d), lambda i: (i, 0)), pl.BlockSpec((N_EXPERTS, d), lambda i: (0, 0))],
        out_specs=pl.BlockSpec((N_EXPERTS, tm), lambda i: (0, i)),
        out_shape=jax.ShapeDtypeStruct((N_EXPERTS, n), F32),
        compiler_params=_cparams("parallel"), name="router_affinities")(x, w_router_t)


def _tri_matrices(r):
    li = lax.broadcasted_iota(jnp.int32, (LANES, LANES), 0)
    lj = lax.broadcasted_iota(jnp.int32, (LANES, LANES), 1)
    tri_u = jnp.where(li <= lj, 1.0, 0.0).astype(BF16)
    ri = lax.broadcasted_iota(jnp.int32, (r, r), 0)
    rj = lax.broadcasted_iota(jnp.int32, (r, r), 1)
    tri_l = jnp.where(rj < ri, 1.0, 0.0).astype(BF16)
    return tri_u, tri_l


def _prefix_counts(mask, tri_u, tri_l):
    r = mask.shape[0]
    within = jnp.dot(mask.astype(BF16), tri_u, preferred_element_type=F32)
    rowtot = within[:, LANES - 1:LANES]
    hi = jnp.floor(rowtot * (1.0 / 16.0))
    lo = rowtot - 16.0 * hi
    hi_b = jnp.broadcast_to(hi, (r, LANES)).astype(BF16)
    lo_b = jnp.broadcast_to(lo, (r, LANES)).astype(BF16)
    rowoff = 16.0 * jnp.dot(tri_l, hi_b, preferred_element_type=F32) + jnp.dot(
        tri_l, lo_b, preferred_element_type=F32)
    return within - mask + rowoff, within, rowoff, rowtot


def _select_kernel(aff_ref, sel_ref, *, cap):
    r = aff_ref.shape[1]
    tri_u, tri_l = _tri_matrices(r)
    bits = pltpu.bitcast(aff_ref[0], jnp.int32)

    def count(m):
        c = jnp.sum(jnp.where(m, 1.0, 0.0), axis=1, keepdims=True)
        return jnp.sum(c, axis=0, keepdims=True)

    def bisect(i, prefix):
        cand = prefix | jnp.left_shift(jnp.int32(1), 30 - i)
        return jnp.where(count(bits >= cand) >= cap, cand, prefix)

    thr = lax.fori_loop(0, 31, bisect, jnp.zeros((1, 1), jnp.int32))
    gt = bits > thr
    eq = bits == thr
    need = cap - count(gt)
    rank_eq, _, _, _ = _prefix_counts(jnp.where(eq, 1.0, 0.0), tri_u, tri_l)
    sel_ref[0] = jnp.where(gt | (eq & (rank_eq < need)), 1.0, 0.0)


def select_tokens(aff3, cap):
    e, r, _ = aff3.shape
    blk = pl.BlockSpec((1, r, LANES), lambda i: (i, 0, 0))
    return pl.pallas_call(
        functools.partial(_select_kernel, cap=cap), grid=(e,), in_specs=[blk], out_specs=blk,
        out_shape=jax.ShapeDtypeStruct((e, r, LANES), F32),
        compiler_params=_cparams("parallel"), name="select_tokens")(aff3)


def _lists_kernel(sel_ref, idx_ref, dst_ref, dstw_ref, ts_ref, ts_s, er_s, *, cap, st):
    e = pl.program_id(0)
    r = sel_ref.shape[1]
    tri_u, tri_l = _tri_matrices(r)

    @pl.when(e == 0)
    def _():
        cnt = sel_ref[0]
        for k in range(1, N_EXPERTS):
            cnt = cnt + sel_ref[k]
        ts, _, _, _ = _prefix_counts(cnt, tri_u, tri_l)
        ts_s[...] = ts
        ts_ref[...] = ts
        er_s[...] = jnp.zeros_like(er_s)

    sel = sel_ref[e]
    _, within, rowoff, rowtot = _prefix_counts(sel, tri_u, tri_l)
    dst = ts_s[...] + er_s[...]
    er_s[...] = er_s[...] + sel
    dstw_ref[0] = jnp.where(sel > 0.0, dst, -1.0)

    d2 = jnp.floor(dst * (1.0 / 65536.0))
    rem = dst - 65536.0 * d2
    d1 = jnp.floor(rem * (1.0 / 256.0))
    d0 = rem - 256.0 * d1
    rhs = jnp.concatenate([within, d0, d1, d2], axis=1).astype(BF16)
    rowoff_row = rowoff.T[0:1, :]
    rowend_row = rowoff_row + jnp.broadcast_to(rowtot, (r, LANES)).T[0:1, :]
    rho_row = lax.broadcasted_iota(jnp.int32, (1, r), 1).astype(F32)
    lane = lax.broadcasted_iota(jnp.int32, (st, LANES), 1).astype(F32)
    eye = lax.broadcasted_iota(jnp.int32, (LANES, LANES), 0) == lax.broadcasted_iota(
        jnp.int32, (LANES, LANES), 1)

    for t in range(cap // st):
        s_col = (t * st + lax.broadcasted_iota(jnp.int32, (st, 1), 0)).astype(F32)
        in_row = (rowoff_row <= s_col) & (s_col < rowend_row)
        got = jnp.dot(jnp.where(in_row, 1.0, 0.0).astype(BF16), rhs, preferred_element_type=F32)
        base = jnp.sum(jnp.where(in_row, rowoff_row, 0.0), axis=1, keepdims=True)
        rho = jnp.sum(jnp.where(in_row, rho_row, 0.0), axis=1, keepdims=True)
        local = s_col - base
        lam = jnp.sum(jnp.where(got[:, 0:LANES] <= local, 1.0, 0.0), axis=1, keepdims=True)
        pair = got[:, LANES:2 * LANES] + 256.0 * got[:, 2 * LANES:3 * LANES] + 65536.0 * got[:, 3 * LANES:]
        dval = jnp.sum(jnp.where(lane == lam, pair, 0.0), axis=1, keepdims=True)
        ival = rho * float(LANES) + lam
        for j in range(st // LANES):
            seg = slice(j * LANES, (j + 1) * LANES)
            row = t * (st // LANES) + j
            idx_ref[0, row:row + 1, :] = jnp.sum(
                jnp.where(eye, ival[seg], 0.0), axis=0, keepdims=True).astype(jnp.int32)
            dst_ref[0, row:row + 1, :] = jnp.sum(
                jnp.where(eye, dval[seg], 0.0), axis=0, keepdims=True).astype(jnp.int32)


def build_lists(sel3, cap):
    e, r, _ = sel3.shape
    st = min(512, cap)
    lst = pl.BlockSpec((1, cap // LANES, LANES), lambda i: (i, 0, 0))
    return pl.pallas_call(
        functools.partial(_lists_kernel, cap=cap, st=st), grid=(e,),
        in_specs=[pl.BlockSpec((e, r, LANES), lambda i: (0, 0, 0))],
        out_specs=[lst, lst, pl.BlockSpec((1, r, LANES), lambda i: (i, 0, 0)),
                   pl.BlockSpec((r, LANES), lambda i: (0, 0))],
        out_shape=[jax.ShapeDtypeStruct((e, cap // LANES, LANES), jnp.int32),
                   jax.ShapeDtypeStruct((e, cap // LANES, LANES), jnp.int32),
                   jax.ShapeDtypeStruct((e, r, LANES), F32), jax.ShapeDtypeStruct((r, LANES), F32)],
        scratch_shapes=[pltpu.VMEM((r, LANES), F32), pltpu.VMEM((r, LANES), F32)],
        compiler_params=_cparams("arbitrary"), name="build_lists")(sel3)


def _ffn_kernel(idx_ref, idxn_ref, dst_ref, x_hbm, wg_ref, wu_ref, wd_ref, z_hbm,
                xbuf, ybuf, gsem, ssem, *, tm, n_steps):
    s = pl.program_id(0) * pl.num_programs(1) + pl.program_id(1)
    slot = s % 2

    def gather(ids, buf_slot):
        def issue(i, carry):
            pltpu.make_async_copy(x_hbm.at[pl.ds(ids[0, 0, i], 1), :],
                                  xbuf.at[buf_slot, pl.ds(i, 1), :], gsem.at[buf_slot]).start()
            return carry
        lax.fori_loop(0, tm, issue, 0)

    @pl.when(s == 0)
    def _():
        gather(idx_ref, 0)

    @pl.when(s + 1 < n_steps)
    def _():
        gather(idxn_ref, 1 - slot)

    pltpu.make_async_copy(x_hbm.at[pl.ds(0, tm), :], xbuf.at[slot], gsem.at[slot]).wait()

    xb = xbuf[slot].astype(BF16)
    gate = jnp.dot(xb, wg_ref[0], preferred_element_type=F32)
    up = jnp.dot(xb, wu_ref[0], preferred_element_type=F32)
    h = (gate / (1.0 + jnp.exp(-gate)) * up).astype(BF16)
    y = jnp.dot(h, wd_ref[0], preferred_element_type=F32)

    @pl.when(s >= 2)
    def _():
        pltpu.make_async_copy(ybuf.at[slot], z_hbm.at[pl.ds(0, tm), :], ssem.at[slot]).wait()

    ybuf[slot] = y

    def scatter(i, carry):
        pltpu.make_async_copy(ybuf.at[slot, pl.ds(i, 1), :],
                              z_hbm.at[pl.ds(dst_ref[0, 0, i], 1), :], ssem.at[slot]).start()
        return carry
    lax.fori_loop(0, tm, scatter, 0)

    @pl.when(s == n_steps - 1)
    def _():
        pltpu.make_async_copy(ybuf.at[slot], z_hbm.at[pl.ds(0, tm), :], ssem.at[slot]).wait()
        if n_steps > 1:
            pltpu.make_async_copy(ybuf.at[1 - slot], z_hbm.at[pl.ds(0, tm), :], ssem.at[1 - slot]).wait()


def expert_ffn(x, idx, dst, w_gate, w_up, w_down, tm=512):
    n, d = x.shape
    e, cap = idx.shape
    tm = min(tm, cap)
    nt = cap // tm
    n_steps = e * nt
    idx3 = idx.reshape(n_steps, 1, tm)
    dst3 = dst.reshape(n_steps, 1, tm)
    smem = lambda imap: pl.BlockSpec((1, 1, tm), imap, memory_space=pltpu.SMEM)
    cur = lambda i, t: (i * nt + t, 0, 0)
    nxt = lambda i, t: (jnp.minimum(i * nt + t + 1, n_steps - 1), 0, 0)
    wspec = lambda w: pl.BlockSpec((1,) + w.shape[1:], lambda i, t: (i, 0, 0))
    return pl.pallas_call(
        functools.partial(_ffn_kernel, tm=tm, n_steps=n_steps), grid=(e, nt),
        in_specs=[smem(cur), smem(nxt), smem(cur), pl.BlockSpec(memory_space=pl.ANY),
                  wspec(w_gate), wspec(w_up), wspec(w_down)],
        out_specs=pl.BlockSpec(memory_space=pl.ANY),
        out_shape=jax.ShapeDtypeStruct((e * cap, d), F32),
        scratch_shapes=[pltpu.VMEM((2, tm, d), F32), pltpu.VMEM((2, tm, d), F32),
                        pltpu.SemaphoreType.DMA((2,)), pltpu.SemaphoreType.DMA((2,))],
        compiler_params=_cparams("arbitrary", "arbitrary"), name="expert_ffn",
    )(idx3, idx3, dst3, x, w_gate, w_up, w_down)


SUBLANES = 8
COMBINE_ZB = 256
COMBINE_ZC = COMBINE_ZB - SUBLANES


def _combine_kernel(ts_ref, x_ref, dstt_ref, gate_ref, g_ref, b_ref, z_hbm, o_ref, zbuf, sem, *, z_rows):
    i = pl.program_id(0)
    tt = x_ref.shape[0]
    z0 = ts_ref[i]
    z1 = ts_ref[i + 1]
    n_chunks = (z1 - z0 + COMBINE_ZC - 1) // COMBINE_ZC
    col = lax.broadcasted_iota(jnp.int32, (tt, COMBINE_ZB), 1).astype(F32)
    dstt = dstt_ref[...]
    gate = gate_ref[...]

    def chunk(c, acc):
        lo = z0 + c * COMBINE_ZC
        start = pl.multiple_of(jnp.minimum((lo // SUBLANES) * SUBLANES, z_rows - COMBINE_ZB), SUBLANES)
        cp = pltpu.make_async_copy(z_hbm.at[pl.ds(start, COMBINE_ZB), :], zbuf, sem.at[0])
        cp.start()
        lo_f = lo.astype(F32)
        hi_f = (lo + COMBINE_ZC).astype(F32)
        rel0 = start.astype(F32)
        a = jnp.zeros((tt, COMBINE_ZB), F32)
        for k in range(N_EXPERTS):
            d = dstt[:, k:k + 1]
            rel = jnp.where((d >= lo_f) & (d < hi_f), d - rel0, -1.0)
            a = a + jnp.where(rel == col, gate[:, k:k + 1], 0.0)
        cp.wait()
        return acc + jnp.dot(a.astype(BF16), zbuf[...].astype(BF16), preferred_element_type=F32)

    ffn = lax.fori_loop(0, n_chunks, chunk, jnp.zeros((tt, D_MODEL), F32))
    o_ref[...] = _layer_norm_rows(DN_ALPHA * x_ref[...] + ffn, g_ref[...], b_ref[...])


def combine_ln(x, z, tile_start, dst_t, gate_t, g, b, tt=256):
    n, d = x.shape
    row = lambda i, ts: (i, 0)
    fixed = lambda i, ts: (0, 0)
    grid_spec = pltpu.PrefetchScalarGridSpec(
        num_scalar_prefetch=1, grid=(n // tt,),
        in_specs=[pl.BlockSpec((tt, d), row), pl.BlockSpec((tt, N_EXPERTS), row),
                  pl.BlockSpec((tt, N_EXPERTS), row), pl.BlockSpec((1, d), fixed),
                  pl.BlockSpec((1, d), fixed), pl.BlockSpec(memory_space=pl.ANY)],
        out_specs=pl.BlockSpec((tt, d), row),
        scratch_shapes=[pltpu.VMEM((COMBINE_ZB, d), F32), pltpu.SemaphoreType.DMA((1,))])
    return pl.pallas_call(
        functools.partial(_combine_kernel, z_rows=z.shape[0]), grid_spec=grid_spec,
        out_shape=jax.ShapeDtypeStruct((n, d), F32),
        compiler_params=_cparams("arbitrary"), name="combine_ln",
    )(tile_start, x, dst_t, gate_t, g.reshape(1, d), b.reshape(1, d), z)


def moe_layer(x, w_router, w_gate, w_up, w_down, g, b, tt=256):
    n, _ = x.shape
    r = n // LANES
    cap = 2 * n // N_EXPERTS
    aff = router_affinities(x, w_router.T)
    sel3 = select_tokens(aff.reshape(N_EXPERTS, r, LANES), cap)
    idx, dst, dstw, ts = build_lists(sel3, cap)
    z = expert_ffn(x, idx.reshape(N_EXPERTS, cap), dst.reshape(N_EXPERTS, cap), w_gate, w_up, w_down)
    tile_start = jnp.concatenate(
        [ts.reshape(n)[::tt], jnp.full((1,), N_EXPERTS * cap, F32)]).astype(jnp.int32)
    dst_t = dstw.reshape(N_EXPERTS, n).T
    gate_t = aff.T
    return combine_ln(x, z, tile_start, dst_t, gate_t, g, b, tt=tt)


def kernel(x_prompt, x_sample, even_w_in, ml_gate_bias, na_rpb, ml_norm_g, even_w_out, da_w_in, da_w_out,
           ln_mix_g, ln_mix_b, ec_router, ec_w_gate, ec_w_up, ec_w_down, ln_ffn_g, ln_ffn_b):
    w_even = even_w_in[0][:, :3584].astype(BF16)
    w_gates = jnp.pad(even_w_in[0][:, 3584:], ((0, 0), (0, LANES - 16))).astype(BF16)
    w_out_a = even_w_out[0][:NA_W].astype(BF16)
    w_out_b = even_w_out[0][NA_W:].astype(BF16)
    w_odd = da_w_in[0].astype(BF16)
    w_odd_out = da_w_out[0].astype(BF16)
    tbl = na_bias_table(na_rpb[0])
    cos_t, sin_t = rope_tables()
    moe_w = [(ec_router[l], ec_w_gate[l].astype(BF16), ec_w_up[l].astype(BF16), ec_w_down[l].astype(BF16),
              ln_ffn_g[l], ln_ffn_b[l]) for l in range(DEPTH)]

    def trunk(x):
        b = x.shape[0]
        xt = x.reshape(b * SEQ, D_MODEL)
        proj, gates = in_projection(xt, w_even, w_gates)
        proj = proj.reshape(b, SEQ, -1)
        ya = neighbourhood_attention(proj, tbl)
        yb = mlstm_mixer(proj, gates.reshape(b, SEQ, LANES), ml_gate_bias[0], ml_norm_g[0])
        xt = out_projection_ln(xt, [ya.reshape(b * SEQ, NA_W), yb.reshape(b * SEQ, ML_W)],
                               [w_out_a, w_out_b], ln_mix_g[0], ln_mix_b[0])
        xt = moe_layer(xt, *moe_w[0])
        proj = in_projection(xt, w_odd).reshape(b, SEQ, -1)
        yc = dilated_attention(proj, cos_t, sin_t)
        xt = out_projection_ln(xt, [yc.reshape(b * SEQ, DA_W)], [w_odd_out], ln_mix_g[1], ln_mix_b[1])
        xt = moe_layer(xt, *moe_w[1])
        return xt.reshape(b, SEQ, D_MODEL)

    return trunk(x_prompt), trunk(x_sample)
```

```python
import functools

import numpy as np
import jax
import jax.numpy as jnp
from jax import lax
from jax.experimental import pallas as pl
from jax.experimental.pallas import tpu as pltpu

F32 = jnp.float32
BF16 = jnp.bfloat16

D_MODEL = 1024
SEQ = 2048
GRID_W = 64
GRID_ROWS = SEQ // GRID_W
NA_HEADS = 8
NA_W = 512
NA_WIN_R = 8
NA_WIN_C = 16
ML_HEADS = 4
ML_HEAD_DIM = 128
ML_W = 512
ML_CHUNK = 128
N_CHUNKS = SEQ // ML_CHUNK
DA_HEADS = 16
DA_W = 1024
DA_HALF = 64
ROPE_THETA = 10000.0
N_EXPERTS = 16
D_FF = 1408
LN_EPS = 1e-5
DEPTH = 2
DN_ALPHA = (2 * DEPTH) ** 0.25
LANES = 128
NEG = -1e30
VMEM_LIMIT = 56 * 1024 * 1024


def _cparams(*sem):
    return pltpu.CompilerParams(dimension_semantics=sem, vmem_limit_bytes=VMEM_LIMIT)


def _inproj_kernel(x_ref, w_ref, o_ref, *, n_chunk):
    xb = x_ref[...].astype(BF16)
    for c in range(0, w_ref.shape[1], n_chunk):
        o_ref[:, c:c + n_chunk] = jnp.dot(
            xb, w_ref[:, c:c + n_chunk], preferred_element_type=F32).astype(o_ref.dtype)


def _inproj_gates_kernel(x_ref, w_ref, wg_ref, o_ref, g_ref, *, n_chunk):
    xb = x_ref[...].astype(BF16)
    for c in range(0, w_ref.shape[1], n_chunk):
        o_ref[:, c:c + n_chunk] = jnp.dot(
            xb, w_ref[:, c:c + n_chunk], preferred_element_type=F32).astype(o_ref.dtype)
    g_ref[...] = jnp.dot(xb, wg_ref[...], preferred_element_type=F32)


def in_projection(x, w, wg=None, tm=512, n_chunk=512):
    m, k = x.shape
    n = w.shape[1]
    x_spec = pl.BlockSpec((tm, k), lambda i: (i, 0))
    w_spec = pl.BlockSpec((k, n), lambda i: (0, 0))
    o_spec = pl.BlockSpec((tm, n), lambda i: (i, 0))
    if wg is None:
        return pl.pallas_call(
            functools.partial(_inproj_kernel, n_chunk=n_chunk),
            grid=(m // tm,), in_specs=[x_spec, w_spec], out_specs=o_spec,
            out_shape=jax.ShapeDtypeStruct((m, n), BF16),
            compiler_params=_cparams("parallel"), name="in_projection")(x, w)
    return pl.pallas_call(
        functools.partial(_inproj_gates_kernel, n_chunk=n_chunk),
        grid=(m // tm,),
        in_specs=[x_spec, w_spec, pl.BlockSpec((k, LANES), lambda i: (0, 0))],
        out_specs=[o_spec, pl.BlockSpec((tm, LANES), lambda i: (i, 0))],
        out_shape=[jax.ShapeDtypeStruct((m, n), BF16), jax.ShapeDtypeStruct((m, LANES), F32)],
        compiler_params=_cparams("parallel"), name="in_projection_gates")(x, w, wg)


def _layer_norm_rows(acc, g, b):
    mu = jnp.mean(acc, axis=-1, keepdims=True)
    cen = acc - mu
    var = jnp.mean(cen * cen, axis=-1, keepdims=True)
    return cen * lax.rsqrt(var + LN_EPS) * g + b


def _outproj_ln_kernel(*refs, n_mix):
    x_ref = refs[0]
    mix_refs = refs[1:1 + n_mix]
    w_refs = refs[1 + n_mix:1 + 2 * n_mix]
    g_ref, b_ref, o_ref = refs[1 + 2 * n_mix:]
    acc = DN_ALPHA * x_ref[...]
    for m_ref, w_ref in zip(mix_refs, w_refs):
        acc = acc + jnp.dot(m_ref[...], w_ref[...], preferred_element_type=F32)
    o_ref[...] = _layer_norm_rows(acc, g_ref[...], b_ref[...])


def out_projection_ln(x, mixes, ws, g, b, tm=512):
    m, d = x.shape
    n_mix = len(mixes)
    row = lambda i: (i, 0)
    fixed = lambda i: (0, 0)
    in_specs = [pl.BlockSpec((tm, d), row)]
    in_specs += [pl.BlockSpec((tm, mx.shape[1]), row) for mx in mixes]
    in_specs += [pl.BlockSpec(w.shape, fixed) for w in ws]
    in_specs += [pl.BlockSpec((1, d), fixed), pl.BlockSpec((1, d), fixed)]
    return pl.pallas_call(
        functools.partial(_outproj_ln_kernel, n_mix=n_mix),
        grid=(m // tm,), in_specs=in_specs, out_specs=pl.BlockSpec((tm, d), row),
        out_shape=jax.ShapeDtypeStruct((m, d), F32),
        compiler_params=_cparams("parallel"), name="out_projection_ln",
    )(x, *mixes, *ws, g.reshape(1, d), b.reshape(1, d))


NA_KEYS = NA_WIN_R * GRID_W
NA_CASES = NA_WIN_R


def na_bias_table(rpb):
    j = np.arange(GRID_W)
    kc = np.arange(GRID_W)
    win_c0 = np.clip(j - NA_WIN_C // 2, 0, GRID_W - NA_WIN_C)
    valid = (kc[None, :] >= win_c0[:, None]) & (kc[None, :] < win_c0[:, None] + NA_WIN_C)
    dc = np.clip(kc[None, :] - j[:, None] + NA_WIN_C - 1, 0, 2 * NA_WIN_C - 2)
    n_dc = 2 * NA_WIN_C - 1
    pick = jnp.asarray(dc[:, :, None] == np.arange(n_dc)[None, None, :], F32)
    rows = jnp.einsum('hrd,jkd->hrjk', rpb.astype(F32), pick, precision=lax.Precision.HIGHEST)
    rows = jnp.where(valid[None, None], rows, NEG)
    t = jnp.stack([rows[:, NA_WIN_R - 1 - ci:2 * NA_WIN_R - 1 - ci] for ci in range(NA_CASES)], axis=1)
    t = t.transpose(0, 1, 3, 2, 4).reshape(NA_HEADS // 2, 2, NA_CASES, GRID_W, NA_KEYS)
    return t.transpose(0, 2, 1, 3, 4).reshape(NA_HEADS // 2, NA_CASES, 2 * GRID_W, NA_KEYS)


def _na_kernel(q_ref, k_ref, v_ref, tbl_ref, o_ref):
    head0 = lax.broadcasted_iota(jnp.int32, (GRID_W, LANES), 1) < 64

    def row(r, carry):
        rs = jnp.clip(r - NA_WIN_R // 2, 0, GRID_ROWS - NA_WIN_R)
        q = q_ref[0, pl.ds(pl.multiple_of(r * GRID_W, GRID_W), GRID_W), :]
        zero = jnp.zeros_like(q)
        q2 = jnp.concatenate([jnp.where(head0, q, zero), jnp.where(head0, zero, q)], axis=0)
        kw = k_ref[0, pl.ds(pl.multiple_of(rs * GRID_W, GRID_W), NA_KEYS), :]
        vw = v_ref[0, pl.ds(pl.multiple_of(rs * GRID_W, GRID_W), NA_KEYS), :]
        s = lax.dot_general(q2, kw, (((1,), (1,)), ((), ())), preferred_element_type=F32)
        s = s * 0.125 + tbl_ref[0, r - rs]
        m = jnp.max(s, axis=-1, keepdims=True)
        p = jnp.exp(s - m)
        l = jnp.sum(p, axis=-1, keepdims=True)
        o = jnp.dot(p.astype(BF16), vw, preferred_element_type=F32) / l
        o_ref[0, pl.ds(pl.multiple_of(r * GRID_W, GRID_W), GRID_W), :] = jnp.where(
            head0, o[:GRID_W], o[GRID_W:]).astype(o_ref.dtype)
        return carry

    lax.fori_loop(0, GRID_ROWS, row, 0, unroll=4)


def neighbourhood_attention(proj, tbl):
    b = proj.shape[0]
    n_hp = NA_HEADS // 2
    blk = lambda off: pl.BlockSpec((1, SEQ, LANES), lambda hp, i, off=off: (i, 0, off + hp))
    return pl.pallas_call(
        _na_kernel, grid=(n_hp, b),
        in_specs=[blk(0), blk(n_hp), blk(2 * n_hp),
                  pl.BlockSpec((1, NA_CASES, 2 * GRID_W, NA_KEYS), lambda hp, i: (hp, 0, 0, 0))],
        out_specs=pl.BlockSpec((1, SEQ, LANES), lambda hp, i: (i, 0, hp)),
        out_shape=jax.ShapeDtypeStruct((b, SEQ, NA_W), BF16),
        compiler_params=_cparams("parallel", "parallel"), name="neighbourhood_attention",
    )(proj, proj, proj, tbl)


DA_BRANCH_DIL = (1, 4, 16)
DA_QB = 128


def rope_tables():
    lane = np.arange(LANES) % 64
    inv = ROPE_THETA ** (-(2.0 * (lane % 32)) / 64.0)
    ang = jnp.arange(SEQ, dtype=F32)[:, None] * jnp.asarray(inv, F32)[None, :]
    sign = jnp.asarray(np.where(lane < 32, -1.0, 1.0), F32)[None, :]
    return jnp.cos(ang), jnp.sin(ang) * sign


LOG2E = 1.4426950408889634
LN2 = 0.6931471805599453
DA_Q_SCALE = 64 ** -0.5 * LOG2E
DA_MASK_CASES = 3


def da_mask_table():
    i = np.arange(2 * DA_QB)[None, :, None] % DA_QB
    j = np.arange(2 * DA_QB)[None, None, :]
    c = np.arange(DA_MASK_CASES)[:, None, None]
    return jnp.asarray(np.where(np.abs(i + DA_HALF * c - j) <= DA_HALF, 0.0, NEG), F32)


def _da_kernel(q_ref, k_ref, v_ref, cos_ref, sin_ref, mask_ref, o_ref, qs, ks, vs, acc_s, lse_s):
    lane_t = lax.broadcasted_iota(jnp.int32, (SEQ, LANES), 1) % 64
    first_half = lane_t < 32

    def rope(x):
        swapped = jnp.where(first_half, pltpu.roll(x, 96, 1), pltpu.roll(x, 32, 1))
        return x * cos_ref[...] + swapped * sin_ref[...]

    qs[...] = rope(q_ref[0].astype(F32)) * DA_Q_SCALE
    ks[...] = rope(k_ref[0].astype(F32))
    vs[...] = v_ref[0].astype(F32)

    head0 = lax.broadcasted_iota(jnp.int32, (DA_QB, LANES), 1) < 64

    def block(g, dil, row0, krow0, nk, case):
        qb = qs[pl.ds(row0, DA_QB, stride=dil), :].astype(BF16)
        kb = ks[pl.ds(krow0, nk, stride=dil), :].astype(BF16)
        vb = vs[pl.ds(krow0, nk, stride=dil), :].astype(BF16)
        zero = jnp.zeros_like(qb)
        q2 = jnp.concatenate([jnp.where(head0, qb, zero), jnp.where(head0, zero, qb)], axis=0)
        s = lax.dot_general(q2, kb, (((1,), (1,)), ((), ())), preferred_element_type=F32)
        s = s + mask_ref[case, :, 0:nk]
        m = jnp.max(s, axis=-1, keepdims=True)
        p = jnp.exp2(s - m)
        l = jnp.sum(p, axis=-1, keepdims=True)
        o = jnp.dot(p.astype(BF16), vb, preferred_element_type=F32) / l
        lse = m + jnp.log(l) * (1.0 / LN2)
        rows = pl.ds(row0, DA_QB, stride=dil)
        acc_s[g, rows, :] = jnp.where(head0, o[:DA_QB], o[DA_QB:])
        lse_s[g, rows, :] = jnp.where(head0, lse[:DA_QB], lse[DA_QB:])

    for g, dil in enumerate(DA_BRANCH_DIL):
        n_sub = SEQ // dil
        if n_sub == DA_QB:
            def body(r, carry, g=g, dil=dil):
                block(g, dil, r, r, DA_QB, 0)
                return carry
            lax.fori_loop(0, dil, body, 0, unroll=4)
        else:
            nb = n_sub // DA_QB
            nk = 2 * DA_QB

            def body(j, carry, g=g, dil=dil, nb=nb, nk=nk, n_sub=n_sub):
                r = j // nb
                q0 = (j % nb) * DA_QB
                k0 = jnp.clip(q0 - DA_HALF, 0, n_sub - nk)
                block(g, dil, r + dil * q0, r + dil * k0, nk, (q0 - k0) // DA_HALF)
                return carry
            lax.fori_loop(0, dil * nb, body, 0, unroll=4)

    lse_all = jnp.maximum(jnp.maximum(lse_s[0], lse_s[1]), lse_s[2])
    num = jnp.zeros((SEQ, LANES), F32)
    den = jnp.zeros((SEQ, LANES), F32)
    for g in range(len(DA_BRANCH_DIL)):
        w = jnp.exp2(lse_s[g] - lse_all)
        num = num + w * acc_s[g]
        den = den + w
    o_ref[0] = (num / den).astype(o_ref.dtype)


def dilated_attention(proj, cos_t, sin_t, mask_t):
    b = proj.shape[0]
    n_hp = DA_HEADS // 2
    blk = lambda off: pl.BlockSpec((1, SEQ, LANES), lambda i, hp, off=off: (i, 0, off + hp))
    tab = pl.BlockSpec((SEQ, LANES), lambda i, hp: (0, 0))
    nbr = len(DA_BRANCH_DIL)
    return pl.pallas_call(
        _da_kernel, grid=(b, n_hp),
        in_specs=[blk(0), blk(n_hp), blk(2 * n_hp), tab, tab,
                  pl.BlockSpec(mask_t.shape, lambda i, hp: (0, 0, 0))],
        out_specs=pl.BlockSpec((1, SEQ, LANES), lambda i, hp: (i, 0, hp)),
        out_shape=jax.ShapeDtypeStruct((b, SEQ, DA_W), BF16),
        scratch_shapes=[pltpu.VMEM((SEQ, LANES), F32)] * 3 + [pltpu.VMEM((nbr, SEQ, LANES), F32)] * 2,
        compiler_params=_cparams("parallel", "parallel"), name="dilated_attention",
    )(proj, proj, proj, cos_t, sin_t, mask_t)


ML_SCALE = ML_HEAD_DIM ** -0.5
GATE_I_FWD, GATE_I_BWD, GATE_F_FWD, GATE_F_BWD = 0, 4, 8, 12


def _exact_ones_matmul(ones_bf16, x):
    hi = x.astype(BF16)
    r1 = x - hi.astype(F32)
    mid = r1.astype(BF16)
    lo = (r1 - mid.astype(F32)).astype(BF16)
    dot = lambda t: jnp.dot(ones_bf16, t, preferred_element_type=F32)
    return dot(hi) + dot(mid) + dot(lo)


def _mlstm_chunk(q, k, v, cum_col, cum_row, i_col, i_row, total, mask, state):
    c_mat, n_vec, m_run = state
    log_d = jnp.where(mask, cum_col - cum_row + i_row, NEG)
    log_inter = cum_col + m_run
    m_t = jnp.maximum(log_inter, jnp.max(log_d, axis=-1, keepdims=True))
    w_inter = jnp.exp(log_inter - m_t)
    s = lax.dot_general(q, k, (((1,), (1,)), ((), ())), preferred_element_type=F32)
    s = s * ML_SCALE * jnp.exp(log_d - m_t)
    num = w_inter * jnp.dot(q, c_mat.astype(BF16), preferred_element_type=F32)
    num = num + jnp.dot(s.astype(BF16), v, preferred_element_type=F32)
    den = w_inter * jnp.sum(q.astype(F32) * n_vec, axis=-1, keepdims=True)
    den = den + jnp.sum(s, axis=-1, keepdims=True)
    h = num / jnp.maximum(jnp.abs(den), jnp.exp(-m_t))
    log_w = total - cum_col + i_col
    m_new = jnp.maximum(total + m_run, jnp.max(log_w, axis=0, keepdims=True))
    kw = k.astype(F32) * (jnp.exp(log_w - m_new) * ML_SCALE)
    decay = jnp.exp(total + m_run - m_new)
    c_new = decay * c_mat + lax.dot_general(
        kw.astype(BF16), v, (((0,), (0,)), ((), ())), preferred_element_type=F32)
    n_new = decay * n_vec + jnp.sum(kw, axis=0, keepdims=True)
    return h, (c_new, n_new, m_new)


def _mlstm_kernel(q_ref, k_ref, v_ref, og_ref, g_ref, gb_ref, ng_ref, out_ref, col_s, row_s, hf_s, hb_s):
    L = ML_CHUNK
    ri = lax.broadcasted_iota(jnp.int32, (L, L), 0)
    ci = lax.broadcasted_iota(jnp.int32, (L, L), 1)
    lower = ci <= ri
    upper = ci >= ri
    lower_m = jnp.where(lower, 1.0, 0.0).astype(BF16)
    upper_m = jnp.where(upper, 1.0, 0.0).astype(BF16)
    lane = lax.broadcasted_iota(jnp.int32, (L, LANES), 1)

    def prep(c, carry):
        rows = pl.ds(pl.multiple_of(c * L, L), L)
        g = g_ref[0, rows, :] + gb_ref[...]
        lf = jnp.minimum(g, 0.0) - jnp.log1p(jnp.exp(-jnp.abs(g)))
        cum_f = _exact_ones_matmul(lower_m, lf)
        suf_b = _exact_ones_matmul(upper_m, lf)
        colv = jnp.where(lane < GATE_F_FWD, g, jnp.where(lane < GATE_F_BWD, cum_f, suf_b))
        col_s[rows, :] = colv
        row_s[c] = colv.T
        return carry

    lax.fori_loop(0, N_CHUNKS, prep, 0)

    zero_state = (jnp.zeros((ML_HEAD_DIM, ML_HEAD_DIM), F32), jnp.zeros((1, ML_HEAD_DIM), F32),
                  jnp.zeros((1, 1), F32))
    for h in range(ML_HEADS):
        hl = slice(ML_HEAD_DIM * h, ML_HEAD_DIM * (h + 1))

        def body(c, carry, h=h, hl=hl):
            st_f, st_b = carry
            rows = pl.ds(pl.multiple_of(c * L, L), L)
            colv = col_s[rows, :]
            rowv = row_s[c]
            jf, ji = GATE_F_FWD + h, GATE_I_FWD + h
            h_f, st_f = _mlstm_chunk(
                q_ref[0, rows, hl], k_ref[0, rows, hl], v_ref[0, rows, hl],
                colv[:, jf:jf + 1], rowv[jf:jf + 1, :], colv[:, ji:ji + 1], rowv[ji:ji + 1, :],
                rowv[jf:jf + 1, L - 1:L], lower, st_f)
            hf_s[rows, hl] = h_f
            cb = N_CHUNKS - 1 - c
            rows_b = pl.ds(pl.multiple_of(cb * L, L), L)
            colv = col_s[rows_b, :]
            rowv = row_s[cb]
            jf, ji = GATE_F_BWD + h, GATE_I_BWD + h
            h_b, st_b = _mlstm_chunk(
                q_ref[0, rows_b, hl], k_ref[0, rows_b, hl], v_ref[0, rows_b, hl],
                colv[:, jf:jf + 1], rowv[jf:jf + 1, :], colv[:, ji:ji + 1], rowv[ji:ji + 1, :],
                rowv[jf:jf + 1, 0:1], upper, st_b)
            hb_s[rows_b, hl] = h_b
            return st_f, st_b

        lax.fori_loop(0, N_CHUNKS, body, (zero_state, zero_state))

    tr = 256
    for r0 in range(0, SEQ, tr):
        for h in range(ML_HEADS):
            hl = slice(ML_HEAD_DIM * h, ML_HEAD_DIM * (h + 1))
            hh = hf_s[r0:r0 + tr, hl] + hb_s[r0:r0 + tr, hl]
            mu = jnp.mean(hh, axis=-1, keepdims=True)
            cen = hh - mu
            var = jnp.mean(cen * cen, axis=-1, keepdims=True)
            hn = cen * lax.rsqrt(var + LN_EPS) * ng_ref[:, hl]
            og = og_ref[0, r0:r0 + tr, hl].astype(F32)
            out_ref[0, r0:r0 + tr, hl] = (hn / (1.0 + jnp.exp(-og))).astype(out_ref.dtype)


def mlstm_mixer(proj, gates, gate_bias, norm_g):
    b = proj.shape[0]
    c0 = (3 * NA_W) // ML_W
    blk = lambda off: pl.BlockSpec((1, SEQ, ML_W), lambda i, off=off: (i, 0, off))
    gb = jnp.pad(gate_bias.astype(F32), (0, LANES - gate_bias.shape[0])).reshape(1, LANES)
    return pl.pallas_call(
        _mlstm_kernel, grid=(b,),
        in_specs=[blk(c0), blk(c0 + 1), blk(c0 + 2), blk(c0 + 3),
                  pl.BlockSpec((1, SEQ, LANES), lambda i: (i, 0, 0)),
                  pl.BlockSpec((1, LANES), lambda i: (0, 0)),
                  pl.BlockSpec((1, ML_W), lambda i: (0, 0))],
        out_specs=pl.BlockSpec((1, SEQ, ML_W), lambda i: (i, 0, 0)),
        out_shape=jax.ShapeDtypeStruct((b, SEQ, ML_W), BF16),
        scratch_shapes=[pltpu.VMEM((SEQ, LANES), F32), pltpu.VMEM((N_CHUNKS, LANES, ML_CHUNK), F32),
                        pltpu.VMEM((SEQ, ML_W), F32), pltpu.VMEM((SEQ, ML_W), F32)],
        compiler_params=_cparams("parallel"), name="mlstm_mixer",
    )(proj, proj, proj, proj, gates, gb, norm_g.reshape(1, ML_W).astype(F32))


def _router_kernel(x_ref, w_ref, aff_ref):
    logits = lax.dot_general(w_ref[...], x_ref[...], (((1,), (1,)), ((), ())),
                             precision=lax.Precision.HIGHEST, preferred_element_type=F32)
    z = jnp.exp(logits - jnp.max(logits, axis=0, keepdims=True))
    aff_ref[...] = z / jnp.sum(z, axis=0, keepdims=True)


def router_affinities(x, w_router_t, tm=1024):
    n, d = x.shape
    return pl.pallas_call(
        _router_kernel, grid=(n // tm,),
        in_specs=[pl.BlockSpec((tm, d), lambda i: (i, 0)), pl.BlockSpec((N_EXPERTS, d), lambda i: (0, 0))],
        out_specs=pl.BlockSpec((N_EXPERTS, tm), lambda i: (0, i)),
        out_shape=jax.ShapeDtypeStruct((N_EXPERTS, n), F32),
        compiler_params=_cparams("parallel"), name="router_affinities")(x, w_router_t)


def _tri_matrices(r):
    li = lax.broadcasted_iota(jnp.int32, (LANES, LANES), 0)
    lj = lax.broadcasted_iota(jnp.int32, (LANES, LANES), 1)
    tri_u = jnp.where(li <= lj, 1.0, 0.0).astype(BF16)
    ri = lax.broadcasted_iota(jnp.int32, (r, r), 0)
    rj = lax.broadcasted_iota(jnp.int32, (r, r), 1)
    tri_l = jnp.where(rj < ri, 1.0, 0.0).astype(BF16)
    return tri_u, tri_l


def _prefix_counts(mask, tri_u, tri_l):
    r = mask.shape[0]
    within = jnp.dot(mask.astype(BF16), tri_u, preferred_element_type=F32)
    rowtot = within[:, LANES - 1:LANES]
    hi = jnp.floor(rowtot * (1.0 / 16.0))
    lo = rowtot - 16.0 * hi
    hi_b = jnp.broadcast_to(hi, (r, LANES)).astype(BF16)
    lo_b = jnp.broadcast_to(lo, (r, LANES)).astype(BF16)
    rowoff = 16.0 * jnp.dot(tri_l, hi_b, preferred_element_type=F32) + jnp.dot(
        tri_l, lo_b, preferred_element_type=F32)
    return within - mask + rowoff, within, rowoff, rowtot


def _select_kernel(aff_ref, sel_ref, *, cap):
    r = aff_ref.shape[1]
    tri_u, tri_l = _tri_matrices(r)
    bits = pltpu.bitcast(aff_ref[0], jnp.int32)

    def count(m):
        c = jnp.sum(jnp.where(m, 1.0, 0.0), axis=1, keepdims=True)
        return jnp.sum(c, axis=0, keepdims=True)

    def bisect(i, prefix):
        cand = prefix | jnp.left_shift(jnp.int32(1), 30 - i)
        return jnp.where(count(bits >= cand) >= cap, cand, prefix)

    thr = lax.fori_loop(0, 31, bisect, jnp.zeros((1, 1), jnp.int32))
    gt = bits > thr
    eq = bits == thr
    need = cap - count(gt)
    rank_eq, _, _, _ = _prefix_counts(jnp.where(eq, 1.0, 0.0), tri_u, tri_l)
    sel_ref[0] = jnp.where(gt | (eq & (rank_eq < need)), 1.0, 0.0)


def select_tokens(aff3, cap):
    e, r, _ = aff3.shape
    blk = pl.BlockSpec((1, r, LANES), lambda i: (i, 0, 0))
    return pl.pallas_call(
        functools.partial(_select_kernel, cap=cap), grid=(e,), in_specs=[blk], out_specs=blk,
        out_shape=jax.ShapeDtypeStruct((e, r, LANES), F32),
        compiler_params=_cparams("parallel"), name="select_tokens")(aff3)


def _lists_kernel(sel_ref, aff_ref, idx_ref, dst_ref, gate_ref, ts_ref, ts_s, er_s, *, cap, st):
    e = pl.program_id(0)
    r = sel_ref.shape[1]
    tri_u, tri_l = _tri_matrices(r)

    @pl.when(e == 0)
    def _():
        cnt = sel_ref[0]
        for k in range(1, N_EXPERTS):
            cnt = cnt + sel_ref[k]
        ts, _, _, _ = _prefix_counts(cnt, tri_u, tri_l)
        ts_s[...] = ts
        ts_ref[...] = ts
        er_s[...] = jnp.zeros_like(er_s)

    sel = sel_ref[e]
    _, within, rowoff, rowtot = _prefix_counts(sel, tri_u, tri_l)
    dst = ts_s[...] + er_s[...]
    er_s[...] = er_s[...] + sel

    d2 = jnp.floor(dst * (1.0 / 65536.0))
    rem = dst - 65536.0 * d2
    d1 = jnp.floor(rem * (1.0 / 256.0))
    d0 = rem - 256.0 * d1
    aff = aff_ref[0]
    a_hi = aff.astype(BF16)
    a_r1 = aff - a_hi.astype(F32)
    a_mid = a_r1.astype(BF16)
    a_lo = (a_r1 - a_mid.astype(F32)).astype(BF16)
    rhs = jnp.concatenate([within.astype(BF16), d0.astype(BF16), d1.astype(BF16), d2.astype(BF16),
                           a_hi, a_mid, a_lo], axis=1)
    rowoff_row = rowoff.T[0:1, :]
    rowend_row = rowoff_row + jnp.broadcast_to(rowtot, (r, LANES)).T[0:1, :]
    rho_row = lax.broadcasted_iota(jnp.int32, (1, r), 1).astype(F32)
    lane = lax.broadcasted_iota(jnp.int32, (st, LANES), 1).astype(F32)
    eye = lax.broadcasted_iota(jnp.int32, (LANES, LANES), 0) == lax.broadcasted_iota(
        jnp.int32, (LANES, LANES), 1)

    for t in range(cap // st):
        s_col = (t * st + lax.broadcasted_iota(jnp.int32, (st, 1), 0)).astype(F32)
        in_row = (rowoff_row <= s_col) & (s_col < rowend_row)
        got = jnp.dot(jnp.where(in_row, 1.0, 0.0).astype(BF16), rhs, preferred_element_type=F32)
        base = jnp.sum(jnp.where(in_row, rowoff_row, 0.0), axis=1, keepdims=True)
        rho = jnp.sum(jnp.where(in_row, rho_row, 0.0), axis=1, keepdims=True)
        local = s_col - base
        lam = jnp.sum(jnp.where(got[:, 0:LANES] <= local, 1.0, 0.0), axis=1, keepdims=True)
        plane = lambda k: got[:, k * LANES:(k + 1) * LANES]
        pair = plane(1) + 256.0 * plane(2) + 65536.0 * plane(3)
        at_lam = lane == lam
        dval = jnp.sum(jnp.where(at_lam, pair, 0.0), axis=1, keepdims=True)
        gate_ref[0, t * st:(t + 1) * st, :] = jnp.sum(
            jnp.where(at_lam, plane(4) + plane(5) + plane(6), 0.0), axis=1, keepdims=True)
        ival = rho * float(LANES) + lam
        for j in range(st // LANES):
            seg = slice(j * LANES, (j + 1) * LANES)
            row = t * (st // LANES) + j
            idx_ref[0, row:row + 1, :] = jnp.sum(
                jnp.where(eye, ival[seg], 0.0), axis=0, keepdims=True).astype(jnp.int32)
            dst_ref[0, row:row + 1, :] = jnp.sum(
                jnp.where(eye, dval[seg], 0.0), axis=0, keepdims=True).astype(jnp.int32)


def build_lists(sel3, aff3, cap):
    e, r, _ = sel3.shape
    st = min(512, cap)
    lst = pl.BlockSpec((1, cap // LANES, LANES), lambda i: (i, 0, 0))
    return pl.pallas_call(
        functools.partial(_lists_kernel, cap=cap, st=st), grid=(e,),
        in_specs=[pl.BlockSpec((e, r, LANES), lambda i: (0, 0, 0)),
                  pl.BlockSpec((1, r, LANES), lambda i: (i, 0, 0))],
        out_specs=[lst, lst, pl.BlockSpec((1, cap, 1), lambda i: (i, 0, 0)),
                   pl.BlockSpec((r, LANES), lambda i: (0, 0))],
        out_shape=[jax.ShapeDtypeStruct((e, cap // LANES, LANES), jnp.int32),
                   jax.ShapeDtypeStruct((e, cap // LANES, LANES), jnp.int32),
                   jax.ShapeDtypeStruct((e, cap, 1), F32), jax.ShapeDtypeStruct((r, LANES), F32)],
        scratch_shapes=[pltpu.VMEM((r, LANES), F32), pltpu.VMEM((r, LANES), F32)],
        compiler_params=_cparams("arbitrary"), name="build_lists")(sel3, aff3)


FFN_CHUNKS = tuple((f, min(f + 256, D_FF)) for f in range(0, D_FF, 256))


def _ffn_kernel(idx_first, idx_next_a, idx_next_b, dst_prev_a, dst_prev_b, dst_last,
                x_hbm, gate_a, gate_b, wg_ref, wu_ref, wd_ref, z_hbm,
                xbuf0, xbuf1, ybuf0, ybuf1, acc, gsem, ssem, *, tm, n_grid):
    g = pl.program_id(0)

    def gather_row(ids, i, buf, sem):
        pltpu.make_async_copy(x_hbm.at[pl.ds(ids[0, 0, i], 1), :], buf.at[pl.ds(i, 1), :], sem).start()

    def scatter_row(dsts, i, buf, sem):
        pltpu.make_async_copy(buf.at[pl.ds(i, 1), :], z_hbm.at[pl.ds(dsts[0, 0, i], 1), :], sem).start()

    def wait_rows(buf, sem):
        pltpu.make_async_copy(x_hbm.at[pl.ds(0, tm), :], buf, sem).wait()

    @pl.when(g == 0)
    def _():
        ybuf1[...] = jnp.zeros_like(ybuf1)

        def first(i, carry):
            gather_row(idx_first, i, xbuf0, gsem.at[0])
            return carry
        lax.fori_loop(0, tm, first, 0)

    def tile(xcur, gcur, xnext, gnext, idx_next, yprev, sprev, dst_prev):
        wait_rows(xcur, gcur)
        xb = xcur[...].astype(BF16)
        rows_per_chunk = -(-tm // len(FFN_CHUNKS))
        for c, (f0, f1) in enumerate(FFN_CHUNKS):
            gate = jnp.dot(xb, wg_ref[0, :, f0:f1], preferred_element_type=F32)
            up = jnp.dot(xb, wu_ref[0, :, f0:f1], preferred_element_type=F32)
            h = (gate / (1.0 + jnp.exp(-gate)) * up).astype(BF16)
            part = jnp.dot(h, wd_ref[0, f0:f1, :], preferred_element_type=F32)
            if c == 0:
                acc[...] = part
            else:
                acc[...] += part
            for i in range(c * rows_per_chunk, min((c + 1) * rows_per_chunk, tm)):
                gather_row(idx_next, i, xnext, gnext)
                scatter_row(dst_prev, i, yprev, sprev)

    tile(xbuf0, gsem.at[0], xbuf1, gsem.at[1], idx_next_a, ybuf1, ssem.at[1], dst_prev_a)

    @pl.when(g >= 1)
    def _():
        wait_rows(ybuf0, ssem.at[0])
    ybuf0[...] = acc[...] * gate_a[0]

    tile(xbuf1, gsem.at[1], xbuf0, gsem.at[0], idx_next_b, ybuf0, ssem.at[0], dst_prev_b)
    wait_rows(ybuf1, ssem.at[1])
    ybuf1[...] = acc[...] * gate_b[0]

    @pl.when(g == n_grid - 1)
    def _():
        def last(i, carry):
            scatter_row(dst_last, i, ybuf1, ssem.at[1])
            return carry
        lax.fori_loop(0, tm, last, 0)
        wait_rows(ybuf1, ssem.at[1])
        wait_rows(ybuf0, ssem.at[0])
        wait_rows(xbuf0, gsem.at[0])


def expert_ffn(x, idx, dst, gate, w_gate, w_up, w_down, tm=512):
    n, d = x.shape
    e, cap = idx.shape
    tm = min(tm, cap // 2)
    nt = cap // tm
    assert nt % 2 == 0
    n_tiles = e * nt
    n_grid = n_tiles // 2
    idx3 = idx.reshape(n_tiles, 1, tm)
    gate3 = gate.reshape(n_tiles, tm, 1)
    spare =(e * cap + jnp.arange(tm, dtype=jnp.int32)).reshape(1, 1, tm)
    dst3 = jnp.concatenate([spare, dst.reshape(n_tiles, 1, tm)])
    smem = lambda imap: pl.BlockSpec((1, 1, tm), imap, memory_space=pltpu.SMEM)
    wspec = lambda w: pl.BlockSpec((1,) + w.shape[1:], lambda i: ((2 * i) // nt, 0, 0))
    return pl.pallas_call(
        functools.partial(_ffn_kernel, tm=tm, n_grid=n_grid), grid=(n_grid,),
        in_specs=[smem(lambda i: (0, 0, 0)), smem(lambda i: (2 * i + 1, 0, 0)),
                  smem(lambda i: (jnp.minimum(2 * i + 2, n_tiles - 1), 0, 0)),
                  smem(lambda i: (2 * i, 0, 0)), smem(lambda i: (2 * i + 1, 0, 0)),
                  smem(lambda i: (n_tiles, 0, 0)),
                  pl.BlockSpec(memory_space=pl.ANY),
                  pl.BlockSpec((1, tm, 1), lambda i: (2 * i, 0, 0)),
                  pl.BlockSpec((1, tm, 1), lambda i: (2 * i + 1, 0, 0)),
                  wspec(w_gate), wspec(w_up), wspec(w_down)],
        out_specs=pl.BlockSpec(memory_space=pl.ANY),
        out_shape=jax.ShapeDtypeStruct((e * cap + tm, d), F32),
        scratch_shapes=[pltpu.VMEM((tm, d), F32)] * 5
        + [pltpu.SemaphoreType.DMA((2,)), pltpu.SemaphoreType.DMA((2,))],
        compiler_params=_cparams("arbitrary"), name="expert_ffn",
    )(idx3, idx3, idx3, dst3, dst3, dst3, x, gate3, gate3, w_gate, w_up, w_down)


SUBLANES = 8
COMBINE_ZB = 256
COMBINE_ZC = COMBINE_ZB - SUBLANES


def _combine_kernel(ts_ref, x_ref, run_ref, g_ref, b_ref, z_hbm, o_ref, zbuf, sem, used, *, z_rows, n_tiles):
    i = pl.program_id(0)
    tt = x_ref.shape[0]

    def chunk_rows(t, c):
        lo = ts_ref[t] + c * COMBINE_ZC
        start = jnp.minimum((lo // SUBLANES) * SUBLANES, z_rows - COMBINE_ZB)
        return lo, pl.multiple_of(start, SUBLANES)

    def fetch(t, c, slot):
        _, start = chunk_rows(t, c)
        pltpu.make_async_copy(z_hbm.at[pl.ds(start, COMBINE_ZB), :], zbuf.at[slot], sem.at[slot]).start()

    @pl.when(i == 0)
    def _():
        used[0] = 0
        fetch(0, 0, 0)

    base = used[0]
    n_chunks = jnp.maximum((ts_ref[i + 1] - ts_ref[i] + COMBINE_ZC - 1) // COMBINE_ZC, 1)
    run_lo = jnp.broadcast_to(run_ref[:, 0:1], (tt, COMBINE_ZB))
    run_hi = jnp.broadcast_to(run_ref[:, 1:2], (tt, COMBINE_ZB))
    col = lax.broadcasted_iota(jnp.int32, (1, COMBINE_ZB), 1).astype(F32)

    def chunk(c, acc):
        slot = (base + c) % 2

        @pl.when(c + 1 < n_chunks)
        def _():
            fetch(i, c + 1, 1 - slot)

        @pl.when((c + 1 == n_chunks) & (i + 1 < n_tiles))
        def _():
            fetch(i + 1, 0, 1 - slot)

        lo, start = chunk_rows(i, c)
        lo_f = lo.astype(F32)
        pair = col + start.astype(F32)
        pair = jnp.where((pair >= lo_f) & (pair < lo_f + COMBINE_ZC), pair, -1.0)
        a = jnp.where((run_lo <= pair) & (pair < run_hi), 1.0, 0.0).astype(BF16)
        pltpu.make_async_copy(z_hbm.at[pl.ds(0, COMBINE_ZB), :], zbuf.at[slot], sem.at[slot]).wait()
        return acc + jnp.dot(a, zbuf[slot].astype(BF16), preferred_element_type=F32)

    ffn = lax.fori_loop(0, n_chunks, chunk, jnp.zeros((tt, D_MODEL), F32))
    used[0] = base + n_chunks
    o_ref[...] = _layer_norm_rows(DN_ALPHA * x_ref[...] + ffn, g_ref[...], b_ref[...])


def combine_ln(x, z, tile_start, runs, g, b, tt=256):
    n, d = x.shape
    n_tiles = n // tt
    row = lambda i, ts: (i, 0)
    fixed = lambda i, ts: (0, 0)
    grid_spec = pltpu.PrefetchScalarGridSpec(
        num_scalar_prefetch=1, grid=(n_tiles,),
        in_specs=[pl.BlockSpec((tt, d), row), pl.BlockSpec((tt, 2), row), pl.BlockSpec((1, d), fixed),
                  pl.BlockSpec((1, d), fixed), pl.BlockSpec(memory_space=pl.ANY)],
        out_specs=pl.BlockSpec((tt, d), row),
        scratch_shapes=[pltpu.VMEM((2, COMBINE_ZB, d), F32), pltpu.SemaphoreType.DMA((2,)),
                        pltpu.SMEM((1,), jnp.int32)])
    return pl.pallas_call(
        functools.partial(_combine_kernel, z_rows=z.shape[0], n_tiles=n_tiles), grid_spec=grid_spec,
        out_shape=jax.ShapeDtypeStruct((n, d), F32),
        compiler_params=_cparams("arbitrary"), name="combine_ln",
    )(tile_start, x, runs, g.reshape(1, d), b.reshape(1, d), z)


def moe_layer(x, w_router, w_gate, w_up, w_down, g, b, tt=256):
    n, _ = x.shape
    r = n // LANES
    cap = 2 * n // N_EXPERTS
    aff3 = router_affinities(x, w_router.T).reshape(N_EXPERTS, r, LANES)
    sel3 = select_tokens(aff3, cap)
    idx, dst, gate, ts = build_lists(sel3, aff3, cap)
    z = expert_ffn(x, idx.reshape(N_EXPERTS, cap), dst.reshape(N_EXPERTS, cap), gate, w_gate, w_up, w_down)
    ts_ext = jnp.concatenate([ts.reshape(n), jnp.full((1,), N_EXPERTS * cap, F32)])
    tile_start = ts_ext[::tt].astype(jnp.int32)
    runs = jnp.stack([ts_ext[:-1], ts_ext[1:]], axis=1)
    return combine_ln(x, z, tile_start, runs, g, b, tt=tt)


def kernel(x_prompt, x_sample, even_w_in, ml_gate_bias, na_rpb, ml_norm_g, even_w_out, da_w_in, da_w_out,
           ln_mix_g, ln_mix_b, ec_router, ec_w_gate, ec_w_up, ec_w_down, ln_ffn_g, ln_ffn_b):
    w_even = even_w_in[0][:, :3584].astype(BF16)
    w_gates = jnp.pad(even_w_in[0][:, 3584:], ((0, 0), (0, LANES - 16))).astype(BF16)
    w_out_a = even_w_out[0][:NA_W].astype(BF16)
    w_out_b = even_w_out[0][NA_W:].astype(BF16)
    w_odd = da_w_in[0].astype(BF16)
    w_odd_out = da_w_out[0].astype(BF16)
    tbl = na_bias_table(na_rpb[0])
    cos_t, sin_t = rope_tables()
    mask_t = da_mask_table()
    moe_w = [(ec_router[l], ec_w_gate[l].astype(BF16), ec_w_up[l].astype(BF16), ec_w_down[l].astype(BF16),
              ln_ffn_g[l], ln_ffn_b[l]) for l in range(DEPTH)]

    def trunk(x):
        b = x.shape[0]
        xt = x.reshape(b * SEQ, D_MODEL)
        proj, gates = in_projection(xt, w_even, w_gates)
        proj = proj.reshape(b, SEQ, -1)
        ya = neighbourhood_attention(proj, tbl)
        yb = mlstm_mixer(proj, gates.reshape(b, SEQ, LANES), ml_gate_bias[0], ml_norm_g[0])
        xt = out_projection_ln(xt, [ya.reshape(b * SEQ, NA_W), yb.reshape(b * SEQ, ML_W)],
                               [w_out_a, w_out_b], ln_mix_g[0], ln_mix_b[0])
        xt = moe_layer(xt, *moe_w[0])
        proj = in_projection(xt, w_odd).reshape(b, SEQ, -1)
        yc = dilated_attention(proj, cos_t, sin_t, mask_t)
        xt = out_projection_ln(xt, [yc.reshape(b * SEQ, DA_W)], [w_odd_out], ln_mix_g[1], ln_mix_b[1])
        xt = moe_layer(xt, *moe_w[1])
        return xt.reshape(b, SEQ, D_MODEL)

    return trunk(x_prompt), trunk(x_sample)
```

```python
import functools

import numpy as np
import jax
import jax.numpy as jnp
from jax import lax
from jax.experimental import pallas as pl
from jax.experimental.pallas import tpu as pltpu

F32 = jnp.float32
BF16 = jnp.bfloat16

D_MODEL = 1024
SEQ = 2048
GRID_W = 64
GRID_ROWS = SEQ // GRID_W
NA_HEADS = 8
NA_W = 512
NA_WIN_R = 8
NA_WIN_C = 16
ML_HEADS = 4
ML_HEAD_DIM = 128
ML_W = 512
ML_CHUNK = 128
N_CHUNKS = SEQ // ML_CHUNK
DA_HEADS = 16
DA_W = 1024
DA_HALF = 64
ROPE_THETA = 10000.0
N_EXPERTS = 16
D_FF = 1408
LN_EPS = 1e-5
DEPTH = 2
DN_ALPHA = (2 * DEPTH) ** 0.25
LANES = 128
NEG = -1e30
VMEM_LIMIT = 56 * 1024 * 1024


def _cparams(*sem):
    return pltpu.CompilerParams(dimension_semantics=sem, vmem_limit_bytes=VMEM_LIMIT)


def _inproj_kernel(x_ref, w_ref, o_ref, *, n_chunk):
    xb = x_ref[...].astype(BF16)
    for c in range(0, w_ref.shape[1], n_chunk):
        o_ref[:, c:c + n_chunk] = jnp.dot(
            xb, w_ref[:, c:c + n_chunk], preferred_element_type=F32).astype(o_ref.dtype)


def _inproj_gates_kernel(x_ref, w_ref, wg_ref, o_ref, g_ref, *, n_chunk):
    xb = x_ref[...].astype(BF16)
    for c in range(0, w_ref.shape[1], n_chunk):
        o_ref[:, c:c + n_chunk] = jnp.dot(
            xb, w_ref[:, c:c + n_chunk], preferred_element_type=F32).astype(o_ref.dtype)
    g_ref[...] = jnp.dot(xb, wg_ref[...], preferred_element_type=F32)


def in_projection(x, w, wg=None, tm=512, n_chunk=512):
    m, k = x.shape
    n = w.shape[1]
    x_spec = pl.BlockSpec((tm, k), lambda i: (i, 0))
    w_spec = pl.BlockSpec((k, n), lambda i: (0, 0))
    o_spec = pl.BlockSpec((tm, n), lambda i: (i, 0))
    if wg is None:
        return pl.pallas_call(
            functools.partial(_inproj_kernel, n_chunk=n_chunk),
            grid=(m // tm,), in_specs=[x_spec, w_spec], out_specs=o_spec,
            out_shape=jax.ShapeDtypeStruct((m, n), BF16),
            compiler_params=_cparams("parallel"), name="in_projection")(x, w)
    return pl.pallas_call(
        functools.partial(_inproj_gates_kernel, n_chunk=n_chunk),
        grid=(m // tm,),
        in_specs=[x_spec, w_spec, pl.BlockSpec((k, LANES), lambda i: (0, 0))],
        out_specs=[o_spec, pl.BlockSpec((tm, LANES), lambda i: (i, 0))],
        out_shape=[jax.ShapeDtypeStruct((m, n), BF16), jax.ShapeDtypeStruct((m, LANES), F32)],
        compiler_params=_cparams("parallel"), name="in_projection_gates")(x, w, wg)


def _layer_norm_rows(acc, g, b):
    mu = jnp.mean(acc, axis=-1, keepdims=True)
    cen = acc - mu
    var = jnp.mean(cen * cen, axis=-1, keepdims=True)
    return cen * lax.rsqrt(var + LN_EPS) * g + b


def _outproj_ln_kernel(*refs, n_mix):
    x_ref = refs[0]
    mix_refs = refs[1:1 + n_mix]
    w_refs = refs[1 + n_mix:1 + 2 * n_mix]
    g_ref, b_ref, o_ref, ot_ref = refs[1 + 2 * n_mix:]
    acc = DN_ALPHA * x_ref[...]
    for m_ref, w_ref in zip(mix_refs, w_refs):
        acc = acc + jnp.dot(m_ref[...], w_ref[...], preferred_element_type=F32)
    res = _layer_norm_rows(acc, g_ref[...], b_ref[...])
    o_ref[...] = res
    _store_token_tiles(ot_ref, res)


SUBLANES = 8
TILE_ROWS = D_MODEL // LANES


def _store_token_tiles(ref, rows):
    n = rows.shape[0]
    for j in range(TILE_ROWS):
        ref[pl.ds(j, n, stride=TILE_ROWS), :] = rows[:, j * LANES:(j + 1) * LANES]


def _load_token_tiles(ref, n, first_row=0):
    return jnp.concatenate(
        [ref[pl.ds(first_row + j, n, stride=TILE_ROWS), :] for j in range(TILE_ROWS)], axis=1)


def out_projection_ln(x, mixes, ws, g, b, tm=512):
    m, d = x.shape
    n_mix = len(mixes)
    row = lambda i: (i, 0)
    fixed = lambda i: (0, 0)
    in_specs = [pl.BlockSpec((tm, d), row)]
    in_specs += [pl.BlockSpec((tm, mx.shape[1]), row) for mx in mixes]
    in_specs += [pl.BlockSpec(w.shape, fixed) for w in ws]
    in_specs += [pl.BlockSpec((1, d), fixed), pl.BlockSpec((1, d), fixed)]
    return pl.pallas_call(
        functools.partial(_outproj_ln_kernel, n_mix=n_mix),
        grid=(m // tm,), in_specs=in_specs,
        out_specs=[pl.BlockSpec((tm, d), row), pl.BlockSpec((tm * TILE_ROWS, LANES), row)],
        out_shape=[jax.ShapeDtypeStruct((m, d), F32), jax.ShapeDtypeStruct((m * TILE_ROWS, LANES), F32)],
        compiler_params=_cparams("parallel"), name="out_projection_ln",
    )(x, *mixes, *ws, g.reshape(1, d), b.reshape(1, d))


NA_KEYS = NA_WIN_R * GRID_W
NA_CASES = NA_WIN_R


def na_bias_table(rpb):
    j = np.arange(GRID_W)
    kc = np.arange(GRID_W)
    win_c0 = np.clip(j - NA_WIN_C // 2, 0, GRID_W - NA_WIN_C)
    valid = (kc[None, :] >= win_c0[:, None]) & (kc[None, :] < win_c0[:, None] + NA_WIN_C)
    dc = np.clip(kc[None, :] - j[:, None] + NA_WIN_C - 1, 0, 2 * NA_WIN_C - 2)
    n_dc = 2 * NA_WIN_C - 1
    pick = jnp.asarray(dc[:, :, None] == np.arange(n_dc)[None, None, :], F32)
    rows = jnp.einsum('hrd,jkd->hrjk', rpb.astype(F32), pick, precision=lax.Precision.HIGHEST)
    rows = jnp.where(valid[None, None], rows, NEG)
    t = jnp.stack([rows[:, NA_WIN_R - 1 - ci:2 * NA_WIN_R - 1 - ci] for ci in range(NA_CASES)], axis=1)
    t = t.transpose(0, 1, 3, 2, 4).reshape(NA_HEADS // 2, 2, NA_CASES, GRID_W, NA_KEYS)
    return t.transpose(0, 2, 1, 3, 4).reshape(NA_HEADS // 2, NA_CASES, 2 * GRID_W, NA_KEYS)


def _na_kernel(q_ref, k_ref, v_ref, tbl_ref, o_ref):
    head0 = lax.broadcasted_iota(jnp.int32, (GRID_W, LANES), 1) < 64

    def row(r, carry):
        rs = jnp.clip(r - NA_WIN_R // 2, 0, GRID_ROWS - NA_WIN_R)
        q = q_ref[0, pl.ds(pl.multiple_of(r * GRID_W, GRID_W), GRID_W), :]
        zero = jnp.zeros_like(q)
        q2 = jnp.concatenate([jnp.where(head0, q, zero), jnp.where(head0, zero, q)], axis=0)
        kw = k_ref[0, pl.ds(pl.multiple_of(rs * GRID_W, GRID_W), NA_KEYS), :]
        vw = v_ref[0, pl.ds(pl.multiple_of(rs * GRID_W, GRID_W), NA_KEYS), :]
        s = lax.dot_general(q2, kw, (((1,), (1,)), ((), ())), preferred_element_type=F32)
        s = s * 0.125 + tbl_ref[0, r - rs]
        m = jnp.max(s, axis=-1, keepdims=True)
        p = jnp.exp(s - m)
        l = jnp.sum(p, axis=-1, keepdims=True)
        o = jnp.dot(p.astype(BF16), vw, preferred_element_type=F32) / l
        o_ref[0, pl.ds(pl.multiple_of(r * GRID_W, GRID_W), GRID_W), :] = jnp.where(
            head0, o[:GRID_W], o[GRID_W:]).astype(o_ref.dtype)
        return carry

    lax.fori_loop(0, GRID_ROWS, row, 0, unroll=4)


def neighbourhood_attention(proj, tbl):
    b = proj.shape[0]
    n_hp = NA_HEADS // 2
    blk = lambda off: pl.BlockSpec((1, SEQ, LANES), lambda hp, i, off=off: (i, 0, off + hp))
    return pl.pallas_call(
        _na_kernel, grid=(n_hp, b),
        in_specs=[blk(0), blk(n_hp), blk(2 * n_hp),
                  pl.BlockSpec((1, NA_CASES, 2 * GRID_W, NA_KEYS), lambda hp, i: (hp, 0, 0, 0))],
        out_specs=pl.BlockSpec((1, SEQ, LANES), lambda hp, i: (i, 0, hp)),
        out_shape=jax.ShapeDtypeStruct((b, SEQ, NA_W), BF16),
        compiler_params=_cparams("parallel", "parallel"), name="neighbourhood_attention",
    )(proj, proj, proj, tbl)


DA_BRANCH_DIL = (1, 4, 16)
DA_QB = 128


def rope_tables():
    lane = np.arange(LANES) % 64
    inv = ROPE_THETA ** (-(2.0 * (lane % 32)) / 64.0)
    ang = jnp.arange(SEQ, dtype=F32)[:, None] * jnp.asarray(inv, F32)[None, :]
    sign = jnp.asarray(np.where(lane < 32, -1.0, 1.0), F32)[None, :]
    return jnp.cos(ang), jnp.sin(ang) * sign


LOG2E = 1.4426950408889634
LN2 = 0.6931471805599453
DA_Q_SCALE = 64 ** -0.5 * LOG2E
DA_MASK_CASES = 3


def da_mask_table():
    i = np.arange(2 * DA_QB)[None, :, None] % DA_QB
    j = np.arange(2 * DA_QB)[None, None, :]
    c = np.arange(DA_MASK_CASES)[:, None, None]
    return jnp.asarray(np.where(np.abs(i + DA_HALF * c - j) <= DA_HALF, 0.0, NEG), F32)


def _da_kernel(q_ref, k_ref, v_ref, cos_ref, sin_ref, mask_ref, o_ref, qs, ks, vs, acc_s, lse_s):
    lane_t = lax.broadcasted_iota(jnp.int32, (SEQ, LANES), 1) % 64
    first_half = lane_t < 32

    def rope(x):
        swapped = jnp.where(first_half, pltpu.roll(x, 96, 1), pltpu.roll(x, 32, 1))
        return x * cos_ref[...] + swapped * sin_ref[...]

    qs[...] = rope(q_ref[0].astype(F32)) * DA_Q_SCALE
    ks[...] = rope(k_ref[0].astype(F32))
    vs[...] = v_ref[0].astype(F32)

    head0 = lax.broadcasted_iota(jnp.int32, (DA_QB, LANES), 1) < 64

    def block(g, dil, row0, krow0, nk, case):
        qb = qs[pl.ds(row0, DA_QB, stride=dil), :].astype(BF16)
        kb = ks[pl.ds(krow0, nk, stride=dil), :].astype(BF16)
        vb = vs[pl.ds(krow0, nk, stride=dil), :].astype(BF16)
        zero = jnp.zeros_like(qb)
        q2 = jnp.concatenate([jnp.where(head0, qb, zero), jnp.where(head0, zero, qb)], axis=0)
        s = lax.dot_general(q2, kb, (((1,), (1,)), ((), ())), preferred_element_type=F32)
        s = s + mask_ref[case, :, 0:nk]
        m = jnp.max(s, axis=-1, keepdims=True)
        p = jnp.exp2(s - m)
        l = jnp.sum(p, axis=-1, keepdims=True)
        o = jnp.dot(p.astype(BF16), vb, preferred_element_type=F32) / l
        lse = m + jnp.log(l) * (1.0 / LN2)
        rows = pl.ds(row0, DA_QB, stride=dil)
        acc_s[g, rows, :] = jnp.where(head0, o[:DA_QB], o[DA_QB:])
        lse_s[g, rows, :] = jnp.where(head0, lse[:DA_QB], lse[DA_QB:])

    for g, dil in enumerate(DA_BRANCH_DIL):
        n_sub = SEQ // dil
        if n_sub == DA_QB:
            def body(r, carry, g=g, dil=dil):
                block(g, dil, r, r, DA_QB, 0)
                return carry
            lax.fori_loop(0, dil, body, 0, unroll=4)
        else:
            nb = n_sub // DA_QB
            nk = 2 * DA_QB

            def body(j, carry, g=g, dil=dil, nb=nb, nk=nk, n_sub=n_sub):
                r = j // nb
                q0 = (j % nb) * DA_QB
                k0 = jnp.clip(q0 - DA_HALF, 0, n_sub - nk)
                block(g, dil, r + dil * q0, r + dil * k0, nk, (q0 - k0) // DA_HALF)
                return carry
            lax.fori_loop(0, dil * nb, body, 0, unroll=4)

    lse_all = jnp.maximum(jnp.maximum(lse_s[0], lse_s[1]), lse_s[2])
    num = jnp.zeros((SEQ, LANES), F32)
    den = jnp.zeros((SEQ, LANES), F32)
    for g in range(len(DA_BRANCH_DIL)):
        w = jnp.exp2(lse_s[g] - lse_all)
        num = num + w * acc_s[g]
        den = den + w
    o_ref[0] = (num / den).astype(o_ref.dtype)


def dilated_attention(proj, cos_t, sin_t, mask_t):
    b = proj.shape[0]
    n_hp = DA_HEADS // 2
    blk = lambda off: pl.BlockSpec((1, SEQ, LANES), lambda i, hp, off=off: (i, 0, off + hp))
    tab = pl.BlockSpec((SEQ, LANES), lambda i, hp: (0, 0))
    nbr = len(DA_BRANCH_DIL)
    return pl.pallas_call(
        _da_kernel, grid=(b, n_hp),
        in_specs=[blk(0), blk(n_hp), blk(2 * n_hp), tab, tab,
                  pl.BlockSpec(mask_t.shape, lambda i, hp: (0, 0, 0))],
        out_specs=pl.BlockSpec((1, SEQ, LANES), lambda i, hp: (i, 0, hp)),
        out_shape=jax.ShapeDtypeStruct((b, SEQ, DA_W), BF16),
        scratch_shapes=[pltpu.VMEM((SEQ, LANES), F32)] * 3 + [pltpu.VMEM((nbr, SEQ, LANES), F32)] * 2,
        compiler_params=_cparams("parallel", "parallel"), name="dilated_attention",
    )(proj, proj, proj, cos_t, sin_t, mask_t)


ML_SCALE = ML_HEAD_DIM ** -0.5
GATE_I_FWD, GATE_I_BWD, GATE_F_FWD, GATE_F_BWD = 0, 4, 8, 12


def _exact_ones_matmul(ones_bf16, x):
    hi = x.astype(BF16)
    r1 = x - hi.astype(F32)
    mid = r1.astype(BF16)
    lo = (r1 - mid.astype(F32)).astype(BF16)
    dot = lambda t: jnp.dot(ones_bf16, t, preferred_element_type=F32)
    return dot(hi) + dot(mid) + dot(lo)


def _mlstm_chunk(q, k, v, cum_col, cum_row, i_col, i_row, total, mask, state):
    c_mat, n_vec, m_run = state
    log_d = jnp.where(mask, cum_col - cum_row + i_row, NEG)
    log_inter = cum_col + m_run
    m_t = jnp.maximum(log_inter, jnp.max(log_d, axis=-1, keepdims=True))
    w_inter = jnp.exp(log_inter - m_t)
    s = lax.dot_general(q, k, (((1,), (1,)), ((), ())), preferred_element_type=F32)
    s = s * ML_SCALE * jnp.exp(log_d - m_t)
    num = w_inter * jnp.dot(q, c_mat.astype(BF16), preferred_element_type=F32)
    num = num + jnp.dot(s.astype(BF16), v, preferred_element_type=F32)
    den = w_inter * jnp.sum(q.astype(F32) * n_vec, axis=-1, keepdims=True)
    den = den + jnp.sum(s, axis=-1, keepdims=True)
    h = num / jnp.maximum(jnp.abs(den), jnp.exp(-m_t))
    log_w = total - cum_col + i_col
    m_new = jnp.maximum(total + m_run, jnp.max(log_w, axis=0, keepdims=True))
    kw = k.astype(F32) * (jnp.exp(log_w - m_new) * ML_SCALE)
    decay = jnp.exp(total + m_run - m_new)
    c_new = decay * c_mat + lax.dot_general(
        kw.astype(BF16), v, (((0,), (0,)), ((), ())), preferred_element_type=F32)
    n_new = decay * n_vec + jnp.sum(kw, axis=0, keepdims=True)
    return h, (c_new, n_new, m_new)


def _mlstm_kernel(q_ref, k_ref, v_ref, og_ref, g_ref, gb_ref, ng_ref, out_ref, col_s, row_s, hf_s, hb_s):
    L = ML_CHUNK
    ri = lax.broadcasted_iota(jnp.int32, (L, L), 0)
    ci = lax.broadcasted_iota(jnp.int32, (L, L), 1)
    lower = ci <= ri
    upper = ci >= ri
    lower_m = jnp.where(lower, 1.0, 0.0).astype(BF16)
    upper_m = jnp.where(upper, 1.0, 0.0).astype(BF16)
    lane = lax.broadcasted_iota(jnp.int32, (L, LANES), 1)

    def prep(c, carry):
        rows = pl.ds(pl.multiple_of(c * L, L), L)
        g = g_ref[0, rows, :] + gb_ref[...]
        lf = jnp.minimum(g, 0.0) - jnp.log1p(jnp.exp(-jnp.abs(g)))
        cum_f = _exact_ones_matmul(lower_m, lf)
        suf_b = _exact_ones_matmul(upper_m, lf)
        colv = jnp.where(lane < GATE_F_FWD, g, jnp.where(lane < GATE_F_BWD, cum_f, suf_b))
        col_s[rows, :] = colv
        row_s[c] = colv.T
        return carry

    lax.fori_loop(0, N_CHUNKS, prep, 0)

    zero_state = (jnp.zeros((ML_HEAD_DIM, ML_HEAD_DIM), F32), jnp.zeros((1, ML_HEAD_DIM), F32),
                  jnp.zeros((1, 1), F32))
    for h in range(ML_HEADS):
        hl = slice(ML_HEAD_DIM * h, ML_HEAD_DIM * (h + 1))

        def body(c, carry, h=h, hl=hl):
            st_f, st_b = carry
            rows = pl.ds(pl.multiple_of(c * L, L), L)
            colv = col_s[rows, :]
            rowv = row_s[c]
            jf, ji = GATE_F_FWD + h, GATE_I_FWD + h
            h_f, st_f = _mlstm_chunk(
                q_ref[0, rows, hl], k_ref[0, rows, hl], v_ref[0, rows, hl],
                colv[:, jf:jf + 1], rowv[jf:jf + 1, :], colv[:, ji:ji + 1], rowv[ji:ji + 1, :],
                rowv[jf:jf + 1, L - 1:L], lower, st_f)
            hf_s[rows, hl] = h_f
            cb = N_CHUNKS - 1 - c
            rows_b = pl.ds(pl.multiple_of(cb * L, L), L)
            colv = col_s[rows_b, :]
            rowv = row_s[cb]
            jf, ji = GATE_F_BWD + h, GATE_I_BWD + h
            h_b, st_b = _mlstm_chunk(
                q_ref[0, rows_b, hl], k_ref[0, rows_b, hl], v_ref[0, rows_b, hl],
                colv[:, jf:jf + 1], rowv[jf:jf + 1, :], colv[:, ji:ji + 1], rowv[ji:ji + 1, :],
                rowv[jf:jf + 1, 0:1], upper, st_b)
            hb_s[rows_b, hl] = h_b
            return st_f, st_b

        lax.fori_loop(0, N_CHUNKS, body, (zero_state, zero_state))

    tr = 256
    for r0 in range(0, SEQ, tr):
        for h in range(ML_HEADS):
            hl = slice(ML_HEAD_DIM * h, ML_HEAD_DIM * (h + 1))
            hh = hf_s[r0:r0 + tr, hl] + hb_s[r0:r0 + tr, hl]
            mu = jnp.mean(hh, axis=-1, keepdims=True)
            cen = hh - mu
            var = jnp.mean(cen * cen, axis=-1, keepdims=True)
            hn = cen * lax.rsqrt(var + LN_EPS) * ng_ref[:, hl]
            og = og_ref[0, r0:r0 + tr, hl].astype(F32)
            out_ref[0, r0:r0 + tr, hl] = (hn / (1.0 + jnp.exp(-og))).astype(out_ref.dtype)


def mlstm_mixer(proj, gates, gate_bias, norm_g):
    b = proj.shape[0]
    c0 = (3 * NA_W) // ML_W
    blk = lambda off: pl.BlockSpec((1, SEQ, ML_W), lambda i, off=off: (i, 0, off))
    gb = jnp.pad(gate_bias.astype(F32), (0, LANES - gate_bias.shape[0])).reshape(1, LANES)
    return pl.pallas_call(
        _mlstm_kernel, grid=(b,),
        in_specs=[blk(c0), blk(c0 + 1), blk(c0 + 2), blk(c0 + 3),
                  pl.BlockSpec((1, SEQ, LANES), lambda i: (i, 0, 0)),
                  pl.BlockSpec((1, LANES), lambda i: (0, 0)),
                  pl.BlockSpec((1, ML_W), lambda i: (0, 0))],
        out_specs=pl.BlockSpec((1, SEQ, ML_W), lambda i: (i, 0, 0)),
        out_shape=jax.ShapeDtypeStruct((b, SEQ, ML_W), BF16),
        scratch_shapes=[pltpu.VMEM((SEQ, LANES), F32), pltpu.VMEM((N_CHUNKS, LANES, ML_CHUNK), F32),
                        pltpu.VMEM((SEQ, ML_W), F32), pltpu.VMEM((SEQ, ML_W), F32)],
        compiler_params=_cparams("parallel"), name="mlstm_mixer",
    )(proj, proj, proj, proj, gates, gb, norm_g.reshape(1, ML_W).astype(F32))


def _router_kernel(x_ref, w_ref, aff_ref):
    logits = lax.dot_general(w_ref[...], x_ref[...], (((1,), (1,)), ((), ())),
                             precision=lax.Precision.HIGHEST, preferred_element_type=F32)
    z = jnp.exp(logits - jnp.max(logits, axis=0, keepdims=True))
    aff_ref[...] = z / jnp.sum(z, axis=0, keepdims=True)


def router_affinities(x, w_router_t, tm=1024):
    n, d = x.shape
    return pl.pallas_call(
        _router_kernel, grid=(n // tm,),
        in_specs=[pl.BlockSpec((tm, d), lambda i: (i, 0)), pl.BlockSpec((N_EXPERTS, d), lambda i: (0, 0))],
        out_specs=pl.BlockSpec((N_EXPERTS, tm), lambda i: (0, i)),
        out_shape=jax.ShapeDtypeStruct((N_EXPERTS, n), F32),
        compiler_params=_cparams("parallel"), name="router_affinities")(x, w_router_t)


def _tri_matrices(r):
    li = lax.broadcasted_iota(jnp.int32, (LANES, LANES), 0)
    lj = lax.broadcasted_iota(jnp.int32, (LANES, LANES), 1)
    tri_u = jnp.where(li <= lj, 1.0, 0.0).astype(BF16)
    ri = lax.broadcasted_iota(jnp.int32, (r, r), 0)
    rj = lax.broadcasted_iota(jnp.int32, (r, r), 1)
    tri_l = jnp.where(rj < ri, 1.0, 0.0).astype(BF16)
    return tri_u, tri_l


def _prefix_counts(mask, tri_u, tri_l):
    r = mask.shape[0]
    within = jnp.dot(mask.astype(BF16), tri_u, preferred_element_type=F32)
    rowtot = within[:, LANES - 1:LANES]
    hi = jnp.floor(rowtot * (1.0 / 16.0))
    lo = rowtot - 16.0 * hi
    hi_b = jnp.broadcast_to(hi, (r, LANES)).astype(BF16)
    lo_b = jnp.broadcast_to(lo, (r, LANES)).astype(BF16)
    rowoff = 16.0 * jnp.dot(tri_l, hi_b, preferred_element_type=F32) + jnp.dot(
        tri_l, lo_b, preferred_element_type=F32)
    return within - mask + rowoff, within, rowoff, rowtot


def _select_kernel(aff_ref, sel_ref, *, cap):
    r = aff_ref.shape[1]
    tri_u, tri_l = _tri_matrices(r)
    bits = pltpu.bitcast(aff_ref[0], jnp.int32)

    def count(m):
        c = jnp.sum(jnp.where(m, 1.0, 0.0), axis=1, keepdims=True)
        return jnp.sum(c, axis=0, keepdims=True)

    def bisect(i, prefix):
        cand = prefix | jnp.left_shift(jnp.int32(1), 30 - i)
        return jnp.where(count(bits >= cand) >= cap, cand, prefix)

    thr = lax.fori_loop(0, 31, bisect, jnp.zeros((1, 1), jnp.int32))
    gt = bits > thr
    eq = bits == thr
    need = cap - count(gt)
    rank_eq, _, _, _ = _prefix_counts(jnp.where(eq, 1.0, 0.0), tri_u, tri_l)
    sel_ref[0] = jnp.where(gt | (eq & (rank_eq < need)), 1.0, 0.0)


def select_tokens(aff3, cap):
    e, r, _ = aff3.shape
    blk = pl.BlockSpec((1, r, LANES), lambda i: (i, 0, 0))
    return pl.pallas_call(
        functools.partial(_select_kernel, cap=cap), grid=(e,), in_specs=[blk], out_specs=blk,
        out_shape=jax.ShapeDtypeStruct((e, r, LANES), F32),
        compiler_params=_cparams("parallel"), name="select_tokens")(aff3)


def _lists_kernel(sel_ref, aff_ref, idx_ref, dst_ref, gate_ref, ts_ref, ts_s, er_s, *, cap, st):
    e = pl.program_id(0)
    r = sel_ref.shape[1]
    tri_u, tri_l = _tri_matrices(r)

    @pl.when(e == 0)
    def _():
        cnt = sel_ref[0]
        for k in range(1, N_EXPERTS):
            cnt = cnt + sel_ref[k]
        ts, _, _, _ = _prefix_counts(cnt, tri_u, tri_l)
        ts_s[...] = ts
        ts_ref[...] = ts
        er_s[...] = jnp.zeros_like(er_s)

    sel = sel_ref[e]
    _, within, rowoff, rowtot = _prefix_counts(sel, tri_u, tri_l)
    dst = ts_s[...] + er_s[...]
    er_s[...] = er_s[...] + sel

    d2 = jnp.floor(dst * (1.0 / 65536.0))
    rem = dst - 65536.0 * d2
    d1 = jnp.floor(rem * (1.0 / 256.0))
    d0 = rem - 256.0 * d1
    aff = aff_ref[0]
    a_hi = aff.astype(BF16)
    a_r1 = aff - a_hi.astype(F32)
    a_mid = a_r1.astype(BF16)
    a_lo = (a_r1 - a_mid.astype(F32)).astype(BF16)
    rhs = jnp.concatenate([within.astype(BF16), d0.astype(BF16), d1.astype(BF16), d2.astype(BF16),
                           a_hi, a_mid, a_lo], axis=1)
    rowoff_row = rowoff.T[0:1, :]
    rowend_row = rowoff_row + jnp.broadcast_to(rowtot, (r, LANES)).T[0:1, :]
    rho_row = lax.broadcasted_iota(jnp.int32, (1, r), 1).astype(F32)
    lane = lax.broadcasted_iota(jnp.int32, (st, LANES), 1).astype(F32)
    eye = lax.broadcasted_iota(jnp.int32, (LANES, LANES), 0) == lax.broadcasted_iota(
        jnp.int32, (LANES, LANES), 1)

    for t in range(cap // st):
        s_col = (t * st + lax.broadcasted_iota(jnp.int32, (st, 1), 0)).astype(F32)
        in_row = (rowoff_row <= s_col) & (s_col < rowend_row)
        got = jnp.dot(jnp.where(in_row, 1.0, 0.0).astype(BF16), rhs, preferred_element_type=F32)
        base = jnp.sum(jnp.where(in_row, rowoff_row, 0.0), axis=1, keepdims=True)
        rho = jnp.sum(jnp.where(in_row, rho_row, 0.0), axis=1, keepdims=True)
        local = s_col - base
        lam = jnp.sum(jnp.where(got[:, 0:LANES] <= local, 1.0, 0.0), axis=1, keepdims=True)
        plane = lambda k: got[:, k * LANES:(k + 1) * LANES]
        pair = plane(1) + 256.0 * plane(2) + 65536.0 * plane(3)
        at_lam = lane == lam
        dval = jnp.sum(jnp.where(at_lam, pair, 0.0), axis=1, keepdims=True)
        gate_ref[0, t * st:(t + 1) * st, :] = jnp.sum(
            jnp.where(at_lam, plane(4) + plane(5) + plane(6), 0.0), axis=1, keepdims=True)
        ival = rho * float(LANES) + lam
        for j in range(st // LANES):
            seg = slice(j * LANES, (j + 1) * LANES)
            row = t * (st // LANES) + j
            idx_ref[0, row:row + 1, :] = jnp.sum(
                jnp.where(eye, ival[seg], 0.0), axis=0, keepdims=True).astype(jnp.int32)
            dst_ref[0, row:row + 1, :] = jnp.sum(
                jnp.where(eye, dval[seg], 0.0), axis=0, keepdims=True).astype(jnp.int32)


def build_lists(sel3, aff3, cap):
    e, r, _ = sel3.shape
    st = min(512, cap)
    lst = pl.BlockSpec((1, cap // LANES, LANES), lambda i: (i, 0, 0))
    return pl.pallas_call(
        functools.partial(_lists_kernel, cap=cap, st=st), grid=(e,),
        in_specs=[pl.BlockSpec((e, r, LANES), lambda i: (0, 0, 0)),
                  pl.BlockSpec((1, r, LANES), lambda i: (i, 0, 0))],
        out_specs=[lst, lst, pl.BlockSpec((1, cap, 1), lambda i: (i, 0, 0)),
                   pl.BlockSpec((r, LANES), lambda i: (0, 0))],
        out_shape=[jax.ShapeDtypeStruct((e, cap // LANES, LANES), jnp.int32),
                   jax.ShapeDtypeStruct((e, cap // LANES, LANES), jnp.int32),
                   jax.ShapeDtypeStruct((e, cap, 1), F32), jax.ShapeDtypeStruct((r, LANES), F32)],
        scratch_shapes=[pltpu.VMEM((r, LANES), F32), pltpu.VMEM((r, LANES), F32)],
        compiler_params=_cparams("arbitrary"), name="build_lists")(sel3, aff3)


FFN_CHUNKS = tuple((f, min(f + 256, D_FF)) for f in range(0, D_FF, 256))


def _ffn_kernel(idx_first, idx_next_a, idx_next_b, dst_prev_a, dst_prev_b, dst_last,
                x_hbm, gate_a, gate_b, wg_ref, wu_ref, wd_ref, z_hbm,
                xbuf0, xbuf1, ybuf0, ybuf1, acc, gsem, ssem, *, tm, n_grid):
    g = pl.program_id(0)

    def token_tile(t):
        if isinstance(t, int):
            return pl.ds(t * TILE_ROWS, TILE_ROWS)
        return pl.ds(pl.multiple_of(t * TILE_ROWS, TILE_ROWS), TILE_ROWS)

    def gather_row(ids, i, buf, sem):
        pltpu.make_async_copy(x_hbm.at[token_tile(ids[0, 0, i]), :], buf.at[token_tile(i), :], sem).start()

    def scatter_row(dsts, i, buf, sem):
        pltpu.make_async_copy(buf.at[token_tile(i), :], z_hbm.at[token_tile(dsts[0, 0, i]), :], sem).start()

    def wait_rows(buf, sem):
        pltpu.make_async_copy(x_hbm.at[pl.ds(0, tm * TILE_ROWS), :], buf, sem).wait()

    @pl.when(g == 0)
    def _():
        ybuf1[...] = jnp.zeros_like(ybuf1)

        def first(i, carry):
            gather_row(idx_first, i, xbuf0, gsem.at[0])
            return carry
        lax.fori_loop(0, tm, first, 0, unroll=8)

    def tile(xcur, gcur, xnext, gnext, idx_next, yprev, sprev, dst_prev):
        for i in range(tm):
            gather_row(idx_next, i, xnext, gnext)
            scatter_row(dst_prev, i, yprev, sprev)

        wait_rows(xcur, gcur)
        xb = _load_token_tiles(xcur, tm).astype(BF16)
        for c, (f0, f1) in enumerate(FFN_CHUNKS):
            gate = jnp.dot(xb, wg_ref[0, :, f0:f1], preferred_element_type=F32)
            up = jnp.dot(xb, wu_ref[0, :, f0:f1], preferred_element_type=F32)
            h = (gate / (1.0 + jnp.exp(-gate)) * up).astype(BF16)
            part = jnp.dot(h, wd_ref[0, f0:f1, :], preferred_element_type=F32)
            if c == 0:
                acc[...] = part
            else:
                acc[...] += part

    tile(xbuf0, gsem.at[0], xbuf1, gsem.at[1], idx_next_a, ybuf1, ssem.at[1], dst_prev_a)

    @pl.when(g >= 1)
    def _():
        wait_rows(ybuf0, ssem.at[0])
    _store_token_tiles(ybuf0, acc[...] * gate_a[0])

    tile(xbuf1, gsem.at[1], xbuf0, gsem.at[0], idx_next_b, ybuf0, ssem.at[0], dst_prev_b)
    wait_rows(ybuf1, ssem.at[1])
    _store_token_tiles(ybuf1, acc[...] * gate_b[0])

    @pl.when(g == n_grid - 1)
    def _():
        def last(i, carry):
            scatter_row(dst_last, i, ybuf1, ssem.at[1])
            return carry
        lax.fori_loop(0, tm, last, 0, unroll=8)
        wait_rows(ybuf1, ssem.at[1])
        wait_rows(ybuf0, ssem.at[0])
        wait_rows(xbuf0, gsem.at[0])


def expert_ffn(x_tiles, idx, dst, gate, w_gate, w_up, w_down, tm=512):
    d = D_MODEL
    e, cap = idx.shape
    tm = min(tm, cap // 2)
    nt = cap // tm
    assert nt % 2 == 0
    n_tiles = e * nt
    n_grid = n_tiles // 2
    idx3 = idx.reshape(n_tiles, 1, tm)
    gate3 = gate.reshape(n_tiles, tm, 1)
    spare =(e * cap + jnp.arange(tm, dtype=jnp.int32)).reshape(1, 1, tm)
    dst3 = jnp.concatenate([spare, dst.reshape(n_tiles, 1, tm)])
    smem = lambda imap: pl.BlockSpec((1, 1, tm), imap, memory_space=pltpu.SMEM)
    wspec = lambda w: pl.BlockSpec((1,) + w.shape[1:], lambda i: ((2 * i) // nt, 0, 0))
    return pl.pallas_call(
        functools.partial(_ffn_kernel, tm=tm, n_grid=n_grid), grid=(n_grid,),
        in_specs=[smem(lambda i: (0, 0, 0)), smem(lambda i: (2 * i + 1, 0, 0)),
                  smem(lambda i: (jnp.minimum(2 * i + 2, n_tiles - 1), 0, 0)),
                  smem(lambda i: (2 * i, 0, 0)), smem(lambda i: (2 * i + 1, 0, 0)),
                  smem(lambda i: (n_tiles, 0, 0)),
                  pl.BlockSpec(memory_space=pl.ANY),
                  pl.BlockSpec((1, tm, 1), lambda i: (2 * i, 0, 0)),
                  pl.BlockSpec((1, tm, 1), lambda i: (2 * i + 1, 0, 0)),
                  wspec(w_gate), wspec(w_up), wspec(w_down)],
        out_specs=pl.BlockSpec(memory_space=pl.ANY),
        out_shape=jax.ShapeDtypeStruct(((e * cap + tm) * TILE_ROWS, LANES), F32),
        scratch_shapes=[pltpu.VMEM((tm * TILE_ROWS, LANES), F32)] * 4 + [pltpu.VMEM((tm, d), F32)]
        + [pltpu.SemaphoreType.DMA((2,)), pltpu.SemaphoreType.DMA((2,))],
        compiler_params=_cparams("arbitrary"), name="expert_ffn",
    )(idx3, idx3, idx3, dst3, dst3, dst3, x_tiles, gate3, gate3, w_gate, w_up, w_down)


COMBINE_ZB = 256


def _combine_kernel(ts_ref, x_ref, run_ref, g_ref, b_ref, z_hbm, o_ref, zbuf, sem, used, *, z_rows, n_tiles):
    i = pl.program_id(0)
    tt = x_ref.shape[0]

    def chunk_rows(t, c):
        lo = ts_ref[t] + c * COMBINE_ZB
        return lo, jnp.minimum(lo, z_rows - COMBINE_ZB)

    def chunk_tiles(start):
        return pl.ds(pl.multiple_of(start * TILE_ROWS, TILE_ROWS), COMBINE_ZB * TILE_ROWS)

    def fetch(t, c, slot):
        _, start = chunk_rows(t, c)
        pltpu.make_async_copy(z_hbm.at[chunk_tiles(start), :], zbuf.at[slot], sem.at[slot]).start()

    @pl.when(i == 0)
    def _():
        used[0] = 0
        fetch(0, 0, 0)

    base = used[0]
    n_chunks = jnp.maximum((ts_ref[i + 1] - ts_ref[i] + COMBINE_ZB - 1) // COMBINE_ZB, 1)
    run_lo = jnp.broadcast_to(run_ref[:, 0:1], (tt, COMBINE_ZB))
    run_hi = jnp.broadcast_to(run_ref[:, 1:2], (tt, COMBINE_ZB))
    col = lax.broadcasted_iota(jnp.int32, (1, COMBINE_ZB), 1).astype(F32)

    def chunk(c, acc):
        slot = (base + c) % 2

        @pl.when(c + 1 < n_chunks)
        def _():
            fetch(i, c + 1, 1 - slot)

        @pl.when((c + 1 == n_chunks) & (i + 1 < n_tiles))
        def _():
            fetch(i + 1, 0, 1 - slot)

        lo, start = chunk_rows(i, c)
        pair = col + start.astype(F32)
        pair = jnp.where(pair >= lo.astype(F32), pair, -1.0)
        a = jnp.where((run_lo <= pair) & (pair < run_hi), 1.0, 0.0).astype(BF16)
        pltpu.make_async_copy(z_hbm.at[pl.ds(0, COMBINE_ZB * TILE_ROWS), :], zbuf.at[slot],
                              sem.at[slot]).wait()
        rows = _load_token_tiles(zbuf.at[slot], COMBINE_ZB).astype(BF16)
        return acc + jnp.dot(a, rows, preferred_element_type=F32)

    ffn = lax.fori_loop(0, n_chunks, chunk, jnp.zeros((tt, D_MODEL), F32))
    used[0] = base + n_chunks
    o_ref[...] = _layer_norm_rows(DN_ALPHA * x_ref[...] + ffn, g_ref[...], b_ref[...])


def combine_ln(x, z, tile_start, runs, g, b, tt=256):
    n, d = x.shape
    n_tiles = n // tt
    row = lambda i, ts: (i, 0)
    fixed = lambda i, ts: (0, 0)
    grid_spec = pltpu.PrefetchScalarGridSpec(
        num_scalar_prefetch=1, grid=(n_tiles,),
        in_specs=[pl.BlockSpec((tt, d), row), pl.BlockSpec((tt, 2), row), pl.BlockSpec((1, d), fixed),
                  pl.BlockSpec((1, d), fixed), pl.BlockSpec(memory_space=pl.ANY)],
        out_specs=pl.BlockSpec((tt, d), row),
        scratch_shapes=[pltpu.VMEM((2, COMBINE_ZB * TILE_ROWS, LANES), F32), pltpu.SemaphoreType.DMA((2,)),
                        pltpu.SMEM((1,), jnp.int32)])
    return pl.pallas_call(
        functools.partial(_combine_kernel, z_rows=z.shape[0] // TILE_ROWS, n_tiles=n_tiles), grid_spec=grid_spec,
        out_shape=jax.ShapeDtypeStruct((n, d), F32),
        compiler_params=_cparams("arbitrary"), name="combine_ln",
    )(tile_start, x, runs, g.reshape(1, d), b.reshape(1, d), z)


def moe_layer(x, x_tiles, w_router, w_gate, w_up, w_down, g, b, tt=256):
    n, _ = x.shape
    r = n // LANES
    cap = 2 * n // N_EXPERTS
    aff3 = router_affinities(x, w_router.T).reshape(N_EXPERTS, r, LANES)
    sel3 = select_tokens(aff3, cap)
    idx, dst, gate, ts = build_lists(sel3, aff3, cap)
    z = expert_ffn(x_tiles, idx.reshape(N_EXPERTS, cap), dst.reshape(N_EXPERTS, cap), gate, w_gate, w_up, w_down)
    ts_ext = jnp.concatenate([ts.reshape(n), jnp.full((1,), N_EXPERTS * cap, F32)])
    tile_start = ts_ext[::tt].astype(jnp.int32)
    runs = jnp.stack([ts_ext[:-1], ts_ext[1:]], axis=1)
    return combine_ln(x, z, tile_start, runs, g, b, tt=tt)


def kernel(x_prompt, x_sample, even_w_in, ml_gate_bias, na_rpb, ml_norm_g, even_w_out, da_w_in, da_w_out,
           ln_mix_g, ln_mix_b, ec_router, ec_w_gate, ec_w_up, ec_w_down, ln_ffn_g, ln_ffn_b):
    w_even = even_w_in[0][:, :3584].astype(BF16)
    w_gates = jnp.pad(even_w_in[0][:, 3584:], ((0, 0), (0, LANES - 16))).astype(BF16)
    w_out_a = even_w_out[0][:NA_W].astype(BF16)
    w_out_b = even_w_out[0][NA_W:].astype(BF16)
    w_odd = da_w_in[0].astype(BF16)
    w_odd_out = da_w_out[0].astype(BF16)
    tbl = na_bias_table(na_rpb[0])
    cos_t, sin_t = rope_tables()
    mask_t = da_mask_table()
    moe_w = [(ec_router[l], ec_w_gate[l].astype(BF16), ec_w_up[l].astype(BF16), ec_w_down[l].astype(BF16),
              ln_ffn_g[l], ln_ffn_b[l]) for l in range(DEPTH)]

    def trunk(x):
        b = x.shape[0]
        xt = x.reshape(b * SEQ, D_MODEL)
        proj, gates = in_projection(xt, w_even, w_gates)
        proj = proj.reshape(b, SEQ, -1)
        ya = neighbourhood_attention(proj, tbl)
        yb = mlstm_mixer(proj, gates.reshape(b, SEQ, LANES), ml_gate_bias[0], ml_norm_g[0])
        xt, xt_tiles = out_projection_ln(xt, [ya.reshape(b * SEQ, NA_W), yb.reshape(b * SEQ, ML_W)],
                                         [w_out_a, w_out_b], ln_mix_g[0], ln_mix_b[0])
        xt = moe_layer(xt, xt_tiles, *moe_w[0])
        proj = in_projection(xt, w_odd).reshape(b, SEQ, -1)
        yc = dilated_attention(proj, cos_t, sin_t, mask_t)
        xt, xt_tiles = out_projection_ln(xt, [yc.reshape(b * SEQ, DA_W)], [w_odd_out],
                                         ln_mix_g[1], ln_mix_b[1])
        xt = moe_layer(xt, xt_tiles, *moe_w[1])
        return xt.reshape(b, SEQ, D_MODEL)

    return trunk(x_prompt), trunk(x_sample)
```

```python
import functools

import numpy as np
import jax
import jax.numpy as jnp
from jax import lax
from jax.experimental import pallas as pl
from jax.experimental.pallas import tpu as pltpu

F32 = jnp.float32
BF16 = jnp.bfloat16

D_MODEL = 1024
SEQ = 2048
GRID_W = 64
GRID_ROWS = SEQ // GRID_W
NA_HEADS = 8
NA_W = 512
NA_WIN_R = 8
NA_WIN_C = 16
ML_HEADS = 4
ML_HEAD_DIM = 128
ML_W = 512
ML_CHUNK = 128
N_CHUNKS = SEQ // ML_CHUNK
DA_HEADS = 16
DA_W = 1024
DA_HALF = 64
ROPE_THETA = 10000.0
N_EXPERTS = 16
D_FF = 1408
LN_EPS = 1e-5
DEPTH = 2
DN_ALPHA = (2 * DEPTH) ** 0.25
LANES = 128
NEG = -1e30
LOG2E = 1.4426950408889634
LN2 = 0.6931471805599453
VMEM_LIMIT = 56 * 1024 * 1024


def _cparams(*sem):
    return pltpu.CompilerParams(dimension_semantics=sem, vmem_limit_bytes=VMEM_LIMIT)


def _inproj_kernel(x_ref, w_ref, o_ref, *, n_chunk):
    xb = x_ref[...].astype(BF16)
    for c in range(0, w_ref.shape[1], n_chunk):
        o_ref[:, c:c + n_chunk] = jnp.dot(
            xb, w_ref[:, c:c + n_chunk], preferred_element_type=F32).astype(o_ref.dtype)


def _inproj_gates_kernel(x_ref, w_ref, wg_ref, o_ref, g_ref, *, n_chunk):
    xb = x_ref[...].astype(BF16)
    for c in range(0, w_ref.shape[1], n_chunk):
        o_ref[:, c:c + n_chunk] = jnp.dot(
            xb, w_ref[:, c:c + n_chunk], preferred_element_type=F32).astype(o_ref.dtype)
    g_ref[...] = jnp.dot(xb, wg_ref[...], preferred_element_type=F32)


def in_projection(x, w, wg=None, tm=512, n_chunk=512):
    m, k = x.shape
    n = w.shape[1]
    x_spec = pl.BlockSpec((tm, k), lambda i: (i, 0))
    w_spec = pl.BlockSpec((k, n), lambda i: (0, 0))
    o_spec = pl.BlockSpec((tm, n), lambda i: (i, 0))
    if wg is None:
        return pl.pallas_call(
            functools.partial(_inproj_kernel, n_chunk=n_chunk),
            grid=(m // tm,), in_specs=[x_spec, w_spec], out_specs=o_spec,
            out_shape=jax.ShapeDtypeStruct((m, n), BF16),
            compiler_params=_cparams("parallel"), name="in_projection")(x, w)
    return pl.pallas_call(
        functools.partial(_inproj_gates_kernel, n_chunk=n_chunk),
        grid=(m // tm,),
        in_specs=[x_spec, w_spec, pl.BlockSpec((k, LANES), lambda i: (0, 0))],
        out_specs=[o_spec, pl.BlockSpec((tm, LANES), lambda i: (i, 0))],
        out_shape=[jax.ShapeDtypeStruct((m, n), BF16), jax.ShapeDtypeStruct((m, LANES), F32)],
        compiler_params=_cparams("parallel"), name="in_projection_gates")(x, w, wg)


def _layer_norm_rows(acc, g, b):
    mu = jnp.mean(acc, axis=-1, keepdims=True)
    cen = acc - mu
    var = jnp.mean(cen * cen, axis=-1, keepdims=True)
    return cen * lax.rsqrt(var + LN_EPS) * g + b


def _outproj_ln_kernel(*refs, n_mix):
    x_ref = refs[0]
    mix_refs = refs[1:1 + n_mix]
    w_refs = refs[1 + n_mix:1 + 2 * n_mix]
    g_ref, b_ref, o_ref, ot_ref = refs[1 + 2 * n_mix:]
    acc = DN_ALPHA * x_ref[...]
    for m_ref, w_ref in zip(mix_refs, w_refs):
        acc = acc + jnp.dot(m_ref[...], w_ref[...], preferred_element_type=F32)
    res = _layer_norm_rows(acc, g_ref[...], b_ref[...])
    o_ref[...] = res
    _store_token_tiles(ot_ref, res)


SUBLANES = 8
TILE_ROWS = D_MODEL // LANES


def _store_token_tiles(ref, rows):
    n = rows.shape[0]
    for j in range(TILE_ROWS):
        ref[pl.ds(j, n, stride=TILE_ROWS), :] = rows[:, j * LANES:(j + 1) * LANES]


def _load_token_tiles(ref, n, first_row=0):
    return jnp.concatenate(
        [ref[pl.ds(first_row + j, n, stride=TILE_ROWS), :] for j in range(TILE_ROWS)], axis=1)


def out_projection_ln(x, mixes, ws, g, b, tm=512):
    m, d = x.shape
    n_mix = len(mixes)
    row = lambda i: (i, 0)
    fixed = lambda i: (0, 0)
    in_specs = [pl.BlockSpec((tm, d), row)]
    in_specs += [pl.BlockSpec((tm, mx.shape[1]), row) for mx in mixes]
    in_specs += [pl.BlockSpec(w.shape, fixed) for w in ws]
    in_specs += [pl.BlockSpec((1, d), fixed), pl.BlockSpec((1, d), fixed)]
    return pl.pallas_call(
        functools.partial(_outproj_ln_kernel, n_mix=n_mix),
        grid=(m // tm,), in_specs=in_specs,
        out_specs=[pl.BlockSpec((tm, d), row), pl.BlockSpec((tm * TILE_ROWS, LANES), row)],
        out_shape=[jax.ShapeDtypeStruct((m, d), F32), jax.ShapeDtypeStruct((m * TILE_ROWS, LANES), F32)],
        compiler_params=_cparams("parallel"), name="out_projection_ln",
    )(x, *mixes, *ws, g.reshape(1, d), b.reshape(1, d))


NA_KEYS = NA_WIN_R * GRID_W
NA_CASES = NA_WIN_R


def na_bias_table(rpb):
    j = np.arange(GRID_W)
    kc = np.arange(GRID_W)
    win_c0 = np.clip(j - NA_WIN_C // 2, 0, GRID_W - NA_WIN_C)
    valid = (kc[None, :] >= win_c0[:, None]) & (kc[None, :] < win_c0[:, None] + NA_WIN_C)
    dc = np.clip(kc[None, :] - j[:, None] + NA_WIN_C - 1, 0, 2 * NA_WIN_C - 2)
    n_dc = 2 * NA_WIN_C - 1
    pick = jnp.asarray(dc[:, :, None] == np.arange(n_dc)[None, None, :], F32)
    rows = jnp.einsum('hrd,jkd->hrjk', rpb.astype(F32), pick, precision=lax.Precision.HIGHEST)
    rows = jnp.where(valid[None, None], rows * LOG2E, NEG)
    t = jnp.stack([rows[:, NA_WIN_R - 1 - ci:2 * NA_WIN_R - 1 - ci] for ci in range(NA_CASES)], axis=1)
    t = t.transpose(0, 1, 3, 2, 4).reshape(NA_HEADS // 2, 2, NA_CASES, GRID_W, NA_KEYS)
    return t.transpose(0, 2, 1, 3, 4).reshape(NA_HEADS // 2, NA_CASES, 2 * GRID_W, NA_KEYS)


NA_SCORE_SCALE = 64 ** -0.5 * LOG2E
NA_UNROLL = 8


def _na_kernel(q_ref, k_ref, v_ref, tbl_ref, o_ref, v2a, v2b):
    head0 = lax.broadcasted_iota(jnp.int32, (GRID_W, LANES), 1) < 64

    in0 = lax.broadcasted_iota(jnp.int32, (SEQ, LANES), 1) < 64
    vf = v_ref[0].astype(F32)
    v2a[...] = jnp.concatenate([jnp.where(in0, vf, 0.0), jnp.where(in0, 1.0, 0.0)], axis=1).astype(BF16)
    v2b[...] = jnp.concatenate([jnp.where(in0, 0.0, vf), jnp.where(in0, 0.0, 1.0)], axis=1).astype(BF16)

    def row(r, carry):
        rs = jnp.clip(r - NA_WIN_R // 2, 0, GRID_ROWS - NA_WIN_R)
        q = q_ref[0, pl.ds(pl.multiple_of(r * GRID_W, GRID_W), GRID_W), :]
        zero = jnp.zeros_like(q)
        q2 = jnp.concatenate([jnp.where(head0, q, zero), jnp.where(head0, zero, q)], axis=0)
        win = pl.ds(pl.multiple_of(rs * GRID_W, GRID_W), NA_KEYS)
        s = lax.dot_general(q2, k_ref[0, win, :], (((1,), (1,)), ((), ())), preferred_element_type=F32)
        s = s * NA_SCORE_SCALE + tbl_ref[0, r - rs]
        p = jnp.exp2(s - jnp.max(s, axis=-1, keepdims=True)).astype(BF16)
        p2 = jnp.concatenate([p[:GRID_W], p[GRID_W:]], axis=1)
        ol = jnp.dot(p2, jnp.concatenate([v2a[win, :], v2b[win, :]], axis=0), preferred_element_type=F32)
        o_ref[0, pl.ds(pl.multiple_of(r * GRID_W, GRID_W), GRID_W), :] = (
            ol[:, :LANES] / ol[:, LANES:]).astype(o_ref.dtype)
        return carry

    lax.fori_loop(0, GRID_ROWS, row, 0, unroll=NA_UNROLL)


def neighbourhood_attention(proj, tbl):
    b = proj.shape[0]
    n_hp = NA_HEADS // 2
    blk = lambda off: pl.BlockSpec((1, SEQ, LANES), lambda hp, i, off=off: (i, 0, off + hp))
    return pl.pallas_call(
        _na_kernel, grid=(n_hp, b),
        in_specs=[blk(0), blk(n_hp), blk(2 * n_hp),
                  pl.BlockSpec((1, NA_CASES, 2 * GRID_W, NA_KEYS), lambda hp, i: (hp, 0, 0, 0))],
        out_specs=pl.BlockSpec((1, SEQ, LANES), lambda hp, i: (i, 0, hp)),
        out_shape=jax.ShapeDtypeStruct((b, SEQ, NA_W), BF16),
        scratch_shapes=[pltpu.VMEM((SEQ, 2 * LANES), BF16)] * 2,
        compiler_params=_cparams("parallel", "parallel"), name="neighbourhood_attention",
    )(proj, proj, proj, tbl)


DA_BRANCH_DIL = (1, 4, 16)
DA_QB = 128


def rope_tables():
    lane = np.arange(LANES) % 64
    inv = ROPE_THETA ** (-(2.0 * (lane % 32)) / 64.0)
    ang = jnp.arange(SEQ, dtype=F32)[:, None] * jnp.asarray(inv, F32)[None, :]
    sign = jnp.asarray(np.where(lane < 32, -1.0, 1.0), F32)[None, :]
    return jnp.cos(ang), jnp.sin(ang) * sign


DA_Q_SCALE = 64 ** -0.5 * LOG2E
DA_MASK_CASES = 3
DA_UNROLL = 8


def da_mask_table():
    i = np.arange(2 * DA_QB)[None, :, None] % DA_QB
    j = np.arange(2 * DA_QB)[None, None, :]
    c = np.arange(DA_MASK_CASES)[:, None, None]
    return jnp.asarray(np.where(np.abs(i + DA_HALF * c - j) <= DA_HALF, 0.0, NEG), F32)


def _da_kernel(q_ref, k_ref, v_ref, cos_ref, sin_ref, mask_ref, o_ref, qs, ks, vs, acc_s, lse_s):
    lane_t = lax.broadcasted_iota(jnp.int32, (SEQ, LANES), 1) % 64
    first_half = lane_t < 32

    def rope(x):
        swapped = jnp.where(first_half, pltpu.roll(x, 96, 1), pltpu.roll(x, 32, 1))
        return x * cos_ref[...] + swapped * sin_ref[...]

    qs[...] = rope(q_ref[0].astype(F32)) * DA_Q_SCALE
    ks[...] = rope(k_ref[0].astype(F32))
    vs[...] = v_ref[0].astype(F32)

    head0 = lax.broadcasted_iota(jnp.int32, (DA_QB, LANES), 1) < 64

    def block(g, dil, row0, krow0, nk, case):
        qb = qs[pl.ds(row0, DA_QB, stride=dil), :].astype(BF16)
        kb = ks[pl.ds(krow0, nk, stride=dil), :].astype(BF16)
        vb = vs[pl.ds(krow0, nk, stride=dil), :]
        zero = jnp.zeros_like(qb)
        q2 = jnp.concatenate([jnp.where(head0, qb, zero), jnp.where(head0, zero, qb)], axis=0)
        s = lax.dot_general(q2, kb, (((1,), (1,)), ((), ())), preferred_element_type=F32)
        s = s + mask_ref[case, :, 0:nk]
        m = jnp.max(s, axis=-1, keepdims=True)
        p = jnp.exp2(s - m).astype(BF16)
        in0 = lax.broadcasted_iota(jnp.int32, (nk, LANES), 1) < 64
        v2 = jnp.concatenate([
            jnp.concatenate([jnp.where(in0, vb, 0.0), jnp.where(in0, 1.0, 0.0)], axis=1),
            jnp.concatenate([jnp.where(in0, 0.0, vb), jnp.where(in0, 0.0, 1.0)], axis=1)],
            axis=0).astype(BF16)
        p2 = jnp.concatenate([p[:DA_QB], p[DA_QB:]], axis=1)
        ol = jnp.dot(p2, v2, preferred_element_type=F32)
        l = ol[:, LANES:]
        rows = pl.ds(row0, DA_QB, stride=dil)
        acc_s[g, rows, :] = ol[:, :LANES] / l
        lse_s[g, rows, :] = jnp.where(head0, m[:DA_QB], m[DA_QB:]) + jnp.log(l) * (1.0 / LN2)

    for g, dil in enumerate(DA_BRANCH_DIL):
        n_sub = SEQ // dil
        if n_sub == DA_QB:
            def body(r, carry, g=g, dil=dil):
                block(g, dil, r, r, DA_QB, 0)
                return carry
            lax.fori_loop(0, dil, body, 0, unroll=DA_UNROLL)
        else:
            nb = n_sub // DA_QB
            nk = 2 * DA_QB

            def body(j, carry, g=g, dil=dil, nb=nb, nk=nk, n_sub=n_sub):
                r = j // nb
                q0 = (j % nb) * DA_QB
                k0 = jnp.clip(q0 - DA_HALF, 0, n_sub - nk)
                block(g, dil, r + dil * q0, r + dil * k0, nk, (q0 - k0) // DA_HALF)
                return carry
            lax.fori_loop(0, dil * nb, body, 0, unroll=DA_UNROLL)

    lse_all = jnp.maximum(jnp.maximum(lse_s[0], lse_s[1]), lse_s[2])
    num = jnp.zeros((SEQ, LANES), F32)
    den = jnp.zeros((SEQ, LANES), F32)
    for g in range(len(DA_BRANCH_DIL)):
        w = jnp.exp2(lse_s[g] - lse_all)
        num = num + w * acc_s[g]
        den = den + w
    o_ref[0] = (num / den).astype(o_ref.dtype)


def dilated_attention(proj, cos_t, sin_t, mask_t):
    b = proj.shape[0]
    n_hp = DA_HEADS // 2
    blk = lambda off: pl.BlockSpec((1, SEQ, LANES), lambda i, hp, off=off: (i, 0, off + hp))
    tab = pl.BlockSpec((SEQ, LANES), lambda i, hp: (0, 0))
    nbr = len(DA_BRANCH_DIL)
    return pl.pallas_call(
        _da_kernel, grid=(b, n_hp),
        in_specs=[blk(0), blk(n_hp), blk(2 * n_hp), tab, tab,
                  pl.BlockSpec(mask_t.shape, lambda i, hp: (0, 0, 0))],
        out_specs=pl.BlockSpec((1, SEQ, LANES), lambda i, hp: (i, 0, hp)),
        out_shape=jax.ShapeDtypeStruct((b, SEQ, DA_W), BF16),
        scratch_shapes=[pltpu.VMEM((SEQ, LANES), F32)] * 3 + [pltpu.VMEM((nbr, SEQ, LANES), F32)] * 2,
        compiler_params=_cparams("parallel", "parallel"), name="dilated_attention",
    )(proj, proj, proj, cos_t, sin_t, mask_t)


ML_SCALE = ML_HEAD_DIM ** -0.5
GATE_I_FWD, GATE_I_BWD, GATE_F_FWD, GATE_F_BWD = 0, 4, 8, 12


def _exact_ones_matmul(ones_bf16, x):
    hi = x.astype(BF16)
    r1 = x - hi.astype(F32)
    mid = r1.astype(BF16)
    lo = (r1 - mid.astype(F32)).astype(BF16)
    dot = lambda t: jnp.dot(ones_bf16, t, preferred_element_type=F32)
    return dot(hi) + dot(mid) + dot(lo)


def _mlstm_chunk(q, k, v, cum_col, cum_row, i_col, i_row, total, mask, state):
    c_mat, n_vec, m_run = state
    log_d = jnp.where(mask, cum_col - cum_row + i_row, NEG)
    log_inter = cum_col + m_run
    m_t = jnp.maximum(log_inter, jnp.max(log_d, axis=-1, keepdims=True))
    w_inter = jnp.exp(log_inter - m_t)
    s = lax.dot_general(q, k, (((1,), (1,)), ((), ())), preferred_element_type=F32)
    s = s * ML_SCALE * jnp.exp(log_d - m_t)
    num = w_inter * jnp.dot(q, c_mat.astype(BF16), preferred_element_type=F32)
    num = num + jnp.dot(s.astype(BF16), v, preferred_element_type=F32)
    den = w_inter * jnp.sum(q.astype(F32) * n_vec, axis=-1, keepdims=True)
    den = den + jnp.sum(s, axis=-1, keepdims=True)
    h = num / jnp.maximum(jnp.abs(den), jnp.exp(-m_t))
    log_w = total - cum_col + i_col
    m_new = jnp.maximum(total + m_run, jnp.max(log_w, axis=0, keepdims=True))
    kw = k.astype(F32) * (jnp.exp(log_w - m_new) * ML_SCALE)
    decay = jnp.exp(total + m_run - m_new)
    c_new = decay * c_mat + lax.dot_general(
        kw.astype(BF16), v, (((0,), (0,)), ((), ())), preferred_element_type=F32)
    n_new = decay * n_vec + jnp.sum(kw, axis=0, keepdims=True)
    return h, (c_new, n_new, m_new)


def _mlstm_kernel(q_ref, k_ref, v_ref, og_ref, g_ref, gb_ref, ng_ref, out_ref, col_s, row_s, hf_s, hb_s):
    L = ML_CHUNK
    ri = lax.broadcasted_iota(jnp.int32, (L, L), 0)
    ci = lax.broadcasted_iota(jnp.int32, (L, L), 1)
    lower = ci <= ri
    upper = ci >= ri
    lower_m = jnp.where(lower, 1.0, 0.0).astype(BF16)
    upper_m = jnp.where(upper, 1.0, 0.0).astype(BF16)
    lane = lax.broadcasted_iota(jnp.int32, (L, LANES), 1)

    def prep(c, carry):
        rows = pl.ds(pl.multiple_of(c * L, L), L)
        g = g_ref[0, rows, :] + gb_ref[...]
        lf = jnp.minimum(g, 0.0) - jnp.log1p(jnp.exp(-jnp.abs(g)))
        cum_f = _exact_ones_matmul(lower_m, lf)
        suf_b = _exact_ones_matmul(upper_m, lf)
        colv = jnp.where(lane < GATE_F_FWD, g, jnp.where(lane < GATE_F_BWD, cum_f, suf_b))
        col_s[rows, :] = colv
        row_s[c] = colv.T
        return carry

    lax.fori_loop(0, N_CHUNKS, prep, 0)

    zero_state = (jnp.zeros((ML_HEAD_DIM, ML_HEAD_DIM), F32), jnp.zeros((1, ML_HEAD_DIM), F32),
                  jnp.zeros((1, 1), F32))
    for h in range(ML_HEADS):
        hl = slice(ML_HEAD_DIM * h, ML_HEAD_DIM * (h + 1))

        def body(c, carry, h=h, hl=hl):
            st_f, st_b = carry
            rows = pl.ds(pl.multiple_of(c * L, L), L)
            colv = col_s[rows, :]
            rowv = row_s[c]
            jf, ji = GATE_F_FWD + h, GATE_I_FWD + h
            h_f, st_f = _mlstm_chunk(
                q_ref[0, rows, hl], k_ref[0, rows, hl], v_ref[0, rows, hl],
                colv[:, jf:jf + 1], rowv[jf:jf + 1, :], colv[:, ji:ji + 1], rowv[ji:ji + 1, :],
                rowv[jf:jf + 1, L - 1:L], lower, st_f)
            hf_s[rows, hl] = h_f
            cb = N_CHUNKS - 1 - c
            rows_b = pl.ds(pl.multiple_of(cb * L, L), L)
            colv = col_s[rows_b, :]
            rowv = row_s[cb]
            jf, ji = GATE_F_BWD + h, GATE_I_BWD + h
            h_b, st_b = _mlstm_chunk(
                q_ref[0, rows_b, hl], k_ref[0, rows_b, hl], v_ref[0, rows_b, hl],
                colv[:, jf:jf + 1], rowv[jf:jf + 1, :], colv[:, ji:ji + 1], rowv[ji:ji + 1, :],
                rowv[jf:jf + 1, 0:1], upper, st_b)
            hb_s[rows_b, hl] = h_b
            return st_f, st_b

        lax.fori_loop(0, N_CHUNKS, body, (zero_state, zero_state))

    tr = 256
    for r0 in range(0, SEQ, tr):
        for h in range(ML_HEADS):
            hl = slice(ML_HEAD_DIM * h, ML_HEAD_DIM * (h + 1))
            hh = hf_s[r0:r0 + tr, hl] + hb_s[r0:r0 + tr, hl]
            mu = jnp.mean(hh, axis=-1, keepdims=True)
            cen = hh - mu
            var = jnp.mean(cen * cen, axis=-1, keepdims=True)
            hn = cen * lax.rsqrt(var + LN_EPS) * ng_ref[:, hl]
            og = og_ref[0, r0:r0 + tr, hl].astype(F32)
            out_ref[0, r0:r0 + tr, hl] = (hn / (1.0 + jnp.exp(-og))).astype(out_ref.dtype)


def mlstm_mixer(proj, gates, gate_bias, norm_g):
    b = proj.shape[0]
    c0 = (3 * NA_W) // ML_W
    blk = lambda off: pl.BlockSpec((1, SEQ, ML_W), lambda i, off=off: (i, 0, off))
    gb = jnp.pad(gate_bias.astype(F32), (0, LANES - gate_bias.shape[0])).reshape(1, LANES)
    return pl.pallas_call(
        _mlstm_kernel, grid=(b,),
        in_specs=[blk(c0), blk(c0 + 1), blk(c0 + 2), blk(c0 + 3),
                  pl.BlockSpec((1, SEQ, LANES), lambda i: (i, 0, 0)),
                  pl.BlockSpec((1, LANES), lambda i: (0, 0)),
                  pl.BlockSpec((1, ML_W), lambda i: (0, 0))],
        out_specs=pl.BlockSpec((1, SEQ, ML_W), lambda i: (i, 0, 0)),
        out_shape=jax.ShapeDtypeStruct((b, SEQ, ML_W), BF16),
        scratch_shapes=[pltpu.VMEM((SEQ, LANES), F32), pltpu.VMEM((N_CHUNKS, LANES, ML_CHUNK), F32),
                        pltpu.VMEM((SEQ, ML_W), F32), pltpu.VMEM((SEQ, ML_W), F32)],
        compiler_params=_cparams("parallel"), name="mlstm_mixer",
    )(proj, proj, proj, proj, gates, gb, norm_g.reshape(1, ML_W).astype(F32))


def _router_kernel(x_ref, w_ref, aff_ref):
    logits = lax.dot_general(w_ref[...], x_ref[...], (((1,), (1,)), ((), ())),
                             precision=lax.Precision.HIGHEST, preferred_element_type=F32)
    z = jnp.exp(logits - jnp.max(logits, axis=0, keepdims=True))
    aff_ref[...] = z / jnp.sum(z, axis=0, keepdims=True)


def router_affinities(x, w_router_t, tm=1024):
    n, d = x.shape
    return pl.pallas_call(
        _router_kernel, grid=(n // tm,),
        in_specs=[pl.BlockSpec((tm, d), lambda i: (i, 0)), pl.BlockSpec((N_EXPERTS, d), lambda i: (0, 0))],
        out_specs=pl.BlockSpec((N_EXPERTS, tm), lambda i: (0, i)),
        out_shape=jax.ShapeDtypeStruct((N_EXPERTS, n), F32),
        compiler_params=_cparams("parallel"), name="router_affinities")(x, w_router_t)


def _tri_matrices(r):
    li = lax.broadcasted_iota(jnp.int32, (LANES, LANES), 0)
    lj = lax.broadcasted_iota(jnp.int32, (LANES, LANES), 1)
    tri_u = jnp.where(li <= lj, 1.0, 0.0).astype(BF16)
    ri = lax.broadcasted_iota(jnp.int32, (r, r), 0)
    rj = lax.broadcasted_iota(jnp.int32, (r, r), 1)
    tri_l = jnp.where(rj < ri, 1.0, 0.0).astype(BF16)
    return tri_u, tri_l


def _prefix_counts(mask, tri_u, tri_l):
    r = mask.shape[0]
    within = jnp.dot(mask.astype(BF16), tri_u, preferred_element_type=F32)
    rowtot = within[:, LANES - 1:LANES]
    hi = jnp.floor(rowtot * (1.0 / 16.0))
    lo = rowtot - 16.0 * hi
    hi_b = jnp.broadcast_to(hi, (r, LANES)).astype(BF16)
    lo_b = jnp.broadcast_to(lo, (r, LANES)).astype(BF16)
    rowoff = 16.0 * jnp.dot(tri_l, hi_b, preferred_element_type=F32) + jnp.dot(
        tri_l, lo_b, preferred_element_type=F32)
    return within - mask + rowoff, within, rowoff, rowtot


def _select_kernel(aff_ref, sel_ref, *, cap):
    r = aff_ref.shape[1]
    tri_u, tri_l = _tri_matrices(r)
    bits = pltpu.bitcast(aff_ref[0], jnp.int32)

    def count(m):
        c = jnp.sum(jnp.where(m, 1.0, 0.0), axis=1, keepdims=True)
        return jnp.sum(c, axis=0, keepdims=True)

    def bisect(i, prefix):
        cand = prefix | jnp.left_shift(jnp.int32(1), 30 - i)
        return jnp.where(count(bits >= cand) >= cap, cand, prefix)

    thr = lax.fori_loop(0, 31, bisect, jnp.zeros((1, 1), jnp.int32))
    gt = bits > thr
    eq = bits == thr
    need = cap - count(gt)
    rank_eq, _, _, _ = _prefix_counts(jnp.where(eq, 1.0, 0.0), tri_u, tri_l)
    sel_ref[0] = jnp.where(gt | (eq & (rank_eq < need)), 1.0, 0.0)


def select_tokens(aff3, cap):
    e, r, _ = aff3.shape
    blk = pl.BlockSpec((1, r, LANES), lambda i: (i, 0, 0))
    return pl.pallas_call(
        functools.partial(_select_kernel, cap=cap), grid=(e,), in_specs=[blk], out_specs=blk,
        out_shape=jax.ShapeDtypeStruct((e, r, LANES), F32),
        compiler_params=_cparams("parallel"), name="select_tokens")(aff3)


def _lists_kernel(sel_ref, aff_ref, idx_ref, dst_ref, gate_ref, ts_ref, ts_s, er_s, *, cap, st):
    e = pl.program_id(0)
    r = sel_ref.shape[1]
    tri_u, tri_l = _tri_matrices(r)

    @pl.when(e == 0)
    def _():
        cnt = sel_ref[0]
        for k in range(1, N_EXPERTS):
            cnt = cnt + sel_ref[k]
        ts, _, _, _ = _prefix_counts(cnt, tri_u, tri_l)
        ts_s[...] = ts
        ts_ref[...] = ts
        er_s[...] = jnp.zeros_like(er_s)

    sel = sel_ref[e]
    _, within, rowoff, rowtot = _prefix_counts(sel, tri_u, tri_l)
    dst = ts_s[...] + er_s[...]
    er_s[...] = er_s[...] + sel

    d2 = jnp.floor(dst * (1.0 / 65536.0))
    rem = dst - 65536.0 * d2
    d1 = jnp.floor(rem * (1.0 / 256.0))
    d0 = rem - 256.0 * d1
    aff = aff_ref[0]
    a_hi = aff.astype(BF16)
    a_r1 = aff - a_hi.astype(F32)
    a_mid = a_r1.astype(BF16)
    a_lo = (a_r1 - a_mid.astype(F32)).astype(BF16)
    rhs = jnp.concatenate([within.astype(BF16), d0.astype(BF16), d1.astype(BF16), d2.astype(BF16),
                           a_hi, a_mid, a_lo], axis=1)
    rowoff_row = rowoff.T[0:1, :]
    rowend_row = rowoff_row + jnp.broadcast_to(rowtot, (r, LANES)).T[0:1, :]
    rho_row = lax.broadcasted_iota(jnp.int32, (1, r), 1).astype(F32)
    lane = lax.broadcasted_iota(jnp.int32, (st, LANES), 1).astype(F32)
    eye = lax.broadcasted_iota(jnp.int32, (LANES, LANES), 0) == lax.broadcasted_iota(
        jnp.int32, (LANES, LANES), 1)

    for t in range(cap // st):
        s_col = (t * st + lax.broadcasted_iota(jnp.int32, (st, 1), 0)).astype(F32)
        in_row = (rowoff_row <= s_col) & (s_col < rowend_row)
        got = jnp.dot(jnp.where(in_row, 1.0, 0.0).astype(BF16), rhs, preferred_element_type=F32)
        base = jnp.sum(jnp.where(in_row, rowoff_row, 0.0), axis=1, keepdims=True)
        rho = jnp.sum(jnp.where(in_row, rho_row, 0.0), axis=1, keepdims=True)
        local = s_col - base
        lam = jnp.sum(jnp.where(got[:, 0:LANES] <= local, 1.0, 0.0), axis=1, keepdims=True)
        plane = lambda k: got[:, k * LANES:(k + 1) * LANES]
        pair = plane(1) + 256.0 * plane(2) + 65536.0 * plane(3)
        at_lam = lane == lam
        dval = jnp.sum(jnp.where(at_lam, pair, 0.0), axis=1, keepdims=True)
        gate_ref[0, t * st:(t + 1) * st, :] = jnp.sum(
            jnp.where(at_lam, plane(4) + plane(5) + plane(6), 0.0), axis=1, keepdims=True)
        ival = rho * float(LANES) + lam
        for j in range(st // LANES):
            seg = slice(j * LANES, (j + 1) * LANES)
            row = t * (st // LANES) + j
            idx_ref[0, row:row + 1, :] = jnp.sum(
                jnp.where(eye, ival[seg], 0.0), axis=0, keepdims=True).astype(jnp.int32)
            dst_ref[0, row:row + 1, :] = jnp.sum(
                jnp.where(eye, dval[seg], 0.0), axis=0, keepdims=True).astype(jnp.int32)


def build_lists(sel3, aff3, cap):
    e, r, _ = sel3.shape
    st = min(512, cap)
    lst = pl.BlockSpec((1, cap // LANES, LANES), lambda i: (i, 0, 0))
    return pl.pallas_call(
        functools.partial(_lists_kernel, cap=cap, st=st), grid=(e,),
        in_specs=[pl.BlockSpec((e, r, LANES), lambda i: (0, 0, 0)),
                  pl.BlockSpec((1, r, LANES), lambda i: (i, 0, 0))],
        out_specs=[lst, lst, pl.BlockSpec((1, cap, 1), lambda i: (i, 0, 0)),
                   pl.BlockSpec((r, LANES), lambda i: (0, 0))],
        out_shape=[jax.ShapeDtypeStruct((e, cap // LANES, LANES), jnp.int32),
                   jax.ShapeDtypeStruct((e, cap // LANES, LANES), jnp.int32),
                   jax.ShapeDtypeStruct((e, cap, 1), F32), jax.ShapeDtypeStruct((r, LANES), F32)],
        scratch_shapes=[pltpu.VMEM((r, LANES), F32), pltpu.VMEM((r, LANES), F32)],
        compiler_params=_cparams("arbitrary"), name="build_lists")(sel3, aff3)


FFN_CHUNKS = tuple((f, min(f + 256, D_FF)) for f in range(0, D_FF, 256))


def _ffn_kernel(idx_first, idx_next_a, idx_next_b, dst_prev_a, dst_prev_b, dst_last,
                x_hbm, gate_a, gate_b, wg_ref, wu_ref, wd_ref, z_hbm,
                xbuf0, xbuf1, ybuf0, ybuf1, acc, gsem, ssem, *, tm, n_grid):
    g = pl.program_id(0)

    def token_tile(t):
        if isinstance(t, int):
            return pl.ds(t * TILE_ROWS, TILE_ROWS)
        return pl.ds(pl.multiple_of(t * TILE_ROWS, TILE_ROWS), TILE_ROWS)

    def gather_row(ids, i, buf, sem):
        pltpu.make_async_copy(x_hbm.at[token_tile(ids[0, 0, i]), :], buf.at[token_tile(i), :], sem).start()

    def scatter_row(dsts, i, buf, sem):
        pltpu.make_async_copy(buf.at[token_tile(i), :], z_hbm.at[token_tile(dsts[0, 0, i]), :], sem).start()

    def wait_rows(buf, sem):
        pltpu.make_async_copy(x_hbm.at[pl.ds(0, tm * TILE_ROWS), :], buf, sem).wait()

    @pl.when(g == 0)
    def _():
        ybuf1[...] = jnp.zeros_like(ybuf1)

        def first(i, carry):
            gather_row(idx_first, i, xbuf0, gsem.at[0])
            return carry
        lax.fori_loop(0, tm, first, 0, unroll=8)

    def tile(xcur, gcur, xnext, gnext, idx_next, yprev, sprev, dst_prev):
        for i in range(tm):
            gather_row(idx_next, i, xnext, gnext)
            scatter_row(dst_prev, i, yprev, sprev)

        wait_rows(xcur, gcur)
        xb = _load_token_tiles(xcur, tm).astype(BF16)
        for c, (f0, f1) in enumerate(FFN_CHUNKS):
            gate = jnp.dot(xb, wg_ref[0, :, f0:f1], preferred_element_type=F32)
            up = jnp.dot(xb, wu_ref[0, :, f0:f1], preferred_element_type=F32)
            h = (gate / (1.0 + jnp.exp(-gate)) * up).astype(BF16)
            part = jnp.dot(h, wd_ref[0, f0:f1, :], preferred_element_type=F32)
            if c == 0:
                acc[...] = part
            else:
                acc[...] += part

    tile(xbuf0, gsem.at[0], xbuf1, gsem.at[1], idx_next_a, ybuf1, ssem.at[1], dst_prev_a)

    @pl.when(g >= 1)
    def _():
        wait_rows(ybuf0, ssem.at[0])
    _store_token_tiles(ybuf0, acc[...] * gate_a[0])

    tile(xbuf1, gsem.at[1], xbuf0, gsem.at[0], idx_next_b, ybuf0, ssem.at[0], dst_prev_b)
    wait_rows(ybuf1, ssem.at[1])
    _store_token_tiles(ybuf1, acc[...] * gate_b[0])

    @pl.when(g == n_grid - 1)
    def _():
        def last(i, carry):
            scatter_row(dst_last, i, ybuf1, ssem.at[1])
            return carry
        lax.fori_loop(0, tm, last, 0, unroll=8)
        wait_rows(ybuf1, ssem.at[1])
        wait_rows(ybuf0, ssem.at[0])
        wait_rows(xbuf0, gsem.at[0])


def expert_ffn(x_tiles, idx, dst, gate, w_gate, w_up, w_down, tm=512):
    d = D_MODEL
    e, cap = idx.shape
    tm = min(tm, cap // 2)
    nt = cap // tm
    assert nt % 2 == 0
    n_tiles = e * nt
    n_grid = n_tiles // 2
    idx3 = idx.reshape(n_tiles, 1, tm)
    gate3 = gate.reshape(n_tiles, tm, 1)
    spare =(e * cap + jnp.arange(tm, dtype=jnp.int32)).reshape(1, 1, tm)
    dst3 = jnp.concatenate([spare, dst.reshape(n_tiles, 1, tm)])
    smem = lambda imap: pl.BlockSpec((1, 1, tm), imap, memory_space=pltpu.SMEM)
    wspec = lambda w: pl.BlockSpec((1,) + w.shape[1:], lambda i: ((2 * i) // nt, 0, 0))
    return pl.pallas_call(
        functools.partial(_ffn_kernel, tm=tm, n_grid=n_grid), grid=(n_grid,),
        in_specs=[smem(lambda i: (0, 0, 0)), smem(lambda i: (2 * i + 1, 0, 0)),
                  smem(lambda i: (jnp.minimum(2 * i + 2, n_tiles - 1), 0, 0)),
                  smem(lambda i: (2 * i, 0, 0)), smem(lambda i: (2 * i + 1, 0, 0)),
                  smem(lambda i: (n_tiles, 0, 0)),
                  pl.BlockSpec(memory_space=pl.ANY),
                  pl.BlockSpec((1, tm, 1), lambda i: (2 * i, 0, 0)),
                  pl.BlockSpec((1, tm, 1), lambda i: (2 * i + 1, 0, 0)),
                  wspec(w_gate), wspec(w_up), wspec(w_down)],
        out_specs=pl.BlockSpec(memory_space=pl.ANY),
        out_shape=jax.ShapeDtypeStruct(((e * cap + tm) * TILE_ROWS, LANES), F32),
        scratch_shapes=[pltpu.VMEM((tm * TILE_ROWS, LANES), F32)] * 4 + [pltpu.VMEM((tm, d), F32)]
        + [pltpu.SemaphoreType.DMA((2,)), pltpu.SemaphoreType.DMA((2,))],
        compiler_params=_cparams("arbitrary"), name="expert_ffn",
    )(idx3, idx3, idx3, dst3, dst3, dst3, x_tiles, gate3, gate3, w_gate, w_up, w_down)


COMBINE_ZB = 256
COMBINE_SLOTS = 4


def _combine_kernel(ts_ref, x_ref, run_ref, g_ref, b_ref, z_hbm, o_ref, zbuf, sem, used, *, z_rows, n_tiles):
    i = pl.program_id(0)
    tt = x_ref.shape[0]

    def chunk_rows(t, c):
        lo = ts_ref[t] + c * COMBINE_ZB
        return lo, jnp.minimum(lo, z_rows - COMBINE_ZB)

    def chunk_tiles(start):
        return pl.ds(pl.multiple_of(start * TILE_ROWS, TILE_ROWS), COMBINE_ZB * TILE_ROWS)

    def chunks_of(t):
        return jnp.maximum((ts_ref[t + 1] - ts_ref[t] + COMBINE_ZB - 1) // COMBINE_ZB, 1)

    def fetch_next():
        t = used[2]

        @pl.when(t < n_tiles)
        def _():
            c = used[3]
            slot = used[1] % COMBINE_SLOTS
            _, start = chunk_rows(t, c)
            pltpu.make_async_copy(z_hbm.at[chunk_tiles(start), :], zbuf.at[slot], sem.at[slot]).start()
            used[1] = used[1] + 1
            last = c + 1 >= chunks_of(t)
            used[2] = jnp.where(last, t + 1, t)
            used[3] = jnp.where(last, 0, c + 1)

    @pl.when(i == 0)
    def _():
        for k in range(4):
            used[k] = 0
        for _ in range(COMBINE_SLOTS - 1):
            fetch_next()

    base = used[0]
    n_chunks = chunks_of(i)
    run_lo = jnp.broadcast_to(run_ref[:, 0:1], (tt, COMBINE_ZB))
    run_hi = jnp.broadcast_to(run_ref[:, 1:2], (tt, COMBINE_ZB))
    col = lax.broadcasted_iota(jnp.int32, (1, COMBINE_ZB), 1).astype(F32)

    def chunk(c, acc):
        slot = (base + c) % COMBINE_SLOTS
        lo, start = chunk_rows(i, c)
        pair = col + start.astype(F32)
        pair = jnp.where(pair >= lo.astype(F32), pair, -1.0)
        a = jnp.where((run_lo <= pair) & (pair < run_hi), 1.0, 0.0).astype(BF16)
        pltpu.make_async_copy(z_hbm.at[pl.ds(0, COMBINE_ZB * TILE_ROWS), :], zbuf.at[slot],
                              sem.at[slot]).wait()
        rows = _load_token_tiles(zbuf.at[slot], COMBINE_ZB).astype(BF16)
        acc = acc + jnp.dot(a, rows, preferred_element_type=F32)
        fetch_next()
        return acc

    ffn = lax.fori_loop(0, n_chunks, chunk, jnp.zeros((tt, D_MODEL), F32))
    used[0] = base + n_chunks
    o_ref[...] = _layer_norm_rows(DN_ALPHA * x_ref[...] + ffn, g_ref[...], b_ref[...])


def combine_ln(x, z, tile_start, runs, g, b, tt=256):
    n, d = x.shape
    n_tiles = n // tt
    row = lambda i, ts: (i, 0)
    fixed = lambda i, ts: (0, 0)
    grid_spec = pltpu.PrefetchScalarGridSpec(
        num_scalar_prefetch=1, grid=(n_tiles,),
        in_specs=[pl.BlockSpec((tt, d), row), pl.BlockSpec((tt, 2), row), pl.BlockSpec((1, d), fixed),
                  pl.BlockSpec((1, d), fixed), pl.BlockSpec(memory_space=pl.ANY)],
        out_specs=pl.BlockSpec((tt, d), row),
        scratch_shapes=[pltpu.VMEM((COMBINE_SLOTS, COMBINE_ZB * TILE_ROWS, LANES), F32),
                        pltpu.SemaphoreType.DMA((COMBINE_SLOTS,)), pltpu.SMEM((4,), jnp.int32)])
    return pl.pallas_call(
        functools.partial(_combine_kernel, z_rows=z.shape[0] // TILE_ROWS, n_tiles=n_tiles), grid_spec=grid_spec,
        out_shape=jax.ShapeDtypeStruct((n, d), F32),
        compiler_params=_cparams("arbitrary"), name="combine_ln",
    )(tile_start, x, runs, g.reshape(1, d), b.reshape(1, d), z)


def moe_layer(x, x_tiles, w_router, w_gate, w_up, w_down, g, b, tt=256):
    n, _ = x.shape
    r = n // LANES
    cap = 2 * n // N_EXPERTS
    aff3 = router_affinities(x, w_router.T).reshape(N_EXPERTS, r, LANES)
    sel3 = select_tokens(aff3, cap)
    idx, dst, gate, ts = build_lists(sel3, aff3, cap)
    z = expert_ffn(x_tiles, idx.reshape(N_EXPERTS, cap), dst.reshape(N_EXPERTS, cap), gate, w_gate, w_up, w_down)
    ts_ext = jnp.concatenate([ts.reshape(n), jnp.full((1,), N_EXPERTS * cap, F32)])
    tile_start = ts_ext[::tt].astype(jnp.int32)
    runs = jnp.stack([ts_ext[:-1], ts_ext[1:]], axis=1)
    return combine_ln(x, z, tile_start, runs, g, b, tt=tt)


def kernel(x_prompt, x_sample, even_w_in, ml_gate_bias, na_rpb, ml_norm_g, even_w_out, da_w_in, da_w_out,
           ln_mix_g, ln_mix_b, ec_router, ec_w_gate, ec_w_up, ec_w_down, ln_ffn_g, ln_ffn_b):
    w_even = even_w_in[0][:, :3584].astype(BF16)
    w_gates = jnp.pad(even_w_in[0][:, 3584:], ((0, 0), (0, LANES - 16))).astype(BF16)
    w_out_a = even_w_out[0][:NA_W].astype(BF16)
    w_out_b = even_w_out[0][NA_W:].astype(BF16)
    w_odd = da_w_in[0].astype(BF16)
    w_odd_out = da_w_out[0].astype(BF16)
    tbl = na_bias_table(na_rpb[0])
    cos_t, sin_t = rope_tables()
    mask_t = da_mask_table()
    moe_w = [(ec_router[l], ec_w_gate[l].astype(BF16), ec_w_up[l].astype(BF16), ec_w_down[l].astype(BF16),
              ln_ffn_g[l], ln_ffn_b[l]) for l in range(DEPTH)]

    def trunk(x):
        b = x.shape[0]
        xt = x.reshape(b * SEQ, D_MODEL)
        proj, gates = in_projection(xt, w_even, w_gates)
        proj = proj.reshape(b, SEQ, -1)
        ya = neighbourhood_attention(proj, tbl)
        yb = mlstm_mixer(proj, gates.reshape(b, SEQ, LANES), ml_gate_bias[0], ml_norm_g[0])
        xt, xt_tiles = out_projection_ln(xt, [ya.reshape(b * SEQ, NA_W), yb.reshape(b * SEQ, ML_W)],
                                         [w_out_a, w_out_b], ln_mix_g[0], ln_mix_b[0])
        xt = moe_layer(xt, xt_tiles, *moe_w[0])
        proj = in_projection(xt, w_odd).reshape(b, SEQ, -1)
        yc = dilated_attention(proj, cos_t, sin_t, mask_t)
        xt, xt_tiles = out_projection_ln(xt, [yc.reshape(b * SEQ, DA_W)], [w_odd_out],
                                         ln_mix_g[1], ln_mix_b[1])
        xt = moe_layer(xt, xt_tiles, *moe_w[1])
        return xt.reshape(b, SEQ, D_MODEL)

    return trunk(x_prompt), trunk(x_sample)
```

```python
import functools

import numpy as np
import jax
import jax.numpy as jnp
from jax import lax
from jax.experimental import pallas as pl
from jax.experimental.pallas import tpu as pltpu

F32 = jnp.float32
BF16 = jnp.bfloat16

D_MODEL = 1024
SEQ = 2048
GRID_W = 64
GRID_ROWS = SEQ // GRID_W
NA_HEADS = 8
NA_W = 512
NA_WIN_R = 8
NA_WIN_C = 16
ML_HEADS = 4
ML_HEAD_DIM = 128
ML_W = 512
ML_CHUNK = 128
N_CHUNKS = SEQ // ML_CHUNK
DA_HEADS = 16
DA_W = 1024
DA_HALF = 64
ROPE_THETA = 10000.0
N_EXPERTS = 16
D_FF = 1408
LN_EPS = 1e-5
DEPTH = 2
DN_ALPHA = (2 * DEPTH) ** 0.25
LANES = 128
NEG = -1e30
LOG2E = 1.4426950408889634
LN2 = 0.6931471805599453
VMEM_LIMIT = 56 * 1024 * 1024


def _cparams(*sem):
    return pltpu.CompilerParams(dimension_semantics=sem, vmem_limit_bytes=VMEM_LIMIT)


def _inproj_kernel(x_ref, w_ref, o_ref, *, n_chunk):
    xb = x_ref[...].astype(BF16)
    for c in range(0, w_ref.shape[1], n_chunk):
        o_ref[:, c:c + n_chunk] = jnp.dot(
            xb, w_ref[:, c:c + n_chunk], preferred_element_type=F32).astype(o_ref.dtype)


def _inproj_gates_kernel(x_ref, w_ref, wg_ref, o_ref, g_ref, *, n_chunk):
    xb = x_ref[...].astype(BF16)
    for c in range(0, w_ref.shape[1], n_chunk):
        o_ref[:, c:c + n_chunk] = jnp.dot(
            xb, w_ref[:, c:c + n_chunk], preferred_element_type=F32).astype(o_ref.dtype)
    g_ref[...] = jnp.dot(xb, wg_ref[...], preferred_element_type=F32)


def in_projection(x, w, wg=None, tm=512, n_chunk=512):
    m, k = x.shape
    n = w.shape[1]
    x_spec = pl.BlockSpec((tm, k), lambda i: (i, 0))
    w_spec = pl.BlockSpec((k, n), lambda i: (0, 0))
    o_spec = pl.BlockSpec((tm, n), lambda i: (i, 0))
    if wg is None:
        return pl.pallas_call(
            functools.partial(_inproj_kernel, n_chunk=n_chunk),
            grid=(m // tm,), in_specs=[x_spec, w_spec], out_specs=o_spec,
            out_shape=jax.ShapeDtypeStruct((m, n), BF16),
            compiler_params=_cparams("parallel"), name="in_projection")(x, w)
    return pl.pallas_call(
        functools.partial(_inproj_gates_kernel, n_chunk=n_chunk),
        grid=(m // tm,),
        in_specs=[x_spec, w_spec, pl.BlockSpec((k, LANES), lambda i: (0, 0))],
        out_specs=[o_spec, pl.BlockSpec((tm, LANES), lambda i: (i, 0))],
        out_shape=[jax.ShapeDtypeStruct((m, n), BF16), jax.ShapeDtypeStruct((m, LANES), F32)],
        compiler_params=_cparams("parallel"), name="in_projection_gates")(x, w, wg)


def _layer_norm_rows(acc, g, b):
    mu = jnp.mean(acc, axis=-1, keepdims=True)
    cen = acc - mu
    var = jnp.mean(cen * cen, axis=-1, keepdims=True)
    return cen * lax.rsqrt(var + LN_EPS) * g + b


def _outproj_ln_kernel(*refs, n_mix):
    x_ref = refs[0]
    mix_refs = refs[1:1 + n_mix]
    w_refs = refs[1 + n_mix:1 + 2 * n_mix]
    g_ref, b_ref, o_ref, ot_ref = refs[1 + 2 * n_mix:]
    acc = DN_ALPHA * x_ref[...]
    for m_ref, w_ref in zip(mix_refs, w_refs):
        acc = acc + jnp.dot(m_ref[...], w_ref[...], preferred_element_type=F32)
    res = _layer_norm_rows(acc, g_ref[...], b_ref[...])
    o_ref[...] = res
    _store_token_tiles(ot_ref, res)


SUBLANES = 8
TILE_ROWS = D_MODEL // LANES


def _store_token_tiles(ref, rows):
    n = rows.shape[0]
    for j in range(TILE_ROWS):
        ref[pl.ds(j, n, stride=TILE_ROWS), :] = rows[:, j * LANES:(j + 1) * LANES]


def _load_token_tiles(ref, n, first_row=0):
    return jnp.concatenate(
        [ref[pl.ds(first_row + j, n, stride=TILE_ROWS), :] for j in range(TILE_ROWS)], axis=1)


def out_projection_ln(x, mixes, ws, g, b, tm=512):
    m, d = x.shape
    n_mix = len(mixes)
    row = lambda i: (i, 0)
    fixed = lambda i: (0, 0)
    in_specs = [pl.BlockSpec((tm, d), row)]
    in_specs += [pl.BlockSpec((tm, mx.shape[1]), row) for mx in mixes]
    in_specs += [pl.BlockSpec(w.shape, fixed) for w in ws]
    in_specs += [pl.BlockSpec((1, d), fixed), pl.BlockSpec((1, d), fixed)]
    return pl.pallas_call(
        functools.partial(_outproj_ln_kernel, n_mix=n_mix),
        grid=(m // tm,), in_specs=in_specs,
        out_specs=[pl.BlockSpec((tm, d), row), pl.BlockSpec((tm * TILE_ROWS, LANES), row)],
        out_shape=[jax.ShapeDtypeStruct((m, d), F32), jax.ShapeDtypeStruct((m * TILE_ROWS, LANES), F32)],
        compiler_params=_cparams("parallel"), name="out_projection_ln",
    )(x, *mixes, *ws, g.reshape(1, d), b.reshape(1, d))


NA_KEYS = NA_WIN_R * GRID_W
NA_CASES = NA_WIN_R


def na_bias_table(rpb):
    j = np.arange(GRID_W)
    kc = np.arange(GRID_W)
    win_c0 = np.clip(j - NA_WIN_C // 2, 0, GRID_W - NA_WIN_C)
    valid = (kc[None, :] >= win_c0[:, None]) & (kc[None, :] < win_c0[:, None] + NA_WIN_C)
    dc = np.clip(kc[None, :] - j[:, None] + NA_WIN_C - 1, 0, 2 * NA_WIN_C - 2)
    n_dc = 2 * NA_WIN_C - 1
    pick = jnp.asarray(dc[:, :, None] == np.arange(n_dc)[None, None, :], F32)
    rows = jnp.einsum('hrd,jkd->hrjk', rpb.astype(F32), pick, precision=lax.Precision.HIGHEST)
    rows = jnp.where(valid[None, None], rows * LOG2E, NEG)
    t = jnp.stack([rows[:, NA_WIN_R - 1 - ci:2 * NA_WIN_R - 1 - ci] for ci in range(NA_CASES)], axis=1)
    t = t.transpose(0, 1, 3, 2, 4).reshape(NA_HEADS // 2, 2, NA_CASES, GRID_W, NA_KEYS)
    return t.transpose(0, 2, 1, 3, 4).reshape(NA_HEADS // 2, NA_CASES, 2 * GRID_W, NA_KEYS)


NA_SCORE_SCALE = 64 ** -0.5 * LOG2E
NA_UNROLL = 8


def _na_kernel(q_ref, k_ref, v_ref, tbl_ref, o_ref, v2a, v2b):
    head0 = lax.broadcasted_iota(jnp.int32, (GRID_W, LANES), 1) < 64

    in0 = lax.broadcasted_iota(jnp.int32, (SEQ, LANES), 1) < 64
    vf = v_ref[0].astype(F32)
    v2a[...] = jnp.concatenate([jnp.where(in0, vf, 0.0), jnp.where(in0, 1.0, 0.0)], axis=1).astype(BF16)
    v2b[...] = jnp.concatenate([jnp.where(in0, 0.0, vf), jnp.where(in0, 0.0, 1.0)], axis=1).astype(BF16)

    def row(r, carry):
        rs = jnp.clip(r - NA_WIN_R // 2, 0, GRID_ROWS - NA_WIN_R)
        q = q_ref[0, pl.ds(pl.multiple_of(r * GRID_W, GRID_W), GRID_W), :]
        zero = jnp.zeros_like(q)
        q2 = jnp.concatenate([jnp.where(head0, q, zero), jnp.where(head0, zero, q)], axis=0)
        win = pl.ds(pl.multiple_of(rs * GRID_W, GRID_W), NA_KEYS)
        s = lax.dot_general(q2, k_ref[0, win, :], (((1,), (1,)), ((), ())), preferred_element_type=F32)
        s = s * NA_SCORE_SCALE + tbl_ref[0, r - rs]
        p = jnp.exp2(s - jnp.max(s, axis=-1, keepdims=True)).astype(BF16)
        p2 = jnp.concatenate([p[:GRID_W], p[GRID_W:]], axis=1)
        ol = jnp.dot(p2, jnp.concatenate([v2a[win, :], v2b[win, :]], axis=0), preferred_element_type=F32)
        o_ref[0, pl.ds(pl.multiple_of(r * GRID_W, GRID_W), GRID_W), :] = (
            ol[:, :LANES] / ol[:, LANES:]).astype(o_ref.dtype)
        return carry

    lax.fori_loop(0, GRID_ROWS, row, 0, unroll=NA_UNROLL)


def neighbourhood_attention(proj, tbl):
    b = proj.shape[0]
    n_hp = NA_HEADS // 2
    blk = lambda off: pl.BlockSpec((1, SEQ, LANES), lambda hp, i, off=off: (i, 0, off + hp))
    return pl.pallas_call(
        _na_kernel, grid=(n_hp, b),
        in_specs=[blk(0), blk(n_hp), blk(2 * n_hp),
                  pl.BlockSpec((1, NA_CASES, 2 * GRID_W, NA_KEYS), lambda hp, i: (hp, 0, 0, 0))],
        out_specs=pl.BlockSpec((1, SEQ, LANES), lambda hp, i: (i, 0, hp)),
        out_shape=jax.ShapeDtypeStruct((b, SEQ, NA_W), BF16),
        scratch_shapes=[pltpu.VMEM((SEQ, 2 * LANES), BF16)] * 2,
        compiler_params=_cparams("parallel", "parallel"), name="neighbourhood_attention",
    )(proj, proj, proj, tbl)


DA_BRANCH_DIL = (1, 4, 16)
DA_QB = 128


def rope_tables():
    lane = np.arange(LANES) % 64
    inv = ROPE_THETA ** (-(2.0 * (lane % 32)) / 64.0)
    ang = jnp.arange(SEQ, dtype=F32)[:, None] * jnp.asarray(inv, F32)[None, :]
    sign = jnp.asarray(np.where(lane < 32, -1.0, 1.0), F32)[None, :]
    return jnp.cos(ang), jnp.sin(ang) * sign


DA_Q_SCALE = 64 ** -0.5 * LOG2E
DA_MASK_CASES = 3
DA_UNROLL = 8


def da_mask_table():
    i = np.arange(2 * DA_QB)[None, :, None] % DA_QB
    j = np.arange(2 * DA_QB)[None, None, :]
    c = np.arange(DA_MASK_CASES)[:, None, None]
    return jnp.asarray(np.where(np.abs(i + DA_HALF * c - j) <= DA_HALF, 0.0, NEG), F32)


def _da_kernel(q_ref, k_ref, v_ref, cos_ref, sin_ref, mask_ref, o_ref, qs, ks, vs, acc_s, lse_s):
    lane_t = lax.broadcasted_iota(jnp.int32, (SEQ, LANES), 1) % 64
    first_half = lane_t < 32

    def rope(x):
        swapped = jnp.where(first_half, pltpu.roll(x, 96, 1), pltpu.roll(x, 32, 1))
        return x * cos_ref[...] + swapped * sin_ref[...]

    qs[...] = rope(q_ref[0].astype(F32)) * DA_Q_SCALE
    ks[...] = rope(k_ref[0].astype(F32))
    vs[...] = v_ref[0].astype(F32)

    head0 = lax.broadcasted_iota(jnp.int32, (DA_QB, LANES), 1) < 64

    def block(g, dil, row0, krow0, nk, case):
        qb = qs[pl.ds(row0, DA_QB, stride=dil), :].astype(BF16)
        kb = ks[pl.ds(krow0, nk, stride=dil), :].astype(BF16)
        vb = vs[pl.ds(krow0, nk, stride=dil), :]
        zero = jnp.zeros_like(qb)
        q2 = jnp.concatenate([jnp.where(head0, qb, zero), jnp.where(head0, zero, qb)], axis=0)
        s = lax.dot_general(q2, kb, (((1,), (1,)), ((), ())), preferred_element_type=F32)
        s = s + mask_ref[case, :, 0:nk]
        m = jnp.max(s, axis=-1, keepdims=True)
        p = jnp.exp2(s - m).astype(BF16)
        in0 = lax.broadcasted_iota(jnp.int32, (nk, LANES), 1) < 64
        v2 = jnp.concatenate([
            jnp.concatenate([jnp.where(in0, vb, 0.0), jnp.where(in0, 1.0, 0.0)], axis=1),
            jnp.concatenate([jnp.where(in0, 0.0, vb), jnp.where(in0, 0.0, 1.0)], axis=1)],
            axis=0).astype(BF16)
        p2 = jnp.concatenate([p[:DA_QB], p[DA_QB:]], axis=1)
        ol = jnp.dot(p2, v2, preferred_element_type=F32)
        l = ol[:, LANES:]
        rows = pl.ds(row0, DA_QB, stride=dil)
        acc_s[g, rows, :] = ol[:, :LANES] / l
        lse_s[g, rows, :] = jnp.where(head0, m[:DA_QB], m[DA_QB:]) + jnp.log(l) * (1.0 / LN2)

    for g, dil in enumerate(DA_BRANCH_DIL):
        n_sub = SEQ // dil
        if n_sub == DA_QB:
            def body(r, carry, g=g, dil=dil):
                block(g, dil, r, r, DA_QB, 0)
                return carry
            lax.fori_loop(0, dil, body, 0, unroll=DA_UNROLL)
        else:
            nb = n_sub // DA_QB
            nk = 2 * DA_QB

            def body(j, carry, g=g, dil=dil, nb=nb, nk=nk, n_sub=n_sub):
                r = j // nb
                q0 = (j % nb) * DA_QB
                k0 = jnp.clip(q0 - DA_HALF, 0, n_sub - nk)
                block(g, dil, r + dil * q0, r + dil * k0, nk, (q0 - k0) // DA_HALF)
                return carry
            lax.fori_loop(0, dil * nb, body, 0, unroll=DA_UNROLL)

    lse_all = jnp.maximum(jnp.maximum(lse_s[0], lse_s[1]), lse_s[2])
    num = jnp.zeros((SEQ, LANES), F32)
    den = jnp.zeros((SEQ, LANES), F32)
    for g in range(len(DA_BRANCH_DIL)):
        w = jnp.exp2(lse_s[g] - lse_all)
        num = num + w * acc_s[g]
        den = den + w
    o_ref[0] = (num / den).astype(o_ref.dtype)


def dilated_attention(proj, cos_t, sin_t, mask_t):
    b = proj.shape[0]
    n_hp = DA_HEADS // 2
    blk = lambda off: pl.BlockSpec((1, SEQ, LANES), lambda i, hp, off=off: (i, 0, off + hp))
    tab = pl.BlockSpec((SEQ, LANES), lambda i, hp: (0, 0))
    nbr = len(DA_BRANCH_DIL)
    return pl.pallas_call(
        _da_kernel, grid=(b, n_hp),
        in_specs=[blk(0), blk(n_hp), blk(2 * n_hp), tab, tab,
                  pl.BlockSpec(mask_t.shape, lambda i, hp: (0, 0, 0))],
        out_specs=pl.BlockSpec((1, SEQ, LANES), lambda i, hp: (i, 0, hp)),
        out_shape=jax.ShapeDtypeStruct((b, SEQ, DA_W), BF16),
        scratch_shapes=[pltpu.VMEM((SEQ, LANES), F32)] * 3 + [pltpu.VMEM((nbr, SEQ, LANES), F32)] * 2,
        compiler_params=_cparams("parallel", "parallel"), name="dilated_attention",
    )(proj, proj, proj, cos_t, sin_t, mask_t)


ML_SCALE = ML_HEAD_DIM ** -0.5
GATE_I_FWD, GATE_I_BWD, GATE_F_FWD, GATE_F_BWD = 0, 4, 8, 12


def _exact_ones_matmul(ones_bf16, x):
    hi = x.astype(BF16)
    r1 = x - hi.astype(F32)
    mid = r1.astype(BF16)
    lo = (r1 - mid.astype(F32)).astype(BF16)
    dot = lambda t: jnp.dot(ones_bf16, t, preferred_element_type=F32)
    return dot(hi) + dot(mid) + dot(lo)


_NT = (((1,), (1,)), ((), ()))
ML_COMBOS = 2 * ML_HEADS
ML_HEADS_PER_TRIP = 4
ROW_CUM, ROW_RMAX, ROW_LOGW, ROW_TOT, ROW_LWMAX = range(5)


def _hi_lo_rows(row):
    hi = row.astype(BF16)
    lo = (row - hi.astype(F32)).astype(BF16)
    return jnp.concatenate([hi, lo, jnp.zeros((SUBLANES - 2, row.shape[1]), BF16)], axis=0)


def _mlstm_chunk(q, k, v_t, log_d_t, rows, state):
    c_t, n_vec, m_run = state
    cum, rmax, logw, tot, lwmax = rows
    log_inter = cum + m_run
    m_t = jnp.maximum(log_inter, rmax)
    w_inter = jnp.exp(log_inter - m_t)
    s_t = lax.dot_general(k, q, _NT, preferred_element_type=F32) * ML_SCALE * jnp.exp(log_d_t - m_t)
    num = w_inter * lax.dot_general(c_t.astype(BF16), q, _NT, preferred_element_type=F32)
    num = num + jnp.dot(v_t, s_t.astype(BF16), preferred_element_type=F32)
    nq = lax.dot_general(_hi_lo_rows(n_vec), q, _NT, preferred_element_type=F32)
    den = w_inter * (nq[0:1] + nq[1:2]) + jnp.sum(s_t, axis=0, keepdims=True)
    h_t = num / jnp.maximum(jnp.abs(den), jnp.exp(-m_t))
    m_new = jnp.maximum(tot + m_run, lwmax)
    w_row = jnp.exp(logw - m_new) * ML_SCALE
    decay = jnp.exp(tot + m_run - m_new)
    vw_t = (v_t.astype(F32) * w_row).astype(BF16)
    c_new = decay * c_t + jnp.dot(vw_t, k, preferred_element_type=F32)
    nk = jnp.dot(_hi_lo_rows(w_row), k, preferred_element_type=F32)
    n_new = decay * n_vec + nk[0:1] + nk[1:2]
    return h_t, (c_new, n_new, m_new)


def _mlstm_kernel(q_ref, k_ref, v_ref, og_ref, g_ref, gb_ref, ng_ref, out_ref, vt_s, ldt_s, rows_s, hf_s, hb_s):
    L = ML_CHUNK
    assert ML_HEAD_DIM == L == LANES
    ri = lax.broadcasted_iota(jnp.int32, (L, L), 0)
    ci = lax.broadcasted_iota(jnp.int32, (L, L), 1)
    lower = ci <= ri
    upper = ci >= ri
    lower_m = jnp.where(lower, 1.0, 0.0).astype(BF16)
    upper_m = jnp.where(upper, 1.0, 0.0).astype(BF16)
    lane = lax.broadcasted_iota(jnp.int32, (L, LANES), 1)
    fwd_rows = lax.broadcasted_iota(jnp.int32, (ML_COMBOS, L), 0) < ML_HEADS

    def prep(c, carry):
        rows = pl.ds(pl.multiple_of(c * L, L), L)
        vt_s[c] = v_ref[0, rows, :].T
        g = g_ref[0, rows, :] + gb_ref[...]
        lf = jnp.minimum(g, 0.0) - jnp.log1p(jnp.exp(-jnp.abs(g)))
        cum_f = _exact_ones_matmul(lower_m, lf)
        suf_b = _exact_ones_matmul(upper_m, lf)
        colv = jnp.where(lane < GATE_F_FWD, g, jnp.where(lane < GATE_F_BWD, cum_f, suf_b))
        rowv = colv.T
        i8 = rowv[0:ML_COMBOS]
        cum8 = rowv[ML_COMBOS:2 * ML_COMBOS]
        r_all = colv - pltpu.roll(colv, LANES - ML_COMBOS, 1)
        tot8 = jnp.where(fwd_rows, jnp.broadcast_to(cum8[:, L - 1:L], (ML_COMBOS, L)),
                         jnp.broadcast_to(cum8[:, 0:1], (ML_COMBOS, L)))
        logw8 = tot8 - cum8 + i8
        lwmax8 = jnp.broadcast_to(jnp.max(logw8, axis=-1, keepdims=True), (ML_COMBOS, L))
        rmax = []
        for j in range(ML_COMBOS):
            valid = upper if j < ML_HEADS else lower
            ldt = jnp.where(valid, jnp.broadcast_to(r_all[:, j:j + 1], (L, L)) + cum8[j:j + 1, :], NEG)
            ldt_s[c * ML_COMBOS + j] = ldt
            rmax.append(jnp.max(ldt, axis=0, keepdims=True))
        rows_s[c, ROW_CUM] = cum8
        rows_s[c, ROW_RMAX] = jnp.concatenate(rmax, axis=0)
        rows_s[c, ROW_LOGW] = logw8
        rows_s[c, ROW_TOT] = tot8
        rows_s[c, ROW_LWMAX] = lwmax8
        return carry

    lax.fori_loop(0, N_CHUNKS, prep, 0)

    zero_state = (jnp.zeros((ML_HEAD_DIM, ML_HEAD_DIM), F32), jnp.zeros((1, ML_HEAD_DIM), F32),
                  jnp.zeros((1, L), F32))
    def one(c, h, backward, state):
        hl = slice(ML_HEAD_DIM * h, ML_HEAD_DIM * (h + 1))
        j = ML_HEADS + h if backward else h
        rows = pl.ds(pl.multiple_of(c * L, L), L)
        prepared = tuple(rows_s[c, k, j:j + 1, :] for k in range(5))
        h_t, state = _mlstm_chunk(q_ref[0, rows, hl], k_ref[0, rows, hl], vt_s[c, hl, :],
                                  ldt_s[c * ML_COMBOS + j], prepared, state)
        (hb_s if backward else hf_s)[h, c] = h_t
        return state

    for h0 in range(0, ML_HEADS, ML_HEADS_PER_TRIP):
        def body(c, states, h0=h0):
            out = []
            for k in range(ML_HEADS_PER_TRIP):
                out.append(one(c, h0 + k, False, states[2 * k]))
                out.append(one(N_CHUNKS - 1 - c, h0 + k, True, states[2 * k + 1]))
            return tuple(out)

        lax.fori_loop(0, N_CHUNKS, body, (zero_state,) * (2 * ML_HEADS_PER_TRIP))

    for h in range(ML_HEADS):
        hl = slice(ML_HEAD_DIM * h, ML_HEAD_DIM * (h + 1))
        gain_t = jnp.broadcast_to(ng_ref[:, hl], (L, ML_HEAD_DIM)).T
        for c in range(N_CHUNKS):
            r0 = c * L
            hh = hf_s[h, c] + hb_s[h, c]
            mu = jnp.mean(hh, axis=0, keepdims=True)
            cen = hh - mu
            var = jnp.mean(cen * cen, axis=0, keepdims=True)
            hn = (cen * lax.rsqrt(var + LN_EPS) * gain_t).T
            og = og_ref[0, r0:r0 + L, hl].astype(F32)
            out_ref[0, r0:r0 + L, hl] = (hn / (1.0 + jnp.exp(-og))).astype(out_ref.dtype)


def mlstm_mixer(proj, gates, gate_bias, norm_g):
    b = proj.shape[0]
    c0 = (3 * NA_W) // ML_W
    blk = lambda off: pl.BlockSpec((1, SEQ, ML_W), lambda i, off=off: (i, 0, off))
    gb = jnp.pad(gate_bias.astype(F32), (0, LANES - gate_bias.shape[0])).reshape(1, LANES)
    return pl.pallas_call(
        _mlstm_kernel, grid=(b,),
        in_specs=[blk(c0), blk(c0 + 1), blk(c0 + 2), blk(c0 + 3),
                  pl.BlockSpec((1, SEQ, LANES), lambda i: (i, 0, 0)),
                  pl.BlockSpec((1, LANES), lambda i: (0, 0)),
                  pl.BlockSpec((1, ML_W), lambda i: (0, 0))],
        out_specs=pl.BlockSpec((1, SEQ, ML_W), lambda i: (i, 0, 0)),
        out_shape=jax.ShapeDtypeStruct((b, SEQ, ML_W), BF16),
        scratch_shapes=[pltpu.VMEM((N_CHUNKS, ML_W, ML_CHUNK), BF16),
                        pltpu.VMEM((N_CHUNKS * ML_COMBOS, ML_CHUNK, ML_CHUNK), F32),
                        pltpu.VMEM((N_CHUNKS, 5, ML_COMBOS, ML_CHUNK), F32),
                        pltpu.VMEM((ML_HEADS, N_CHUNKS, ML_HEAD_DIM, ML_CHUNK), F32),
                        pltpu.VMEM((ML_HEADS, N_CHUNKS, ML_HEAD_DIM, ML_CHUNK), F32)],
        compiler_params=_cparams("parallel"), name="mlstm_mixer",
    )(proj, proj, proj, proj, gates, gb, norm_g.reshape(1, ML_W).astype(F32))


def _router_kernel(x_ref, w_ref, aff_ref):
    logits = lax.dot_general(w_ref[...], x_ref[...], (((1,), (1,)), ((), ())),
                             precision=lax.Precision.HIGHEST, preferred_element_type=F32)
    z = jnp.exp(logits - jnp.max(logits, axis=0, keepdims=True))
    aff_ref[...] = z / jnp.sum(z, axis=0, keepdims=True)


def router_affinities(x, w_router_t, tm=1024):
    n, d = x.shape
    return pl.pallas_call(
        _router_kernel, grid=(n // tm,),
        in_specs=[pl.BlockSpec((tm, d), lambda i: (i, 0)), pl.BlockSpec((N_EXPERTS, d), lambda i: (0, 0))],
        out_specs=pl.BlockSpec((N_EXPERTS, tm), lambda i: (0, i)),
        out_shape=jax.ShapeDtypeStruct((N_EXPERTS, n), F32),
        compiler_params=_cparams("parallel"), name="router_affinities")(x, w_router_t)


def _tri_matrices(r):
    li = lax.broadcasted_iota(jnp.int32, (LANES, LANES), 0)
    lj = lax.broadcasted_iota(jnp.int32, (LANES, LANES), 1)
    tri_u = jnp.where(li <= lj, 1.0, 0.0).astype(BF16)
    ri = lax.broadcasted_iota(jnp.int32, (r, r), 0)
    rj = lax.broadcasted_iota(jnp.int32, (r, r), 1)
    tri_l = jnp.where(rj < ri, 1.0, 0.0).astype(BF16)
    return tri_u, tri_l


def _prefix_counts(mask, tri_u, tri_l):
    r = mask.shape[0]
    within = jnp.dot(mask.astype(BF16), tri_u, preferred_element_type=F32)
    rowtot = within[:, LANES - 1:LANES]
    hi = jnp.floor(rowtot * (1.0 / 16.0))
    lo = rowtot - 16.0 * hi
    hi_b = jnp.broadcast_to(hi, (r, LANES)).astype(BF16)
    lo_b = jnp.broadcast_to(lo, (r, LANES)).astype(BF16)
    rowoff = 16.0 * jnp.dot(tri_l, hi_b, preferred_element_type=F32) + jnp.dot(
        tri_l, lo_b, preferred_element_type=F32)
    return within - mask + rowoff, within, rowoff, rowtot


def _select_kernel(aff_ref, sel_ref, *, cap):
    r = aff_ref.shape[1]
    tri_u, tri_l = _tri_matrices(r)
    bits = pltpu.bitcast(aff_ref[0], jnp.int32)

    def count(m):
        c = jnp.sum(jnp.where(m, 1.0, 0.0), axis=1, keepdims=True)
        return jnp.sum(c, axis=0, keepdims=True)

    def bisect(i, prefix):
        cand = prefix | jnp.left_shift(jnp.int32(1), 30 - i)
        return jnp.where(count(bits >= cand) >= cap, cand, prefix)

    thr = lax.fori_loop(0, 31, bisect, jnp.zeros((1, 1), jnp.int32))
    gt = bits > thr
    eq = bits == thr
    need = cap - count(gt)
    rank_eq, _, _, _ = _prefix_counts(jnp.where(eq, 1.0, 0.0), tri_u, tri_l)
    sel_ref[0] = jnp.where(gt | (eq & (rank_eq < need)), 1.0, 0.0)


def select_tokens(aff3, cap):
    e, r, _ = aff3.shape
    blk = pl.BlockSpec((1, r, LANES), lambda i: (i, 0, 0))
    return pl.pallas_call(
        functools.partial(_select_kernel, cap=cap), grid=(e,), in_specs=[blk], out_specs=blk,
        out_shape=jax.ShapeDtypeStruct((e, r, LANES), F32),
        compiler_params=_cparams("parallel"), name="select_tokens")(aff3)


def _lists_kernel(sel_ref, aff_ref, idx_ref, dst_ref, gate_ref, ts_ref, ts_s, er_s, *, cap, st):
    e = pl.program_id(0)
    r = sel_ref.shape[1]
    tri_u, tri_l = _tri_matrices(r)

    @pl.when(e == 0)
    def _():
        cnt = sel_ref[0]
        for k in range(1, N_EXPERTS):
            cnt = cnt + sel_ref[k]
        ts, _, _, _ = _prefix_counts(cnt, tri_u, tri_l)
        ts_s[...] = ts
        ts_ref[...] = ts
        er_s[...] = jnp.zeros_like(er_s)

    sel = sel_ref[e]
    _, within, rowoff, rowtot = _prefix_counts(sel, tri_u, tri_l)
    dst = ts_s[...] + er_s[...]
    er_s[...] = er_s[...] + sel

    d2 = jnp.floor(dst * (1.0 / 65536.0))
    rem = dst - 65536.0 * d2
    d1 = jnp.floor(rem * (1.0 / 256.0))
    d0 = rem - 256.0 * d1
    aff = aff_ref[0]
    a_hi = aff.astype(BF16)
    a_r1 = aff - a_hi.astype(F32)
    a_mid = a_r1.astype(BF16)
    a_lo = (a_r1 - a_mid.astype(F32)).astype(BF16)
    rhs = jnp.concatenate([within.astype(BF16), d0.astype(BF16), d1.astype(BF16), d2.astype(BF16),
                           a_hi, a_mid, a_lo], axis=1)
    rowoff_row = rowoff.T[0:1, :]
    rowend_row = rowoff_row + jnp.broadcast_to(rowtot, (r, LANES)).T[0:1, :]
    rho_row = lax.broadcasted_iota(jnp.int32, (1, r), 1).astype(F32)
    lane = lax.broadcasted_iota(jnp.int32, (st, LANES), 1).astype(F32)
    eye = lax.broadcasted_iota(jnp.int32, (LANES, LANES), 0) == lax.broadcasted_iota(
        jnp.int32, (LANES, LANES), 1)

    for t in range(cap // st):
        s_col = (t * st + lax.broadcasted_iota(jnp.int32, (st, 1), 0)).astype(F32)
        in_row = (rowoff_row <= s_col) & (s_col < rowend_row)
        got = jnp.dot(jnp.where(in_row, 1.0, 0.0).astype(BF16), rhs, preferred_element_type=F32)
        base = jnp.sum(jnp.where(in_row, rowoff_row, 0.0), axis=1, keepdims=True)
        rho = jnp.sum(jnp.where(in_row, rho_row, 0.0), axis=1, keepdims=True)
        local = s_col - base
        lam = jnp.sum(jnp.where(got[:, 0:LANES] <= local, 1.0, 0.0), axis=1, keepdims=True)
        plane = lambda k: got[:, k * LANES:(k + 1) * LANES]
        pair = plane(1) + 256.0 * plane(2) + 65536.0 * plane(3)
        at_lam = lane == lam
        dval = jnp.sum(jnp.where(at_lam, pair, 0.0), axis=1, keepdims=True)
        gate_ref[0, t * st:(t + 1) * st, :] = jnp.sum(
            jnp.where(at_lam, plane(4) + plane(5) + plane(6), 0.0), axis=1, keepdims=True)
        ival = rho * float(LANES) + lam
        for j in range(st // LANES):
            seg = slice(j * LANES, (j + 1) * LANES)
            row = t * (st // LANES) + j
            idx_ref[0, row:row + 1, :] = jnp.sum(
                jnp.where(eye, ival[seg], 0.0), axis=0, keepdims=True).astype(jnp.int32)
            dst_ref[0, row:row + 1, :] = jnp.sum(
                jnp.where(eye, dval[seg], 0.0), axis=0, keepdims=True).astype(jnp.int32)


def build_lists(sel3, aff3, cap):
    e, r, _ = sel3.shape
    st = min(512, cap)
    lst = pl.BlockSpec((1, cap // LANES, LANES), lambda i: (i, 0, 0))
    return pl.pallas_call(
        functools.partial(_lists_kernel, cap=cap, st=st), grid=(e,),
        in_specs=[pl.BlockSpec((e, r, LANES), lambda i: (0, 0, 0)),
                  pl.BlockSpec((1, r, LANES), lambda i: (i, 0, 0))],
        out_specs=[lst, lst, pl.BlockSpec((1, cap, 1), lambda i: (i, 0, 0)),
                   pl.BlockSpec((r, LANES), lambda i: (0, 0))],
        out_shape=[jax.ShapeDtypeStruct((e, cap // LANES, LANES), jnp.int32),
                   jax.ShapeDtypeStruct((e, cap // LANES, LANES), jnp.int32),
                   jax.ShapeDtypeStruct((e, cap, 1), F32), jax.ShapeDtypeStruct((r, LANES), F32)],
        scratch_shapes=[pltpu.VMEM((r, LANES), F32), pltpu.VMEM((r, LANES), F32)],
        compiler_params=_cparams("arbitrary"), name="build_lists")(sel3, aff3)


FFN_CHUNKS = tuple((f, min(f + 256, D_FF)) for f in range(0, D_FF, 256))


def _ffn_kernel(idx_first, idx_next_a, idx_next_b, dst_prev_a, dst_prev_b, dst_last,
                x_hbm, gate_a, gate_b, wg_ref, wu_ref, wd_ref, z_hbm,
                xbuf0, xbuf1, ybuf0, ybuf1, acc, gsem, ssem, *, tm, n_grid):
    g = pl.program_id(0)

    def token_tile(t):
        if isinstance(t, int):
            return pl.ds(t * TILE_ROWS, TILE_ROWS)
        return pl.ds(pl.multiple_of(t * TILE_ROWS, TILE_ROWS), TILE_ROWS)

    def gather_row(ids, i, buf, sem):
        pltpu.make_async_copy(x_hbm.at[token_tile(ids[0, 0, i]), :], buf.at[token_tile(i), :],
                              sem).start(priority=1)

    def scatter_row(dsts, i, buf, sem):
        pltpu.make_async_copy(buf.at[token_tile(i), :], z_hbm.at[token_tile(dsts[0, 0, i]), :],
                              sem).start(priority=0)

    def wait_rows(buf, sem):
        pltpu.make_async_copy(x_hbm.at[pl.ds(0, tm * TILE_ROWS), :], buf, sem).wait()

    @pl.when(g == 0)
    def _():
        ybuf1[...] = jnp.zeros_like(ybuf1)

        def first(i, carry):
            gather_row(idx_first, i, xbuf0, gsem.at[0])
            return carry
        lax.fori_loop(0, tm, first, 0, unroll=8)

    def tile(xcur, gcur, xnext, gnext, idx_next, yprev, sprev, dst_prev):
        for i in range(tm):
            gather_row(idx_next, i, xnext, gnext)
            scatter_row(dst_prev, i, yprev, sprev)

        wait_rows(xcur, gcur)
        xb = _load_token_tiles(xcur, tm).astype(BF16)
        for c, (f0, f1) in enumerate(FFN_CHUNKS):
            gate = jnp.dot(xb, wg_ref[0, :, f0:f1], preferred_element_type=F32)
            up = jnp.dot(xb, wu_ref[0, :, f0:f1], preferred_element_type=F32)
            h = (gate / (1.0 + jnp.exp(-gate)) * up).astype(BF16)
            part = jnp.dot(h, wd_ref[0, f0:f1, :], preferred_element_type=F32)
            if c == 0:
                acc[...] = part
            else:
                acc[...] += part

    tile(xbuf0, gsem.at[0], xbuf1, gsem.at[1], idx_next_a, ybuf1, ssem.at[1], dst_prev_a)

    @pl.when(g >= 1)
    def _():
        wait_rows(ybuf0, ssem.at[0])
    _store_token_tiles(ybuf0, acc[...] * gate_a[0])

    tile(xbuf1, gsem.at[1], xbuf0, gsem.at[0], idx_next_b, ybuf0, ssem.at[0], dst_prev_b)
    wait_rows(ybuf1, ssem.at[1])
    _store_token_tiles(ybuf1, acc[...] * gate_b[0])

    @pl.when(g == n_grid - 1)
    def _():
        def last(i, carry):
            scatter_row(dst_last, i, ybuf1, ssem.at[1])
            return carry
        lax.fori_loop(0, tm, last, 0, unroll=8)
        wait_rows(ybuf1, ssem.at[1])
        wait_rows(ybuf0, ssem.at[0])
        wait_rows(xbuf0, gsem.at[0])


def expert_ffn(x_tiles, idx, dst, gate, w_gate, w_up, w_down, tm=512):
    d = D_MODEL
    e, cap = idx.shape
    tm = min(tm, cap // 2)
    nt = cap // tm
    assert nt % 2 == 0
    n_tiles = e * nt
    n_grid = n_tiles // 2
    idx3 = idx.reshape(n_tiles, 1, tm)
    gate3 = gate.reshape(n_tiles, tm, 1)
    spare =(e * cap + jnp.arange(tm, dtype=jnp.int32)).reshape(1, 1, tm)
    dst3 = jnp.concatenate([spare, dst.reshape(n_tiles, 1, tm)])
    smem = lambda imap: pl.BlockSpec((1, 1, tm), imap, memory_space=pltpu.SMEM)
    wspec = lambda w: pl.BlockSpec((1,) + w.shape[1:], lambda i: ((2 * i) // nt, 0, 0))
    return pl.pallas_call(
        functools.partial(_ffn_kernel, tm=tm, n_grid=n_grid), grid=(n_grid,),
        in_specs=[smem(lambda i: (0, 0, 0)), smem(lambda i: (2 * i + 1, 0, 0)),
                  smem(lambda i: (jnp.minimum(2 * i + 2, n_tiles - 1), 0, 0)),
                  smem(lambda i: (2 * i, 0, 0)), smem(lambda i: (2 * i + 1, 0, 0)),
                  smem(lambda i: (n_tiles, 0, 0)),
                  pl.BlockSpec(memory_space=pl.ANY),
                  pl.BlockSpec((1, tm, 1), lambda i: (2 * i, 0, 0)),
                  pl.BlockSpec((1, tm, 1), lambda i: (2 * i + 1, 0, 0)),
                  wspec(w_gate), wspec(w_up), wspec(w_down)],
        out_specs=pl.BlockSpec(memory_space=pl.ANY),
        out_shape=jax.ShapeDtypeStruct(((e * cap + tm) * TILE_ROWS, LANES), F32),
        scratch_shapes=[pltpu.VMEM((tm * TILE_ROWS, LANES), F32)] * 4 + [pltpu.VMEM((tm, d), F32)]
        + [pltpu.SemaphoreType.DMA((2,)), pltpu.SemaphoreType.DMA((2,))],
        compiler_params=_cparams("arbitrary"), name="expert_ffn",
    )(idx3, idx3, idx3, dst3, dst3, dst3, x_tiles, gate3, gate3, w_gate, w_up, w_down)


COMBINE_ZB = 256
COMBINE_SLOTS = 4


def _combine_kernel(ts_ref, x_ref, run_ref, g_ref, b_ref, z_hbm, o_ref, zbuf, sem, used, *, z_rows, n_tiles):
    i = pl.program_id(0)
    tt = x_ref.shape[0]

    def chunk_rows(t, c):
        lo = ts_ref[t] + c * COMBINE_ZB
        return lo, jnp.minimum(lo, z_rows - COMBINE_ZB)

    def chunk_tiles(start):
        return pl.ds(pl.multiple_of(start * TILE_ROWS, TILE_ROWS), COMBINE_ZB * TILE_ROWS)

    def chunks_of(t):
        return jnp.maximum((ts_ref[t + 1] - ts_ref[t] + COMBINE_ZB - 1) // COMBINE_ZB, 1)

    def fetch_next():
        t = used[2]

        @pl.when(t < n_tiles)
        def _():
            c = used[3]
            slot = used[1] % COMBINE_SLOTS
            _, start = chunk_rows(t, c)
            pltpu.make_async_copy(z_hbm.at[chunk_tiles(start), :], zbuf.at[slot], sem.at[slot]).start()
            used[1] = used[1] + 1
            last = c + 1 >= chunks_of(t)
            used[2] = jnp.where(last, t + 1, t)
            used[3] = jnp.where(last, 0, c + 1)

    @pl.when(i == 0)
    def _():
        for k in range(4):
            used[k] = 0
        for _ in range(COMBINE_SLOTS - 1):
            fetch_next()

    base = used[0]
    n_chunks = chunks_of(i)
    run_lo = jnp.broadcast_to(run_ref[:, 0:1], (tt, COMBINE_ZB))
    run_hi = jnp.broadcast_to(run_ref[:, 1:2], (tt, COMBINE_ZB))
    col = lax.broadcasted_iota(jnp.int32, (1, COMBINE_ZB), 1).astype(F32)

    def chunk(c, acc):
        slot = (base + c) % COMBINE_SLOTS
        lo, start = chunk_rows(i, c)
        pair = col + start.astype(F32)
        pair = jnp.where(pair >= lo.astype(F32), pair, -1.0)
        a = jnp.where((run_lo <= pair) & (pair < run_hi), 1.0, 0.0).astype(BF16)
        pltpu.make_async_copy(z_hbm.at[pl.ds(0, COMBINE_ZB * TILE_ROWS), :], zbuf.at[slot],
                              sem.at[slot]).wait()
        rows = _load_token_tiles(zbuf.at[slot], COMBINE_ZB).astype(BF16)
        acc = acc + jnp.dot(a, rows, preferred_element_type=F32)
        fetch_next()
        return acc

    ffn = lax.fori_loop(0, n_chunks, chunk, jnp.zeros((tt, D_MODEL), F32))
    used[0] = base + n_chunks
    o_ref[...] = _layer_norm_rows(DN_ALPHA * x_ref[...] + ffn, g_ref[...], b_ref[...])


def combine_ln(x, z, tile_start, runs, g, b, tt=256):
    n, d = x.shape
    n_tiles = n // tt
    row = lambda i, ts: (i, 0)
    fixed = lambda i, ts: (0, 0)
    grid_spec = pltpu.PrefetchScalarGridSpec(
        num_scalar_prefetch=1, grid=(n_tiles,),
        in_specs=[pl.BlockSpec((tt, d), row), pl.BlockSpec((tt, 2), row), pl.BlockSpec((1, d), fixed),
                  pl.BlockSpec((1, d), fixed), pl.BlockSpec(memory_space=pl.ANY)],
        out_specs=pl.BlockSpec((tt, d), row),
        scratch_shapes=[pltpu.VMEM((COMBINE_SLOTS, COMBINE_ZB * TILE_ROWS, LANES), F32),
                        pltpu.SemaphoreType.DMA((COMBINE_SLOTS,)), pltpu.SMEM((4,), jnp.int32)])
    return pl.pallas_call(
        functools.partial(_combine_kernel, z_rows=z.shape[0] // TILE_ROWS, n_tiles=n_tiles), grid_spec=grid_spec,
        out_shape=jax.ShapeDtypeStruct((n, d), F32),
        compiler_params=_cparams("arbitrary"), name="combine_ln",
    )(tile_start, x, runs, g.reshape(1, d), b.reshape(1, d), z)


def moe_layer(x, x_tiles, w_router, w_gate, w_up, w_down, g, b, tt=256):
    n, _ = x.shape
    r = n // LANES
    cap = 2 * n // N_EXPERTS
    aff3 = router_affinities(x, w_router.T).reshape(N_EXPERTS, r, LANES)
    sel3 = select_tokens(aff3, cap)
    idx, dst, gate, ts = build_lists(sel3, aff3, cap)
    z = expert_ffn(x_tiles, idx.reshape(N_EXPERTS, cap), dst.reshape(N_EXPERTS, cap), gate, w_gate, w_up, w_down)
    ts_ext = jnp.concatenate([ts.reshape(n), jnp.full((1,), N_EXPERTS * cap, F32)])
    tile_start = ts_ext[::tt].astype(jnp.int32)
    runs = jnp.stack([ts_ext[:-1], ts_ext[1:]], axis=1)
    return combine_ln(x, z, tile_start, runs, g, b, tt=tt)


def kernel(x_prompt, x_sample, even_w_in, ml_gate_bias, na_rpb, ml_norm_g, even_w_out, da_w_in, da_w_out,
           ln_mix_g, ln_mix_b, ec_router, ec_w_gate, ec_w_up, ec_w_down, ln_ffn_g, ln_ffn_b):
    w_even = even_w_in[0][:, :3584].astype(BF16)
    w_gates = jnp.pad(even_w_in[0][:, 3584:], ((0, 0), (0, LANES - 16))).astype(BF16)
    w_out_a = even_w_out[0][:NA_W].astype(BF16)
    w_out_b = even_w_out[0][NA_W:].astype(BF16)
    w_odd = da_w_in[0].astype(BF16)
    w_odd_out = da_w_out[0].astype(BF16)
    tbl = na_bias_table(na_rpb[0])
    cos_t, sin_t = rope_tables()
    mask_t = da_mask_table()
    moe_w = [(ec_router[l], ec_w_gate[l].astype(BF16), ec_w_up[l].astype(BF16), ec_w_down[l].astype(BF16),
              ln_ffn_g[l], ln_ffn_b[l]) for l in range(DEPTH)]

    def trunk(x):
        b = x.shape[0]
        xt = x.reshape(b * SEQ, D_MODEL)
        proj, gates = in_projection(xt, w_even, w_gates)
        proj = proj.reshape(b, SEQ, -1)
        ya = neighbourhood_attention(proj, tbl)
        yb = mlstm_mixer(proj, gates.reshape(b, SEQ, LANES), ml_gate_bias[0], ml_norm_g[0])
        xt, xt_tiles = out_projection_ln(xt, [ya.reshape(b * SEQ, NA_W), yb.reshape(b * SEQ, ML_W)],
                                         [w_out_a, w_out_b], ln_mix_g[0], ln_mix_b[0])
        xt = moe_layer(xt, xt_tiles, *moe_w[0])
        proj = in_projection(xt, w_odd).reshape(b, SEQ, -1)
        yc = dilated_attention(proj, cos_t, sin_t, mask_t)
        xt, xt_tiles = out_projection_ln(xt, [yc.reshape(b * SEQ, DA_W)], [w_odd_out],
                                         ln_mix_g[1], ln_mix_b[1])
        xt = moe_layer(xt, xt_tiles, *moe_w[1])
        return xt.reshape(b, SEQ, D_MODEL)

    return trunk(x_prompt), trunk(x_sample)
```

```python
import functools

import numpy as np
import jax
import jax.numpy as jnp
from jax import lax
from jax.experimental import pallas as pl
from jax.experimental.pallas import tpu as pltpu

F32 = jnp.float32
BF16 = jnp.bfloat16

D_MODEL = 1024
SEQ = 2048
GRID_W = 64
GRID_ROWS = SEQ // GRID_W
NA_HEADS = 8
NA_W = 512
NA_WIN_R = 8
NA_WIN_C = 16
ML_HEADS = 4
ML_HEAD_DIM = 128
ML_W = 512
ML_CHUNK = 128
N_CHUNKS = SEQ // ML_CHUNK
DA_HEADS = 16
DA_W = 1024
DA_HALF = 64
ROPE_THETA = 10000.0
N_EXPERTS = 16
D_FF = 1408
LN_EPS = 1e-5
DEPTH = 2
DN_ALPHA = (2 * DEPTH) ** 0.25
LANES = 128
NEG = -1e30
LOG2E = 1.4426950408889634
LN2 = 0.6931471805599453
VMEM_LIMIT = 56 * 1024 * 1024


def _cparams(*sem):
    return pltpu.CompilerParams(dimension_semantics=sem, vmem_limit_bytes=VMEM_LIMIT)


def _inproj_kernel(x_ref, w_ref, o_ref, *, n_chunk):
    xb = x_ref[...].astype(BF16)
    for c in range(0, w_ref.shape[1], n_chunk):
        o_ref[:, c:c + n_chunk] = jnp.dot(
            xb, w_ref[:, c:c + n_chunk], preferred_element_type=F32).astype(o_ref.dtype)


def _inproj_gates_kernel(x_ref, w_ref, wg_ref, o_ref, g_ref, *, n_chunk):
    xb = x_ref[...].astype(BF16)
    for c in range(0, w_ref.shape[1], n_chunk):
        o_ref[:, c:c + n_chunk] = jnp.dot(
            xb, w_ref[:, c:c + n_chunk], preferred_element_type=F32).astype(o_ref.dtype)
    g_ref[...] = jnp.dot(xb, wg_ref[...], preferred_element_type=F32)


def _rotate_half_pairs(x, cos, sin_signed):
    first_half = lax.broadcasted_iota(jnp.int32, x.shape, 1) % 64 < 32
    swapped = jnp.where(first_half, pltpu.roll(x, 96, 1), pltpu.roll(x, 32, 1))
    return x * cos + swapped * sin_signed


def _inproj_rope_kernel(x_ref, w_ref, rope_ref, o_ref, *, n_chunk, width):
    xb = x_ref[...].astype(BF16)
    for c in range(0, w_ref.shape[1], n_chunk):
        res = jnp.dot(xb, w_ref[:, c:c + n_chunk], preferred_element_type=F32)
        if c < 2 * width:
            t = 0 if c < width else 2
            res = jnp.concatenate(
                [_rotate_half_pairs(res[:, l:l + LANES], rope_ref[t], rope_ref[t + 1])
                 for l in range(0, n_chunk, LANES)], axis=1)
        o_ref[:, c:c + n_chunk] = res.astype(o_ref.dtype)


def in_projection_rope(x, w, rope_tables, width, tm=512, n_chunk=512):
    m, k = x.shape
    n = w.shape[1]
    assert width % n_chunk == 0 and SEQ % tm == 0
    return pl.pallas_call(
        functools.partial(_inproj_rope_kernel, n_chunk=n_chunk, width=width),
        grid=(m // tm,),
        in_specs=[pl.BlockSpec((tm, k), lambda i: (i, 0)), pl.BlockSpec((k, n), lambda i: (0, 0)),
                  pl.BlockSpec((4, tm, LANES), lambda i: (0, i % (SEQ // tm), 0))],
        out_specs=pl.BlockSpec((tm, n), lambda i: (i, 0)),
        out_shape=jax.ShapeDtypeStruct((m, n), BF16),
        compiler_params=_cparams("parallel"), name="in_projection_rope")(x, w, rope_tables)


def in_projection(x, w, wg=None, tm=512, n_chunk=512):
    m, k = x.shape
    n = w.shape[1]
    x_spec = pl.BlockSpec((tm, k), lambda i: (i, 0))
    w_spec = pl.BlockSpec((k, n), lambda i: (0, 0))
    o_spec = pl.BlockSpec((tm, n), lambda i: (i, 0))
    if wg is None:
        return pl.pallas_call(
            functools.partial(_inproj_kernel, n_chunk=n_chunk),
            grid=(m // tm,), in_specs=[x_spec, w_spec], out_specs=o_spec,
            out_shape=jax.ShapeDtypeStruct((m, n), BF16),
            compiler_params=_cparams("parallel"), name="in_projection")(x, w)
    return pl.pallas_call(
        functools.partial(_inproj_gates_kernel, n_chunk=n_chunk),
        grid=(m // tm,),
        in_specs=[x_spec, w_spec, pl.BlockSpec((k, LANES), lambda i: (0, 0))],
        out_specs=[o_spec, pl.BlockSpec((tm, LANES), lambda i: (i, 0))],
        out_shape=[jax.ShapeDtypeStruct((m, n), BF16), jax.ShapeDtypeStruct((m, LANES), F32)],
        compiler_params=_cparams("parallel"), name="in_projection_gates")(x, w, wg)


def _layer_norm_rows(acc, g, b):
    mu = jnp.mean(acc, axis=-1, keepdims=True)
    cen = acc - mu
    var = jnp.mean(cen * cen, axis=-1, keepdims=True)
    return cen * lax.rsqrt(var + LN_EPS) * g + b


def _outproj_ln_kernel(*refs, n_mix):
    x_ref = refs[0]
    mix_refs = refs[1:1 + n_mix]
    w_refs = refs[1 + n_mix:1 + 2 * n_mix]
    g_ref, b_ref, o_ref, ot_ref = refs[1 + 2 * n_mix:]
    acc = DN_ALPHA * x_ref[...]
    for m_ref, w_ref in zip(mix_refs, w_refs):
        acc = acc + jnp.dot(m_ref[...], w_ref[...], preferred_element_type=F32)
    res = _layer_norm_rows(acc, g_ref[...], b_ref[...])
    o_ref[...] = res
    _store_token_tiles(ot_ref, res)


SUBLANES = 8
TILE_ROWS = D_MODEL // LANES


def _store_token_tiles(ref, rows):
    n = rows.shape[0]
    for j in range(TILE_ROWS):
        ref[pl.ds(j, n, stride=TILE_ROWS), :] = rows[:, j * LANES:(j + 1) * LANES]


def _load_token_tiles(ref, n, first_row=0):
    return jnp.concatenate(
        [ref[pl.ds(first_row + j, n, stride=TILE_ROWS), :] for j in range(TILE_ROWS)], axis=1)


def out_projection_ln(x, mixes, ws, g, b, tm=512):
    m, d = x.shape
    n_mix = len(mixes)
    row = lambda i: (i, 0)
    fixed = lambda i: (0, 0)
    in_specs = [pl.BlockSpec((tm, d), row)]
    in_specs += [pl.BlockSpec((tm, mx.shape[1]), row) for mx in mixes]
    in_specs += [pl.BlockSpec(w.shape, fixed) for w in ws]
    in_specs += [pl.BlockSpec((1, d), fixed), pl.BlockSpec((1, d), fixed)]
    return pl.pallas_call(
        functools.partial(_outproj_ln_kernel, n_mix=n_mix),
        grid=(m // tm,), in_specs=in_specs,
        out_specs=[pl.BlockSpec((tm, d), row), pl.BlockSpec((tm * TILE_ROWS, LANES), row)],
        out_shape=[jax.ShapeDtypeStruct((m, d), F32), jax.ShapeDtypeStruct((m * TILE_ROWS, LANES), F32)],
        compiler_params=_cparams("parallel"), name="out_projection_ln",
    )(x, *mixes, *ws, g.reshape(1, d), b.reshape(1, d))


NA_KEYS = NA_WIN_R * GRID_W
NA_CASES = NA_WIN_R


def na_bias_table(rpb):
    j = np.arange(GRID_W)
    kc = np.arange(GRID_W)
    win_c0 = np.clip(j - NA_WIN_C // 2, 0, GRID_W - NA_WIN_C)
    valid = (kc[None, :] >= win_c0[:, None]) & (kc[None, :] < win_c0[:, None] + NA_WIN_C)
    dc = np.clip(kc[None, :] - j[:, None] + NA_WIN_C - 1, 0, 2 * NA_WIN_C - 2)
    n_dc = 2 * NA_WIN_C - 1
    pick = jnp.asarray(dc[:, :, None] == np.arange(n_dc)[None, None, :], F32)
    rows = jnp.einsum('hrd,jkd->hrjk', rpb.astype(F32), pick, precision=lax.Precision.HIGHEST)
    rows = jnp.where(valid[None, None], rows * LOG2E, NEG)
    t = jnp.stack([rows[:, NA_WIN_R - 1 - ci:2 * NA_WIN_R - 1 - ci] for ci in range(NA_CASES)], axis=1)
    t = t.transpose(0, 1, 3, 2, 4).reshape(NA_HEADS // 2, 2, NA_CASES, GRID_W, NA_KEYS)
    return t.transpose(0, 2, 1, 3, 4).reshape(NA_HEADS // 2, NA_CASES, 2 * GRID_W, NA_KEYS)


NA_SCORE_SCALE = 64 ** -0.5 * LOG2E
NA_UNROLL = 8


def _na_kernel(q_ref, k_ref, v_ref, tbl_ref, o_ref, v2a, v2b):
    head0 = lax.broadcasted_iota(jnp.int32, (GRID_W, LANES), 1) < 64

    in0 = lax.broadcasted_iota(jnp.int32, (SEQ, LANES), 1) < 64
    vf = v_ref[0].astype(F32)
    v2a[...] = jnp.concatenate([jnp.where(in0, vf, 0.0), jnp.where(in0, 1.0, 0.0)], axis=1).astype(BF16)
    v2b[...] = jnp.concatenate([jnp.where(in0, 0.0, vf), jnp.where(in0, 0.0, 1.0)], axis=1).astype(BF16)

    def row(r, carry):
        rs = jnp.clip(r - NA_WIN_R // 2, 0, GRID_ROWS - NA_WIN_R)
        q = q_ref[0, pl.ds(pl.multiple_of(r * GRID_W, GRID_W), GRID_W), :]
        zero = jnp.zeros_like(q)
        q2 = jnp.concatenate([jnp.where(head0, q, zero), jnp.where(head0, zero, q)], axis=0)
        win = pl.ds(pl.multiple_of(rs * GRID_W, GRID_W), NA_KEYS)
        s = lax.dot_general(q2, k_ref[0, win, :], (((1,), (1,)), ((), ())), preferred_element_type=F32)
        s = s * NA_SCORE_SCALE + tbl_ref[0, r - rs]
        p = jnp.exp2(s - jnp.max(s, axis=-1, keepdims=True)).astype(BF16)
        p2 = jnp.concatenate([p[:GRID_W], p[GRID_W:]], axis=1)
        ol = jnp.dot(p2, jnp.concatenate([v2a[win, :], v2b[win, :]], axis=0), preferred_element_type=F32)
        o_ref[0, pl.ds(pl.multiple_of(r * GRID_W, GRID_W), GRID_W), :] = (
            ol[:, :LANES] / ol[:, LANES:]).astype(o_ref.dtype)
        return carry

    lax.fori_loop(0, GRID_ROWS, row, 0, unroll=NA_UNROLL)


def neighbourhood_attention(proj, tbl):
    b = proj.shape[0]
    n_hp = NA_HEADS // 2
    blk = lambda off: pl.BlockSpec((1, SEQ, LANES), lambda hp, i, off=off: (i, 0, off + hp))
    return pl.pallas_call(
        _na_kernel, grid=(n_hp, b),
        in_specs=[blk(0), blk(n_hp), blk(2 * n_hp),
                  pl.BlockSpec((1, NA_CASES, 2 * GRID_W, NA_KEYS), lambda hp, i: (hp, 0, 0, 0))],
        out_specs=pl.BlockSpec((1, SEQ, LANES), lambda hp, i: (i, 0, hp)),
        out_shape=jax.ShapeDtypeStruct((b, SEQ, NA_W), BF16),
        scratch_shapes=[pltpu.VMEM((SEQ, 2 * LANES), BF16)] * 2,
        compiler_params=_cparams("parallel", "parallel"), name="neighbourhood_attention",
    )(proj, proj, proj, tbl)


DA_BRANCH_DIL = (1, 4, 16)
DA_QB = 128


DA_Q_SCALE = 64 ** -0.5 * LOG2E


def rope_tables():
    lane = np.arange(LANES) % 64
    inv = ROPE_THETA ** (-(2.0 * (lane % 32)) / 64.0)
    ang = jnp.arange(SEQ, dtype=F32)[:, None] * jnp.asarray(inv, F32)[None, :]
    sign = jnp.asarray(np.where(lane < 32, -1.0, 1.0), F32)[None, :]
    cos, sin = jnp.cos(ang), jnp.sin(ang) * sign
    return jnp.stack([cos * DA_Q_SCALE, sin * DA_Q_SCALE, cos, sin])
DA_MASK_CASES = 3
DA_UNROLL = 8


def da_mask_table():
    i = np.arange(2 * DA_QB)[None, :, None] % DA_QB
    j = np.arange(2 * DA_QB)[None, None, :]
    c = np.arange(DA_MASK_CASES)[:, None, None]
    return jnp.asarray(np.where(np.abs(i + DA_HALF * c - j) <= DA_HALF, 0.0, NEG), F32)


def _da_kernel(q_ref, k_ref, v_ref, mask_ref, o_ref, qs, ks, vs, acc_s, lse_s):
    qs[...] = q_ref[0].astype(F32)
    ks[...] = k_ref[0].astype(F32)
    vs[...] = v_ref[0].astype(F32)

    head0 = lax.broadcasted_iota(jnp.int32, (DA_QB, LANES), 1) < 64

    def block(g, dil, row0, krow0, nk, case):
        qb = qs[pl.ds(row0, DA_QB, stride=dil), :].astype(BF16)
        kb = ks[pl.ds(krow0, nk, stride=dil), :].astype(BF16)
        vb = vs[pl.ds(krow0, nk, stride=dil), :]
        zero = jnp.zeros_like(qb)
        q2 = jnp.concatenate([jnp.where(head0, qb, zero), jnp.where(head0, zero, qb)], axis=0)
        s = lax.dot_general(q2, kb, (((1,), (1,)), ((), ())), preferred_element_type=F32)
        s = s + mask_ref[case, :, 0:nk]
        m = jnp.max(s, axis=-1, keepdims=True)
        p = jnp.exp2(s - m).astype(BF16)
        in0 = lax.broadcasted_iota(jnp.int32, (nk, LANES), 1) < 64
        v2 = jnp.concatenate([
            jnp.concatenate([jnp.where(in0, vb, 0.0), jnp.where(in0, 1.0, 0.0)], axis=1),
            jnp.concatenate([jnp.where(in0, 0.0, vb), jnp.where(in0, 0.0, 1.0)], axis=1)],
            axis=0).astype(BF16)
        p2 = jnp.concatenate([p[:DA_QB], p[DA_QB:]], axis=1)
        ol = jnp.dot(p2, v2, preferred_element_type=F32)
        l = ol[:, LANES:]
        rows = pl.ds(row0, DA_QB, stride=dil)
        acc_s[g, rows, :] = ol[:, :LANES] / l
        lse_s[g, rows, :] = jnp.where(head0, m[:DA_QB], m[DA_QB:]) + jnp.log(l) * (1.0 / LN2)

    for g, dil in enumerate(DA_BRANCH_DIL):
        n_sub = SEQ // dil
        if n_sub == DA_QB:
            def body(r, carry, g=g, dil=dil):
                block(g, dil, r, r, DA_QB, 0)
                return carry
            lax.fori_loop(0, dil, body, 0, unroll=DA_UNROLL)
        else:
            nb = n_sub // DA_QB
            nk = 2 * DA_QB

            def body(j, carry, g=g, dil=dil, nb=nb, nk=nk, n_sub=n_sub):
                r = j // nb
                q0 = (j % nb) * DA_QB
                k0 = jnp.clip(q0 - DA_HALF, 0, n_sub - nk)
                block(g, dil, r + dil * q0, r + dil * k0, nk, (q0 - k0) // DA_HALF)
                return carry
            lax.fori_loop(0, dil * nb, body, 0, unroll=DA_UNROLL)

    lse_all = jnp.maximum(jnp.maximum(lse_s[0], lse_s[1]), lse_s[2])
    num = jnp.zeros((SEQ, LANES), F32)
    den = jnp.zeros((SEQ, LANES), F32)
    for g in range(len(DA_BRANCH_DIL)):
        w = jnp.exp2(lse_s[g] - lse_all)
        num = num + w * acc_s[g]
        den = den + w
    o_ref[0] = (num / den).astype(o_ref.dtype)


def dilated_attention(proj, mask_t):
    b = proj.shape[0]
    n_hp = DA_HEADS // 2
    blk = lambda off: pl.BlockSpec((1, SEQ, LANES), lambda i, hp, off=off: (i, 0, off + hp))
    nbr = len(DA_BRANCH_DIL)
    return pl.pallas_call(
        _da_kernel, grid=(b, n_hp),
        in_specs=[blk(0), blk(n_hp), blk(2 * n_hp),
                  pl.BlockSpec(mask_t.shape, lambda i, hp: (0, 0, 0))],
        out_specs=pl.BlockSpec((1, SEQ, LANES), lambda i, hp: (i, 0, hp)),
        out_shape=jax.ShapeDtypeStruct((b, SEQ, DA_W), BF16),
        scratch_shapes=[pltpu.VMEM((SEQ, LANES), F32)] * 3 + [pltpu.VMEM((nbr, SEQ, LANES), F32)] * 2,
        compiler_params=_cparams("parallel", "parallel"), name="dilated_attention",
    )(proj, proj, proj, mask_t)


ML_SCALE = ML_HEAD_DIM ** -0.5
GATE_I_FWD, GATE_I_BWD, GATE_F_FWD, GATE_F_BWD = 0, 4, 8, 12


def _exact_ones_matmul(ones_bf16, x):
    hi = x.astype(BF16)
    r1 = x - hi.astype(F32)
    mid = r1.astype(BF16)
    lo = (r1 - mid.astype(F32)).astype(BF16)
    dot = lambda t: jnp.dot(ones_bf16, t, preferred_element_type=F32)
    return dot(hi) + dot(mid) + dot(lo)


_NT = (((1,), (1,)), ((), ()))
ML_COMBOS = 2 * ML_HEADS
ML_HEADS_PER_TRIP = 4
ROW_CUM, ROW_RMAX, ROW_LOGW, ROW_TOT, ROW_LWMAX = range(5)


def _hi_lo_rows(row):
    hi = row.astype(BF16)
    lo = (row - hi.astype(F32)).astype(BF16)
    return jnp.concatenate([hi, lo, jnp.zeros((SUBLANES - 2, row.shape[1]), BF16)], axis=0)


def _mlstm_chunk(q, k, v_t, log_d_t, rows, state):
    c_t, n_vec, m_run = state
    cum, rmax, logw, tot, lwmax = rows
    log_inter = cum + m_run
    m_t = jnp.maximum(log_inter, rmax)
    w_inter = jnp.exp(log_inter - m_t)
    s_t = lax.dot_general(k, q, _NT, preferred_element_type=F32) * ML_SCALE * jnp.exp(log_d_t - m_t)
    num = w_inter * lax.dot_general(c_t.astype(BF16), q, _NT, preferred_element_type=F32)
    num = num + jnp.dot(v_t, s_t.astype(BF16), preferred_element_type=F32)
    nq = lax.dot_general(_hi_lo_rows(n_vec), q, _NT, preferred_element_type=F32)
    den = w_inter * (nq[0:1] + nq[1:2]) + jnp.sum(s_t, axis=0, keepdims=True)
    h_t = num / jnp.maximum(jnp.abs(den), jnp.exp(-m_t))
    m_new = jnp.maximum(tot + m_run, lwmax)
    w_row = jnp.exp(logw - m_new) * ML_SCALE
    decay = jnp.exp(tot + m_run - m_new)
    vw_t = (v_t.astype(F32) * w_row).astype(BF16)
    c_new = decay * c_t + jnp.dot(vw_t, k, preferred_element_type=F32)
    nk = jnp.dot(_hi_lo_rows(w_row), k, preferred_element_type=F32)
    n_new = decay * n_vec + nk[0:1] + nk[1:2]
    return h_t, (c_new, n_new, m_new)


def _mlstm_kernel(q_ref, k_ref, v_ref, og_ref, g_ref, gb_ref, ng_ref, out_ref, vt_s, ldt_s, rows_s, hf_s, hb_s):
    L = ML_CHUNK
    assert ML_HEAD_DIM == L == LANES
    ri = lax.broadcasted_iota(jnp.int32, (L, L), 0)
    ci = lax.broadcasted_iota(jnp.int32, (L, L), 1)
    lower = ci <= ri
    upper = ci >= ri
    lower_m = jnp.where(lower, 1.0, 0.0).astype(BF16)
    upper_m = jnp.where(upper, 1.0, 0.0).astype(BF16)
    lane = lax.broadcasted_iota(jnp.int32, (L, LANES), 1)
    fwd_rows = lax.broadcasted_iota(jnp.int32, (ML_COMBOS, L), 0) < ML_HEADS

    def prep(c, carry):
        rows = pl.ds(pl.multiple_of(c * L, L), L)
        vt_s[c] = v_ref[0, rows, :].T
        g = g_ref[0, rows, :] + gb_ref[...]
        lf = jnp.minimum(g, 0.0) - jnp.log1p(jnp.exp(-jnp.abs(g)))
        cum_f = _exact_ones_matmul(lower_m, lf)
        suf_b = _exact_ones_matmul(upper_m, lf)
        colv = jnp.where(lane < GATE_F_FWD, g, jnp.where(lane < GATE_F_BWD, cum_f, suf_b))
        rowv = colv.T
        i8 = rowv[0:ML_COMBOS]
        cum8 = rowv[ML_COMBOS:2 * ML_COMBOS]
        r_all = colv - pltpu.roll(colv, LANES - ML_COMBOS, 1)
        tot8 = jnp.where(fwd_rows, jnp.broadcast_to(cum8[:, L - 1:L], (ML_COMBOS, L)),
                         jnp.broadcast_to(cum8[:, 0:1], (ML_COMBOS, L)))
        logw8 = tot8 - cum8 + i8
        lwmax8 = jnp.broadcast_to(jnp.max(logw8, axis=-1, keepdims=True), (ML_COMBOS, L))
        rmax = []
        for j in range(ML_COMBOS):
            valid = upper if j < ML_HEADS else lower
            ldt = jnp.where(valid, jnp.broadcast_to(r_all[:, j:j + 1], (L, L)) + cum8[j:j + 1, :], NEG)
            ldt_s[c * ML_COMBOS + j] = ldt
            rmax.append(jnp.max(ldt, axis=0, keepdims=True))
        rows_s[c, ROW_CUM] = cum8
        rows_s[c, ROW_RMAX] = jnp.concatenate(rmax, axis=0)
        rows_s[c, ROW_LOGW] = logw8
        rows_s[c, ROW_TOT] = tot8
        rows_s[c, ROW_LWMAX] = lwmax8
        return carry

    lax.fori_loop(0, N_CHUNKS, prep, 0)

    zero_state = (jnp.zeros((ML_HEAD_DIM, ML_HEAD_DIM), F32), jnp.zeros((1, ML_HEAD_DIM), F32),
                  jnp.zeros((1, L), F32))
    def one(c, h, backward, state):
        hl = slice(ML_HEAD_DIM * h, ML_HEAD_DIM * (h + 1))
        j = ML_HEADS + h if backward else h
        rows = pl.ds(pl.multiple_of(c * L, L), L)
        prepared = tuple(rows_s[c, k, j:j + 1, :] for k in range(5))
        h_t, state = _mlstm_chunk(q_ref[0, rows, hl], k_ref[0, rows, hl], vt_s[c, hl, :],
                                  ldt_s[c * ML_COMBOS + j], prepared, state)
        (hb_s if backward else hf_s)[h, c] = h_t
        return state

    for h0 in range(0, ML_HEADS, ML_HEADS_PER_TRIP):
        def body(c, states, h0=h0):
            out = []
            for k in range(ML_HEADS_PER_TRIP):
                out.append(one(c, h0 + k, False, states[2 * k]))
                out.append(one(N_CHUNKS - 1 - c, h0 + k, True, states[2 * k + 1]))
            return tuple(out)

        lax.fori_loop(0, N_CHUNKS, body, (zero_state,) * (2 * ML_HEADS_PER_TRIP))

    for h in range(ML_HEADS):
        hl = slice(ML_HEAD_DIM * h, ML_HEAD_DIM * (h + 1))
        gain_t = jnp.broadcast_to(ng_ref[:, hl], (L, ML_HEAD_DIM)).T
        for c in range(N_CHUNKS):
            r0 = c * L
            hh = hf_s[h, c] + hb_s[h, c]
            mu = jnp.mean(hh, axis=0, keepdims=True)
            cen = hh - mu
            var = jnp.mean(cen * cen, axis=0, keepdims=True)
            hn = (cen * lax.rsqrt(var + LN_EPS) * gain_t).T
            og = og_ref[0, r0:r0 + L, hl].astype(F32)
            out_ref[0, r0:r0 + L, hl] = (hn / (1.0 + jnp.exp(-og))).astype(out_ref.dtype)


def mlstm_mixer(proj, gates, gate_bias, norm_g):
    b = proj.shape[0]
    c0 = (3 * NA_W) // ML_W
    blk = lambda off: pl.BlockSpec((1, SEQ, ML_W), lambda i, off=off: (i, 0, off))
    gb = jnp.pad(gate_bias.astype(F32), (0, LANES - gate_bias.shape[0])).reshape(1, LANES)
    return pl.pallas_call(
        _mlstm_kernel, grid=(b,),
        in_specs=[blk(c0), blk(c0 + 1), blk(c0 + 2), blk(c0 + 3),
                  pl.BlockSpec((1, SEQ, LANES), lambda i: (i, 0, 0)),
                  pl.BlockSpec((1, LANES), lambda i: (0, 0)),
                  pl.BlockSpec((1, ML_W), lambda i: (0, 0))],
        out_specs=pl.BlockSpec((1, SEQ, ML_W), lambda i: (i, 0, 0)),
        out_shape=jax.ShapeDtypeStruct((b, SEQ, ML_W), BF16),
        scratch_shapes=[pltpu.VMEM((N_CHUNKS, ML_W, ML_CHUNK), BF16),
                        pltpu.VMEM((N_CHUNKS * ML_COMBOS, ML_CHUNK, ML_CHUNK), F32),
                        pltpu.VMEM((N_CHUNKS, 5, ML_COMBOS, ML_CHUNK), F32),
                        pltpu.VMEM((ML_HEADS, N_CHUNKS, ML_HEAD_DIM, ML_CHUNK), F32),
                        pltpu.VMEM((ML_HEADS, N_CHUNKS, ML_HEAD_DIM, ML_CHUNK), F32)],
        compiler_params=_cparams("parallel"), name="mlstm_mixer",
    )(proj, proj, proj, proj, gates, gb, norm_g.reshape(1, ML_W).astype(F32))


def _router_kernel(x_ref, w_ref, aff_ref):
    x = x_ref[...]
    w = w_ref[...]
    x_hi = x.astype(BF16)
    x_lo = (x - x_hi.astype(F32)).astype(BF16)
    w_hi = w.astype(BF16)
    w_lo = (w - w_hi.astype(F32)).astype(BF16)
    dot = lambda a, b: jnp.dot(a, b, preferred_element_type=F32)
    logits_t = dot(x_hi, w_hi) + dot(x_hi, w_lo) + dot(x_lo, w_hi)
    logits = logits_t.T[0:N_EXPERTS, :]
    z = jnp.exp(logits - jnp.max(logits, axis=0, keepdims=True))
    aff_ref[...] = z / jnp.sum(z, axis=0, keepdims=True)


def router_affinities(x, w_router, tm=1024):
    n, d = x.shape
    w_pad = jnp.pad(w_router.astype(F32), ((0, 0), (0, LANES - N_EXPERTS)))
    return pl.pallas_call(
        _router_kernel, grid=(n // tm,),
        in_specs=[pl.BlockSpec((tm, d), lambda i: (i, 0)), pl.BlockSpec((d, LANES), lambda i: (0, 0))],
        out_specs=pl.BlockSpec((N_EXPERTS, tm), lambda i: (0, i)),
        out_shape=jax.ShapeDtypeStruct((N_EXPERTS, n), F32),
        compiler_params=_cparams("parallel"), name="router_affinities")(x, w_pad)


def _tri_matrices(r):
    li = lax.broadcasted_iota(jnp.int32, (LANES, LANES), 0)
    lj = lax.broadcasted_iota(jnp.int32, (LANES, LANES), 1)
    tri_u = jnp.where(li <= lj, 1.0, 0.0).astype(BF16)
    ri = lax.broadcasted_iota(jnp.int32, (r, r), 0)
    rj = lax.broadcasted_iota(jnp.int32, (r, r), 1)
    tri_l = jnp.where(rj < ri, 1.0, 0.0).astype(BF16)
    return tri_u, tri_l


def _prefix_counts(mask, tri_u, tri_l):
    r = mask.shape[0]
    within = jnp.dot(mask.astype(BF16), tri_u, preferred_element_type=F32)
    rowtot = within[:, LANES - 1:LANES]
    hi = jnp.floor(rowtot * (1.0 / 16.0))
    lo = rowtot - 16.0 * hi
    hi_b = jnp.broadcast_to(hi, (r, LANES)).astype(BF16)
    lo_b = jnp.broadcast_to(lo, (r, LANES)).astype(BF16)
    rowoff = 16.0 * jnp.dot(tri_l, hi_b, preferred_element_type=F32) + jnp.dot(
        tri_l, lo_b, preferred_element_type=F32)
    return within - mask + rowoff, within, rowoff, rowtot


def _select_kernel(aff_ref, sel_ref, *, cap):
    r = aff_ref.shape[1]
    tri_u, tri_l = _tri_matrices(r)
    bits = pltpu.bitcast(aff_ref[0], jnp.int32)

    def count(m):
        c = jnp.sum(jnp.where(m, 1.0, 0.0), axis=1, keepdims=True)
        return jnp.sum(c, axis=0, keepdims=True)

    def bisect(i, prefix):
        cand = prefix | jnp.left_shift(jnp.int32(1), 30 - i)
        return jnp.where(count(bits >= cand) >= cap, cand, prefix)

    thr = lax.fori_loop(0, 31, bisect, jnp.zeros((1, 1), jnp.int32))
    gt = bits > thr
    eq = bits == thr
    need = cap - count(gt)
    rank_eq, _, _, _ = _prefix_counts(jnp.where(eq, 1.0, 0.0), tri_u, tri_l)
    sel_ref[0] = jnp.where(gt | (eq & (rank_eq < need)), 1.0, 0.0)


def select_tokens(aff3, cap):
    e, r, _ = aff3.shape
    blk = pl.BlockSpec((1, r, LANES), lambda i: (i, 0, 0))
    return pl.pallas_call(
        functools.partial(_select_kernel, cap=cap), grid=(e,), in_specs=[blk], out_specs=blk,
        out_shape=jax.ShapeDtypeStruct((e, r, LANES), F32),
        compiler_params=_cparams("parallel"), name="select_tokens")(aff3)


def _lists_kernel(sel_ref, aff_ref, idx_ref, dst_ref, gate_ref, ts_ref, ts_s, er_s, *, cap, st):
    e = pl.program_id(0)
    r = sel_ref.shape[1]
    tri_u, tri_l = _tri_matrices(r)

    @pl.when(e == 0)
    def _():
        cnt = sel_ref[0]
        for k in range(1, N_EXPERTS):
            cnt = cnt + sel_ref[k]
        ts, _, _, _ = _prefix_counts(cnt, tri_u, tri_l)
        ts_s[...] = ts
        ts_ref[...] = ts
        er_s[...] = jnp.zeros_like(er_s)

    sel = sel_ref[e]
    _, within, rowoff, rowtot = _prefix_counts(sel, tri_u, tri_l)
    dst = ts_s[...] + er_s[...]
    er_s[...] = er_s[...] + sel

    d2 = jnp.floor(dst * (1.0 / 65536.0))
    rem = dst - 65536.0 * d2
    d1 = jnp.floor(rem * (1.0 / 256.0))
    d0 = rem - 256.0 * d1
    aff = aff_ref[0]
    a_hi = aff.astype(BF16)
    a_r1 = aff - a_hi.astype(F32)
    a_mid = a_r1.astype(BF16)
    a_lo = (a_r1 - a_mid.astype(F32)).astype(BF16)
    rhs = jnp.concatenate([within.astype(BF16), d0.astype(BF16), d1.astype(BF16), d2.astype(BF16),
                           a_hi, a_mid, a_lo], axis=1)
    rowoff_row = rowoff.T[0:1, :]
    rowend_row = rowoff_row + jnp.broadcast_to(rowtot, (r, LANES)).T[0:1, :]
    rho_row = lax.broadcasted_iota(jnp.int32, (1, r), 1).astype(F32)
    lane = lax.broadcasted_iota(jnp.int32, (st, LANES), 1).astype(F32)
    eye = lax.broadcasted_iota(jnp.int32, (LANES, LANES), 0) == lax.broadcasted_iota(
        jnp.int32, (LANES, LANES), 1)

    for t in range(cap // st):
        s_col = (t * st + lax.broadcasted_iota(jnp.int32, (st, 1), 0)).astype(F32)
        in_row = (rowoff_row <= s_col) & (s_col < rowend_row)
        got = jnp.dot(jnp.where(in_row, 1.0, 0.0).astype(BF16), rhs, preferred_element_type=F32)
        base = jnp.sum(jnp.where(in_row, rowoff_row, 0.0), axis=1, keepdims=True)
        rho = jnp.sum(jnp.where(in_row, rho_row, 0.0), axis=1, keepdims=True)
        local = s_col - base
        lam = jnp.sum(jnp.where(got[:, 0:LANES] <= local, 1.0, 0.0), axis=1, keepdims=True)
        plane = lambda k: got[:, k * LANES:(k + 1) * LANES]
        pair = plane(1) + 256.0 * plane(2) + 65536.0 * plane(3)
        at_lam = lane == lam
        dval = jnp.sum(jnp.where(at_lam, pair, 0.0), axis=1, keepdims=True)
        gate_ref[0, t * st:(t + 1) * st, :] = jnp.sum(
            jnp.where(at_lam, plane(4) + plane(5) + plane(6), 0.0), axis=1, keepdims=True)
        ival = rho * float(LANES) + lam
        for j in range(st // LANES):
            seg = slice(j * LANES, (j + 1) * LANES)
            row = t * (st // LANES) + j
            idx_ref[0, row:row + 1, :] = jnp.sum(
                jnp.where(eye, ival[seg], 0.0), axis=0, keepdims=True).astype(jnp.int32)
            dst_ref[0, row:row + 1, :] = jnp.sum(
                jnp.where(eye, dval[seg], 0.0), axis=0, keepdims=True).astype(jnp.int32)


def build_lists(sel3, aff3, cap):
    e, r, _ = sel3.shape
    st = min(512, cap)
    lst = pl.BlockSpec((1, cap // LANES, LANES), lambda i: (i, 0, 0))
    return pl.pallas_call(
        functools.partial(_lists_kernel, cap=cap, st=st), grid=(e,),
        in_specs=[pl.BlockSpec((e, r, LANES), lambda i: (0, 0, 0)),
                  pl.BlockSpec((1, r, LANES), lambda i: (i, 0, 0))],
        out_specs=[lst, lst, pl.BlockSpec((1, cap, 1), lambda i: (i, 0, 0)),
                   pl.BlockSpec((r, LANES), lambda i: (0, 0))],
        out_shape=[jax.ShapeDtypeStruct((e, cap // LANES, LANES), jnp.int32),
                   jax.ShapeDtypeStruct((e, cap // LANES, LANES), jnp.int32),
                   jax.ShapeDtypeStruct((e, cap, 1), F32), jax.ShapeDtypeStruct((r, LANES), F32)],
        scratch_shapes=[pltpu.VMEM((r, LANES), F32), pltpu.VMEM((r, LANES), F32)],
        compiler_params=_cparams("arbitrary"), name="build_lists")(sel3, aff3)


FFN_CHUNKS = tuple((f, min(f + 256, D_FF)) for f in range(0, D_FF, 256))


def _ffn_kernel(idx_first, idx_next_a, idx_next_b, dst_prev_a, dst_prev_b, dst_last,
                x_hbm, gate_a, gate_b, wg_ref, wu_ref, wd_ref, z_hbm,
                xbuf0, xbuf1, ybuf0, ybuf1, acc, gsem, ssem, *, tm, n_grid):
    g = pl.program_id(0)

    def token_tile(t):
        if isinstance(t, int):
            return pl.ds(t * TILE_ROWS, TILE_ROWS)
        return pl.ds(pl.multiple_of(t * TILE_ROWS, TILE_ROWS), TILE_ROWS)

    def gather_row(ids, i, buf, sem):
        pltpu.make_async_copy(x_hbm.at[token_tile(ids[0, 0, i]), :], buf.at[token_tile(i), :],
                              sem).start(priority=1)

    def scatter_row(dsts, i, buf, sem):
        pltpu.make_async_copy(buf.at[token_tile(i), :], z_hbm.at[token_tile(dsts[0, 0, i]), :],
                              sem).start(priority=0)

    def wait_rows(buf, sem):
        pltpu.make_async_copy(x_hbm.at[pl.ds(0, tm * TILE_ROWS), :], buf, sem).wait()

    @pl.when(g == 0)
    def _():
        ybuf1[...] = jnp.zeros_like(ybuf1)

        def first(i, carry):
            gather_row(idx_first, i, xbuf0, gsem.at[0])
            return carry
        lax.fori_loop(0, tm, first, 0, unroll=8)

    def tile(xcur, gcur, xnext, gnext, idx_next, yprev, sprev, dst_prev):
        for i in range(tm):
            gather_row(idx_next, i, xnext, gnext)
            scatter_row(dst_prev, i, yprev, sprev)

        wait_rows(xcur, gcur)
        xb = _load_token_tiles(xcur, tm).astype(BF16)
        for c, (f0, f1) in enumerate(FFN_CHUNKS):
            gate = jnp.dot(xb, wg_ref[0, :, f0:f1], preferred_element_type=F32)
            up = jnp.dot(xb, wu_ref[0, :, f0:f1], preferred_element_type=F32)
            h = (gate / (1.0 + jnp.exp(-gate)) * up).astype(BF16)
            part = jnp.dot(h, wd_ref[0, f0:f1, :], preferred_element_type=F32)
            if c == 0:
                acc[...] = part
            else:
                acc[...] += part

    tile(xbuf0, gsem.at[0], xbuf1, gsem.at[1], idx_next_a, ybuf1, ssem.at[1], dst_prev_a)

    @pl.when(g >= 1)
    def _():
        wait_rows(ybuf0, ssem.at[0])
    _store_token_tiles(ybuf0, acc[...] * gate_a[0])

    tile(xbuf1, gsem.at[1], xbuf0, gsem.at[0], idx_next_b, ybuf0, ssem.at[0], dst_prev_b)
    wait_rows(ybuf1, ssem.at[1])
    _store_token_tiles(ybuf1, acc[...] * gate_b[0])

    @pl.when(g == n_grid - 1)
    def _():
        def last(i, carry):
            scatter_row(dst_last, i, ybuf1, ssem.at[1])
            return carry
        lax.fori_loop(0, tm, last, 0, unroll=8)
        wait_rows(ybuf1, ssem.at[1])
        wait_rows(ybuf0, ssem.at[0])
        wait_rows(xbuf0, gsem.at[0])


def expert_ffn(x_tiles, idx, dst, gate, w_gate, w_up, w_down, tm=512):
    d = D_MODEL
    e, cap = idx.shape
    tm = min(tm, cap // 2)
    nt = cap // tm
    assert nt % 2 == 0
    n_tiles = e * nt
    n_grid = n_tiles // 2
    idx3 = idx.reshape(n_tiles, 1, tm)
    gate3 = gate.reshape(n_tiles, tm, 1)
    spare =(e * cap + jnp.arange(tm, dtype=jnp.int32)).reshape(1, 1, tm)
    dst3 = jnp.concatenate([spare, dst.reshape(n_tiles, 1, tm)])
    smem = lambda imap: pl.BlockSpec((1, 1, tm), imap, memory_space=pltpu.SMEM)
    wspec = lambda w: pl.BlockSpec((1,) + w.shape[1:], lambda i: ((2 * i) // nt, 0, 0))
    return pl.pallas_call(
        functools.partial(_ffn_kernel, tm=tm, n_grid=n_grid), grid=(n_grid,),
        in_specs=[smem(lambda i: (0, 0, 0)), smem(lambda i: (2 * i + 1, 0, 0)),
                  smem(lambda i: (jnp.minimum(2 * i + 2, n_tiles - 1), 0, 0)),
                  smem(lambda i: (2 * i, 0, 0)), smem(lambda i: (2 * i + 1, 0, 0)),
                  smem(lambda i: (n_tiles, 0, 0)),
                  pl.BlockSpec(memory_space=pl.ANY),
                  pl.BlockSpec((1, tm, 1), lambda i: (2 * i, 0, 0)),
                  pl.BlockSpec((1, tm, 1), lambda i: (2 * i + 1, 0, 0)),
                  wspec(w_gate), wspec(w_up), wspec(w_down)],
        out_specs=pl.BlockSpec(memory_space=pl.ANY),
        out_shape=jax.ShapeDtypeStruct(((e * cap + tm) * TILE_ROWS, LANES), F32),
        scratch_shapes=[pltpu.VMEM((tm * TILE_ROWS, LANES), F32)] * 4 + [pltpu.VMEM((tm, d), F32)]
        + [pltpu.SemaphoreType.DMA((2,)), pltpu.SemaphoreType.DMA((2,))],
        compiler_params=_cparams("arbitrary"), name="expert_ffn",
    )(idx3, idx3, idx3, dst3, dst3, dst3, x_tiles, gate3, gate3, w_gate, w_up, w_down)


COMBINE_ZB = 256
COMBINE_SLOTS = 4


def _combine_kernel(ts_ref, x_ref, run_ref, g_ref, b_ref, z_hbm, o_ref, zbuf, sem, used, *, z_rows, n_tiles):
    i = pl.program_id(0)
    tt = x_ref.shape[0]

    def chunk_rows(t, c):
        lo = ts_ref[t] + c * COMBINE_ZB
        return lo, jnp.minimum(lo, z_rows - COMBINE_ZB)

    def chunk_tiles(start):
        return pl.ds(pl.multiple_of(start * TILE_ROWS, TILE_ROWS), COMBINE_ZB * TILE_ROWS)

    def chunks_of(t):
        return jnp.maximum((ts_ref[t + 1] - ts_ref[t] + COMBINE_ZB - 1) // COMBINE_ZB, 1)

    def fetch_next():
        t = used[2]

        @pl.when(t < n_tiles)
        def _():
            c = used[3]
            slot = used[1] % COMBINE_SLOTS
            _, start = chunk_rows(t, c)
            pltpu.make_async_copy(z_hbm.at[chunk_tiles(start), :], zbuf.at[slot], sem.at[slot]).start()
            used[1] = used[1] + 1
            last = c + 1 >= chunks_of(t)
            used[2] = jnp.where(last, t + 1, t)
            used[3] = jnp.where(last, 0, c + 1)

    @pl.when(i == 0)
    def _():
        for k in range(4):
            used[k] = 0
        for _ in range(COMBINE_SLOTS - 1):
            fetch_next()

    base = used[0]
    n_chunks = chunks_of(i)
    run_lo = jnp.broadcast_to(run_ref[:, 0:1], (tt, COMBINE_ZB))
    run_hi = jnp.broadcast_to(run_ref[:, 1:2], (tt, COMBINE_ZB))
    col = lax.broadcasted_iota(jnp.int32, (1, COMBINE_ZB), 1).astype(F32)

    def chunk(c, acc):
        slot = (base + c) % COMBINE_SLOTS
        lo, start = chunk_rows(i, c)
        pair = col + start.astype(F32)
        pair = jnp.where(pair >= lo.astype(F32), pair, -1.0)
        a = jnp.where((run_lo <= pair) & (pair < run_hi), 1.0, 0.0).astype(BF16)
        pltpu.make_async_copy(z_hbm.at[pl.ds(0, COMBINE_ZB * TILE_ROWS), :], zbuf.at[slot],
                              sem.at[slot]).wait()
        rows = _load_token_tiles(zbuf.at[slot], COMBINE_ZB).astype(BF16)
        acc = acc + jnp.dot(a, rows, preferred_element_type=F32)
        fetch_next()
        return acc

    ffn = lax.fori_loop(0, n_chunks, chunk, jnp.zeros((tt, D_MODEL), F32))
    used[0] = base + n_chunks
    o_ref[...] = _layer_norm_rows(DN_ALPHA * x_ref[...] + ffn, g_ref[...], b_ref[...])


def combine_ln(x, z, tile_start, runs, g, b, tt=256):
    n, d = x.shape
    n_tiles = n // tt
    row = lambda i, ts: (i, 0)
    fixed = lambda i, ts: (0, 0)
    grid_spec = pltpu.PrefetchScalarGridSpec(
        num_scalar_prefetch=1, grid=(n_tiles,),
        in_specs=[pl.BlockSpec((tt, d), row), pl.BlockSpec((tt, 2), row), pl.BlockSpec((1, d), fixed),
                  pl.BlockSpec((1, d), fixed), pl.BlockSpec(memory_space=pl.ANY)],
        out_specs=pl.BlockSpec((tt, d), row),
        scratch_shapes=[pltpu.VMEM((COMBINE_SLOTS, COMBINE_ZB * TILE_ROWS, LANES), F32),
                        pltpu.SemaphoreType.DMA((COMBINE_SLOTS,)), pltpu.SMEM((4,), jnp.int32)])
    return pl.pallas_call(
        functools.partial(_combine_kernel, z_rows=z.shape[0] // TILE_ROWS, n_tiles=n_tiles), grid_spec=grid_spec,
        out_shape=jax.ShapeDtypeStruct((n, d), F32),
        compiler_params=_cparams("arbitrary"), name="combine_ln",
    )(tile_start, x, runs, g.reshape(1, d), b.reshape(1, d), z)


def moe_layer(x, x_tiles, w_router, w_gate, w_up, w_down, g, b, tt=256):
    n, _ = x.shape
    r = n // LANES
    cap = 2 * n // N_EXPERTS
    aff3 = router_affinities(x, w_router).reshape(N_EXPERTS, r, LANES)
    sel3 = select_tokens(aff3, cap)
    idx, dst, gate, ts = build_lists(sel3, aff3, cap)
    z = expert_ffn(x_tiles, idx.reshape(N_EXPERTS, cap), dst.reshape(N_EXPERTS, cap), gate, w_gate, w_up, w_down)
    ts_ext = jnp.concatenate([ts.reshape(n), jnp.full((1,), N_EXPERTS * cap, F32)])
    tile_start = ts_ext[::tt].astype(jnp.int32)
    runs = jnp.stack([ts_ext[:-1], ts_ext[1:]], axis=1)
    return combine_ln(x, z, tile_start, runs, g, b, tt=tt)


def kernel(x_prompt, x_sample, even_w_in, ml_gate_bias, na_rpb, ml_norm_g, even_w_out, da_w_in, da_w_out,
           ln_mix_g, ln_mix_b, ec_router, ec_w_gate, ec_w_up, ec_w_down, ln_ffn_g, ln_ffn_b):
    w_even = even_w_in[0][:, :3584].astype(BF16)
    w_gates = jnp.pad(even_w_in[0][:, 3584:], ((0, 0), (0, LANES - 16))).astype(BF16)
    w_out_a = even_w_out[0][:NA_W].astype(BF16)
    w_out_b = even_w_out[0][NA_W:].astype(BF16)
    w_odd = da_w_in[0].astype(BF16)
    w_odd_out = da_w_out[0].astype(BF16)
    tbl = na_bias_table(na_rpb[0])
    rope_t = rope_tables()
    mask_t = da_mask_table()
    moe_w = [(ec_router[l], ec_w_gate[l].astype(BF16), ec_w_up[l].astype(BF16), ec_w_down[l].astype(BF16),
              ln_ffn_g[l], ln_ffn_b[l]) for l in range(DEPTH)]

    def trunk(x):
        b = x.shape[0]
        xt = x.reshape(b * SEQ, D_MODEL)
        proj, gates = in_projection(xt, w_even, w_gates)
        proj = proj.reshape(b, SEQ, -1)
        ya = neighbourhood_attention(proj, tbl)
        yb = mlstm_mixer(proj, gates.reshape(b, SEQ, LANES), ml_gate_bias[0], ml_norm_g[0])
        xt, xt_tiles = out_projection_ln(xt, [ya.reshape(b * SEQ, NA_W), yb.reshape(b * SEQ, ML_W)],
                                         [w_out_a, w_out_b], ln_mix_g[0], ln_mix_b[0])
        xt = moe_layer(xt, xt_tiles, *moe_w[0])
        proj = in_projection_rope(xt, w_odd, rope_t, DA_W).reshape(b, SEQ, -1)
        yc = dilated_attention(proj, mask_t)
        xt, xt_tiles = out_projection_ln(xt, [yc.reshape(b * SEQ, DA_W)], [w_odd_out],
                                         ln_mix_g[1], ln_mix_b[1])
        xt = moe_layer(xt, xt_tiles, *moe_w[1])
        return xt.reshape(b, SEQ, D_MODEL)

    return trunk(x_prompt), trunk(x_sample)
```

```python
import functools

import numpy as np
import jax
import jax.numpy as jnp
from jax import lax
from jax.experimental import pallas as pl
from jax.experimental.pallas import tpu as pltpu

F32 = jnp.float32
BF16 = jnp.bfloat16

D_MODEL = 1024
SEQ = 2048
GRID_W = 64
GRID_ROWS = SEQ // GRID_W
NA_HEADS = 8
NA_W = 512
NA_WIN_R = 8
NA_WIN_C = 16
ML_HEADS = 4
ML_HEAD_DIM = 128
ML_W = 512
ML_CHUNK = 128
N_CHUNKS = SEQ // ML_CHUNK
DA_HEADS = 16
DA_W = 1024
DA_HALF = 64
ROPE_THETA = 10000.0
N_EXPERTS = 16
D_FF = 1408
LN_EPS = 1e-5
DEPTH = 2
DN_ALPHA = (2 * DEPTH) ** 0.25
LANES = 128
NEG = -1e30
LOG2E = 1.4426950408889634
LN2 = 0.6931471805599453
VMEM_LIMIT = 56 * 1024 * 1024


def _cparams(*sem):
    return pltpu.CompilerParams(dimension_semantics=sem, vmem_limit_bytes=VMEM_LIMIT)


def _inproj_kernel(x_ref, w_ref, o_ref, *, n_chunk):
    xb = x_ref[...].astype(BF16)
    for c in range(0, w_ref.shape[1], n_chunk):
        o_ref[:, c:c + n_chunk] = jnp.dot(
            xb, w_ref[:, c:c + n_chunk], preferred_element_type=F32).astype(o_ref.dtype)


def _inproj_gates_kernel(x_ref, w_ref, wg_ref, o_ref, g_ref, *, n_chunk):
    xb = x_ref[...].astype(BF16)
    for c in range(0, w_ref.shape[1], n_chunk):
        o_ref[:, c:c + n_chunk] = jnp.dot(
            xb, w_ref[:, c:c + n_chunk], preferred_element_type=F32).astype(o_ref.dtype)
    g_ref[...] = jnp.dot(xb, wg_ref[...], preferred_element_type=F32)


def _rotate_half_pairs(x, cos, sin_signed):
    first_half = lax.broadcasted_iota(jnp.int32, x.shape, 1) % 64 < 32
    swapped = jnp.where(first_half, pltpu.roll(x, 96, 1), pltpu.roll(x, 32, 1))
    return x * cos + swapped * sin_signed


def _inproj_rope_kernel(x_ref, w_ref, rope_ref, o_ref, *, n_chunk, width):
    xb = x_ref[...].astype(BF16)
    for c in range(0, w_ref.shape[1], n_chunk):
        res = jnp.dot(xb, w_ref[:, c:c + n_chunk], preferred_element_type=F32)
        if c < 2 * width:
            t = 0 if c < width else 2
            res = jnp.concatenate(
                [_rotate_half_pairs(res[:, l:l + LANES], rope_ref[t], rope_ref[t + 1])
                 for l in range(0, n_chunk, LANES)], axis=1)
        o_ref[:, c:c + n_chunk] = res.astype(o_ref.dtype)


def in_projection_rope(x, w, rope_tables, width, tm=512, n_chunk=512):
    m, k = x.shape
    n = w.shape[1]
    assert width % n_chunk == 0 and SEQ % tm == 0
    return pl.pallas_call(
        functools.partial(_inproj_rope_kernel, n_chunk=n_chunk, width=width),
        grid=(m // tm,),
        in_specs=[pl.BlockSpec((tm, k), lambda i: (i, 0)), pl.BlockSpec((k, n), lambda i: (0, 0)),
                  pl.BlockSpec((4, tm, LANES), lambda i: (0, i % (SEQ // tm), 0))],
        out_specs=pl.BlockSpec((tm, n), lambda i: (i, 0)),
        out_shape=jax.ShapeDtypeStruct((m, n), BF16),
        compiler_params=_cparams("parallel"), name="in_projection_rope")(x, w, rope_tables)


def in_projection(x, w, wg=None, tm=512, n_chunk=512):
    m, k = x.shape
    n = w.shape[1]
    x_spec = pl.BlockSpec((tm, k), lambda i: (i, 0))
    w_spec = pl.BlockSpec((k, n), lambda i: (0, 0))
    o_spec = pl.BlockSpec((tm, n), lambda i: (i, 0))
    if wg is None:
        return pl.pallas_call(
            functools.partial(_inproj_kernel, n_chunk=n_chunk),
            grid=(m // tm,), in_specs=[x_spec, w_spec], out_specs=o_spec,
            out_shape=jax.ShapeDtypeStruct((m, n), BF16),
            compiler_params=_cparams("parallel"), name="in_projection")(x, w)
    return pl.pallas_call(
        functools.partial(_inproj_gates_kernel, n_chunk=n_chunk),
        grid=(m // tm,),
        in_specs=[x_spec, w_spec, pl.BlockSpec((k, LANES), lambda i: (0, 0))],
        out_specs=[o_spec, pl.BlockSpec((tm, LANES), lambda i: (i, 0))],
        out_shape=[jax.ShapeDtypeStruct((m, n), BF16), jax.ShapeDtypeStruct((m, LANES), F32)],
        compiler_params=_cparams("parallel"), name="in_projection_gates")(x, w, wg)


def _layer_norm_rows(acc, g, b):
    mu = jnp.mean(acc, axis=-1, keepdims=True)
    cen = acc - mu
    var = jnp.mean(cen * cen, axis=-1, keepdims=True)
    return cen * lax.rsqrt(var + LN_EPS) * g + b


def _outproj_ln_kernel(*refs, n_mix):
    x_ref = refs[0]
    mix_refs = refs[1:1 + n_mix]
    w_refs = refs[1 + n_mix:1 + 2 * n_mix]
    g_ref, b_ref, o_ref, ot_ref = refs[1 + 2 * n_mix:]
    acc = DN_ALPHA * x_ref[...]
    for m_ref, w_ref in zip(mix_refs, w_refs):
        acc = acc + jnp.dot(m_ref[...], w_ref[...], preferred_element_type=F32)
    res = _layer_norm_rows(acc, g_ref[...], b_ref[...])
    o_ref[...] = res
    _store_token_tiles(ot_ref, res)


SUBLANES = 8
TILE_ROWS = D_MODEL // LANES


def _store_token_tiles(ref, rows):
    n = rows.shape[0]
    for j in range(TILE_ROWS):
        ref[pl.ds(j, n, stride=TILE_ROWS), :] = rows[:, j * LANES:(j + 1) * LANES]


def _load_token_tiles(ref, n, first_row=0):
    return jnp.concatenate(
        [ref[pl.ds(first_row + j, n, stride=TILE_ROWS), :] for j in range(TILE_ROWS)], axis=1)


def out_projection_ln(x, mixes, ws, g, b, tm=512):
    m, d = x.shape
    n_mix = len(mixes)
    row = lambda i: (i, 0)
    fixed = lambda i: (0, 0)
    in_specs = [pl.BlockSpec((tm, d), row)]
    in_specs += [pl.BlockSpec((tm, mx.shape[1]), row) for mx in mixes]
    in_specs += [pl.BlockSpec(w.shape, fixed) for w in ws]
    in_specs += [pl.BlockSpec((1, d), fixed), pl.BlockSpec((1, d), fixed)]
    return pl.pallas_call(
        functools.partial(_outproj_ln_kernel, n_mix=n_mix),
        grid=(m // tm,), in_specs=in_specs,
        out_specs=[pl.BlockSpec((tm, d), row), pl.BlockSpec((tm * TILE_ROWS, LANES), row)],
        out_shape=[jax.ShapeDtypeStruct((m, d), F32), jax.ShapeDtypeStruct((m * TILE_ROWS, LANES), F32)],
        compiler_params=_cparams("parallel"), name="out_projection_ln",
    )(x, *mixes, *ws, g.reshape(1, d), b.reshape(1, d))


NA_PAIR_ROWS = 10
NA_PAIR_KEYS = NA_PAIR_ROWS * GRID_W
NA_PAIR_CASES = 5
NA_N_PAIRS = GRID_ROWS // 2


def _na_pair_window(p):
    start = jnp.minimum(jnp.clip(2 * p - NA_WIN_R // 2, 0, GRID_ROWS - NA_WIN_R), GRID_ROWS - NA_PAIR_ROWS)
    case = jnp.minimum(p, 2) + jnp.maximum(p - (NA_N_PAIRS - 3), 0)
    return start, case


def na_bias_table(rpb):
    j = np.arange(GRID_W)
    kc = np.arange(GRID_W)
    win_c0 = np.clip(j - NA_WIN_C // 2, 0, GRID_W - NA_WIN_C)
    valid = (kc[None, :] >= win_c0[:, None]) & (kc[None, :] < win_c0[:, None] + NA_WIN_C)
    dc = np.clip(kc[None, :] - j[:, None] + NA_WIN_C - 1, 0, 2 * NA_WIN_C - 2)
    n_dc = 2 * NA_WIN_C - 1
    pick = jnp.asarray(dc[:, :, None] == np.arange(n_dc)[None, None, :], F32)
    rows = jnp.einsum('hrd,jkd->hrjk', rpb.astype(F32), pick, precision=lax.Precision.HIGHEST)
    rows = jnp.where(valid[None, None], rows * LOG2E, NEG)
    masked = jnp.full_like(rows[:, 0], NEG)
    representative = (0, 1, 2, NA_N_PAIRS - 2, NA_N_PAIRS - 1)
    cases = []
    for p in representative:
        start = min(max(2 * p - NA_WIN_R // 2, 0), GRID_ROWS - NA_WIN_R, GRID_ROWS - NA_PAIR_ROWS)
        per_row = []
        for r in (2 * p, 2 * p + 1):
            rs = min(max(r - NA_WIN_R // 2, 0), GRID_ROWS - NA_WIN_R)
            per_row.append(jnp.stack(
                [rows[:, start + w - r + NA_WIN_R - 1] if rs <= start + w < rs + NA_WIN_R else masked
                 for w in range(NA_PAIR_ROWS)], axis=1))
        cases.append(jnp.stack(per_row, axis=1))
    t = jnp.stack(cases, axis=1)
    t = t.transpose(0, 1, 2, 4, 3, 5).reshape(NA_HEADS // 2, 2, NA_PAIR_CASES, 2 * GRID_W, NA_PAIR_KEYS)
    return t.transpose(0, 2, 1, 3, 4).reshape(NA_HEADS // 2, NA_PAIR_CASES, 4 * GRID_W, NA_PAIR_KEYS)


NA_SCORE_SCALE = 64 ** -0.5 * LOG2E
NA_UNROLL = 8


def _na_kernel(q_ref, k_ref, v_ref, tbl_ref, o_ref, v2a, v2b):
    nq = 2 * GRID_W
    head0 = lax.broadcasted_iota(jnp.int32, (nq, LANES), 1) < 64

    in0 = lax.broadcasted_iota(jnp.int32, (SEQ, LANES), 1) < 64
    vf = v_ref[0].astype(F32)
    v2a[...] = jnp.concatenate([jnp.where(in0, vf, 0.0), jnp.where(in0, 1.0, 0.0)], axis=1).astype(BF16)
    v2b[...] = jnp.concatenate([jnp.where(in0, 0.0, vf), jnp.where(in0, 0.0, 1.0)], axis=1).astype(BF16)

    def pair(p, carry):
        start, case = _na_pair_window(p)
        rows = pl.ds(pl.multiple_of(p * nq, nq), nq)
        q = q_ref[0, rows, :]
        zero = jnp.zeros_like(q)
        q2 = jnp.concatenate([jnp.where(head0, q, zero), jnp.where(head0, zero, q)], axis=0)
        win = pl.ds(pl.multiple_of(start * GRID_W, GRID_W), NA_PAIR_KEYS)
        s = lax.dot_general(q2, k_ref[0, win, :], (((1,), (1,)), ((), ())), preferred_element_type=F32)
        s = s * NA_SCORE_SCALE + tbl_ref[0, case]
        pr = jnp.exp2(s - jnp.max(s, axis=-1, keepdims=True)).astype(BF16)
        p2 = jnp.concatenate([pr[:nq], pr[nq:]], axis=1)
        ol = jnp.dot(p2, jnp.concatenate([v2a[win, :], v2b[win, :]], axis=0), preferred_element_type=F32)
        o_ref[0, rows, :] = (ol[:, :LANES] / ol[:, LANES:]).astype(o_ref.dtype)
        return carry

    lax.fori_loop(0, NA_N_PAIRS, pair, 0, unroll=NA_UNROLL)


def neighbourhood_attention(proj, tbl):
    b = proj.shape[0]
    n_hp = NA_HEADS // 2
    blk = lambda off: pl.BlockSpec((1, SEQ, LANES), lambda hp, i, off=off: (i, 0, off + hp))
    return pl.pallas_call(
        _na_kernel, grid=(n_hp, b),
        in_specs=[blk(0), blk(n_hp), blk(2 * n_hp),
                  pl.BlockSpec((1, NA_PAIR_CASES, 4 * GRID_W, NA_PAIR_KEYS), lambda hp, i: (hp, 0, 0, 0))],
        out_specs=pl.BlockSpec((1, SEQ, LANES), lambda hp, i: (i, 0, hp)),
        out_shape=jax.ShapeDtypeStruct((b, SEQ, NA_W), BF16),
        scratch_shapes=[pltpu.VMEM((SEQ, 2 * LANES), BF16)] * 2,
        compiler_params=_cparams("parallel", "parallel"), name="neighbourhood_attention",
    )(proj, proj, proj, tbl)


DA_BRANCH_DIL = (1, 4, 16)
DA_QB = 128


DA_Q_SCALE = 64 ** -0.5 * LOG2E


def rope_tables():
    lane = np.arange(LANES) % 64
    inv = ROPE_THETA ** (-(2.0 * (lane % 32)) / 64.0)
    ang = jnp.arange(SEQ, dtype=F32)[:, None] * jnp.asarray(inv, F32)[None, :]
    sign = jnp.asarray(np.where(lane < 32, -1.0, 1.0), F32)[None, :]
    cos, sin = jnp.cos(ang), jnp.sin(ang) * sign
    return jnp.stack([cos * DA_Q_SCALE, sin * DA_Q_SCALE, cos, sin])
DA_MASK_CASES = 3
DA_UNROLL = 8


def da_mask_table():
    i = np.arange(2 * DA_QB)[None, :, None] % DA_QB
    j = np.arange(2 * DA_QB)[None, None, :]
    c = np.arange(DA_MASK_CASES)[:, None, None]
    return jnp.asarray(np.where(np.abs(i + DA_HALF * c - j) <= DA_HALF, 0.0, NEG), F32)


def _da_kernel(q_ref, k_ref, v_ref, mask_ref, o_ref, qs, ks, vs, acc_s, lse_s):
    qs[...] = q_ref[0].astype(F32)
    ks[...] = k_ref[0].astype(F32)
    vs[...] = v_ref[0].astype(F32)

    head0 = lax.broadcasted_iota(jnp.int32, (DA_QB, LANES), 1) < 64

    def block(g, dil, row0, krow0, nk, case):
        qb = qs[pl.ds(row0, DA_QB, stride=dil), :].astype(BF16)
        kb = ks[pl.ds(krow0, nk, stride=dil), :].astype(BF16)
        vb = vs[pl.ds(krow0, nk, stride=dil), :]
        zero = jnp.zeros_like(qb)
        q2 = jnp.concatenate([jnp.where(head0, qb, zero), jnp.where(head0, zero, qb)], axis=0)
        s = lax.dot_general(q2, kb, (((1,), (1,)), ((), ())), preferred_element_type=F32)
        s = s + mask_ref[case, :, 0:nk]
        m = jnp.max(s, axis=-1, keepdims=True)
        p = jnp.exp2(s - m).astype(BF16)
        in0 = lax.broadcasted_iota(jnp.int32, (nk, LANES), 1) < 64
        v2 = jnp.concatenate([
            jnp.concatenate([jnp.where(in0, vb, 0.0), jnp.where(in0, 1.0, 0.0)], axis=1),
            jnp.concatenate([jnp.where(in0, 0.0, vb), jnp.where(in0, 0.0, 1.0)], axis=1)],
            axis=0).astype(BF16)
        p2 = jnp.concatenate([p[:DA_QB], p[DA_QB:]], axis=1)
        ol = jnp.dot(p2, v2, preferred_element_type=F32)
        l = ol[:, LANES:]
        rows = pl.ds(row0, DA_QB, stride=dil)
        acc_s[g, rows, :] = ol[:, :LANES] / l
        lse_s[g, rows, :] = jnp.where(head0, m[:DA_QB], m[DA_QB:]) + jnp.log(l) * (1.0 / LN2)

    for g, dil in enumerate(DA_BRANCH_DIL):
        n_sub = SEQ // dil
        if n_sub == DA_QB:
            def body(r, carry, g=g, dil=dil):
                block(g, dil, r, r, DA_QB, 0)
                return carry
            lax.fori_loop(0, dil, body, 0, unroll=DA_UNROLL)
        else:
            nb = n_sub // DA_QB
            nk = 2 * DA_QB

            def body(j, carry, g=g, dil=dil, nb=nb, nk=nk, n_sub=n_sub):
                r = j // nb
                q0 = (j % nb) * DA_QB
                k0 = jnp.clip(q0 - DA_HALF, 0, n_sub - nk)
                block(g, dil, r + dil * q0, r + dil * k0, nk, (q0 - k0) // DA_HALF)
                return carry
            lax.fori_loop(0, dil * nb, body, 0, unroll=DA_UNROLL)

    lse_all = jnp.maximum(jnp.maximum(lse_s[0], lse_s[1]), lse_s[2])
    num = jnp.zeros((SEQ, LANES), F32)
    den = jnp.zeros((SEQ, LANES), F32)
    for g in range(len(DA_BRANCH_DIL)):
        w = jnp.exp2(lse_s[g] - lse_all)
        num = num + w * acc_s[g]
        den = den + w
    o_ref[0] = (num / den).astype(o_ref.dtype)


def dilated_attention(proj, mask_t):
    b = proj.shape[0]
    n_hp = DA_HEADS // 2
    blk = lambda off: pl.BlockSpec((1, SEQ, LANES), lambda i, hp, off=off: (i, 0, off + hp))
    nbr = len(DA_BRANCH_DIL)
    return pl.pallas_call(
        _da_kernel, grid=(b, n_hp),
        in_specs=[blk(0), blk(n_hp), blk(2 * n_hp),
                  pl.BlockSpec(mask_t.shape, lambda i, hp: (0, 0, 0))],
        out_specs=pl.BlockSpec((1, SEQ, LANES), lambda i, hp: (i, 0, hp)),
        out_shape=jax.ShapeDtypeStruct((b, SEQ, DA_W), BF16),
        scratch_shapes=[pltpu.VMEM((SEQ, LANES), F32)] * 3 + [pltpu.VMEM((nbr, SEQ, LANES), F32)] * 2,
        compiler_params=_cparams("parallel", "parallel"), name="dilated_attention",
    )(proj, proj, proj, mask_t)


ML_SCALE = ML_HEAD_DIM ** -0.5
GATE_I_FWD, GATE_I_BWD, GATE_F_FWD, GATE_F_BWD = 0, 4, 8, 12


def _exact_ones_matmul(ones_bf16, x):
    hi = x.astype(BF16)
    r1 = x - hi.astype(F32)
    mid = r1.astype(BF16)
    lo = (r1 - mid.astype(F32)).astype(BF16)
    dot = lambda t: jnp.dot(ones_bf16, t, preferred_element_type=F32)
    return dot(hi) + dot(mid) + dot(lo)


_NT = (((1,), (1,)), ((), ()))
ML_COMBOS = 2 * ML_HEADS
ML_HEADS_PER_TRIP = 4
ROW_CUM, ROW_RMAX, ROW_LOGW, ROW_TOT, ROW_LWMAX = range(5)


def _hi_lo_rows(row):
    hi = row.astype(BF16)
    lo = (row - hi.astype(F32)).astype(BF16)
    return jnp.concatenate([hi, lo, jnp.zeros((SUBLANES - 2, row.shape[1]), BF16)], axis=0)


def _mlstm_chunk(q, k, v_t, log_d_t, rows, state):
    c_t, n_vec, m_run = state
    cum, rmax, logw, tot, lwmax = rows
    log_inter = cum + m_run
    m_t = jnp.maximum(log_inter, rmax)
    w_inter = jnp.exp(log_inter - m_t)
    s_t = lax.dot_general(k, q, _NT, preferred_element_type=F32) * ML_SCALE * jnp.exp(log_d_t - m_t)
    num = w_inter * lax.dot_general(c_t.astype(BF16), q, _NT, preferred_element_type=F32)
    num = num + jnp.dot(v_t, s_t.astype(BF16), preferred_element_type=F32)
    nq = lax.dot_general(_hi_lo_rows(n_vec), q, _NT, preferred_element_type=F32)
    den = w_inter * (nq[0:1] + nq[1:2]) + jnp.sum(s_t, axis=0, keepdims=True)
    h_t = num / jnp.maximum(jnp.abs(den), jnp.exp(-m_t))
    m_new = jnp.maximum(tot + m_run, lwmax)
    w_row = jnp.exp(logw - m_new) * ML_SCALE
    decay = jnp.exp(tot + m_run - m_new)
    vw_t = (v_t.astype(F32) * w_row).astype(BF16)
    c_new = decay * c_t + jnp.dot(vw_t, k, preferred_element_type=F32)
    nk = jnp.dot(_hi_lo_rows(w_row), k, preferred_element_type=F32)
    n_new = decay * n_vec + nk[0:1] + nk[1:2]
    return h_t, (c_new, n_new, m_new)


def _mlstm_kernel(q_ref, k_ref, v_ref, og_ref, g_ref, gb_ref, ng_ref, out_ref, vt_s, ldt_s, rows_s, hf_s, hb_s):
    L = ML_CHUNK
    assert ML_HEAD_DIM == L == LANES
    ri = lax.broadcasted_iota(jnp.int32, (L, L), 0)
    ci = lax.broadcasted_iota(jnp.int32, (L, L), 1)
    lower = ci <= ri
    upper = ci >= ri
    lower_m = jnp.where(lower, 1.0, 0.0).astype(BF16)
    upper_m = jnp.where(upper, 1.0, 0.0).astype(BF16)
    lane = lax.broadcasted_iota(jnp.int32, (L, LANES), 1)
    fwd_rows = lax.broadcasted_iota(jnp.int32, (ML_COMBOS, L), 0) < ML_HEADS

    def prep(c, carry):
        rows = pl.ds(pl.multiple_of(c * L, L), L)
        vt_s[c] = v_ref[0, rows, :].T
        g = g_ref[0, rows, :] + gb_ref[...]
        lf = jnp.minimum(g, 0.0) - jnp.log1p(jnp.exp(-jnp.abs(g)))
        cum_f = _exact_ones_matmul(lower_m, lf)
        suf_b = _exact_ones_matmul(upper_m, lf)
        colv = jnp.where(lane < GATE_F_FWD, g, jnp.where(lane < GATE_F_BWD, cum_f, suf_b))
        rowv = colv.T
        i8 = rowv[0:ML_COMBOS]
        cum8 = rowv[ML_COMBOS:2 * ML_COMBOS]
        r_all = colv - pltpu.roll(colv, LANES - ML_COMBOS, 1)
        tot8 = jnp.where(fwd_rows, jnp.broadcast_to(cum8[:, L - 1:L], (ML_COMBOS, L)),
                         jnp.broadcast_to(cum8[:, 0:1], (ML_COMBOS, L)))
        logw8 = tot8 - cum8 + i8
        lwmax8 = jnp.broadcast_to(jnp.max(logw8, axis=-1, keepdims=True), (ML_COMBOS, L))
        rmax = []
        for j in range(ML_COMBOS):
            valid = upper if j < ML_HEADS else lower
            ldt = jnp.where(valid, jnp.broadcast_to(r_all[:, j:j + 1], (L, L)) + cum8[j:j + 1, :], NEG)
            ldt_s[c * ML_COMBOS + j] = ldt
            rmax.append(jnp.max(ldt, axis=0, keepdims=True))
        rows_s[c, ROW_CUM] = cum8
        rows_s[c, ROW_RMAX] = jnp.concatenate(rmax, axis=0)
        rows_s[c, ROW_LOGW] = logw8
        rows_s[c, ROW_TOT] = tot8
        rows_s[c, ROW_LWMAX] = lwmax8
        return carry

    lax.fori_loop(0, N_CHUNKS, prep, 0)

    zero_state = (jnp.zeros((ML_HEAD_DIM, ML_HEAD_DIM), F32), jnp.zeros((1, ML_HEAD_DIM), F32),
                  jnp.zeros((1, L), F32))
    def one(c, h, backward, state):
        hl = slice(ML_HEAD_DIM * h, ML_HEAD_DIM * (h + 1))
        j = ML_HEADS + h if backward else h
        rows = pl.ds(pl.multiple_of(c * L, L), L)
        prepared = tuple(rows_s[c, k, j:j + 1, :] for k in range(5))
        h_t, state = _mlstm_chunk(q_ref[0, rows, hl], k_ref[0, rows, hl], vt_s[c, hl, :],
                                  ldt_s[c * ML_COMBOS + j], prepared, state)
        (hb_s if backward else hf_s)[h, c] = h_t
        return state

    for h0 in range(0, ML_HEADS, ML_HEADS_PER_TRIP):
        def body(c, states, h0=h0):
            out = []
            for k in range(ML_HEADS_PER_TRIP):
                out.append(one(c, h0 + k, False, states[2 * k]))
                out.append(one(N_CHUNKS - 1 - c, h0 + k, True, states[2 * k + 1]))
            return tuple(out)

        lax.fori_loop(0, N_CHUNKS, body, (zero_state,) * (2 * ML_HEADS_PER_TRIP))

    for h in range(ML_HEADS):
        hl = slice(ML_HEAD_DIM * h, ML_HEAD_DIM * (h + 1))
        gain_t = jnp.broadcast_to(ng_ref[:, hl], (L, ML_HEAD_DIM)).T
        for c in range(N_CHUNKS):
            r0 = c * L
            hh = hf_s[h, c] + hb_s[h, c]
            mu = jnp.mean(hh, axis=0, keepdims=True)
            cen = hh - mu
            var = jnp.mean(cen * cen, axis=0, keepdims=True)
            hn = (cen * lax.rsqrt(var + LN_EPS) * gain_t).T
            og = og_ref[0, r0:r0 + L, hl].astype(F32)
            out_ref[0, r0:r0 + L, hl] = (hn / (1.0 + jnp.exp(-og))).astype(out_ref.dtype)


def mlstm_mixer(proj, gates, gate_bias, norm_g):
    b = proj.shape[0]
    c0 = (3 * NA_W) // ML_W
    blk = lambda off: pl.BlockSpec((1, SEQ, ML_W), lambda i, off=off: (i, 0, off))
    gb = jnp.pad(gate_bias.astype(F32), (0, LANES - gate_bias.shape[0])).reshape(1, LANES)
    return pl.pallas_call(
        _mlstm_kernel, grid=(b,),
        in_specs=[blk(c0), blk(c0 + 1), blk(c0 + 2), blk(c0 + 3),
                  pl.BlockSpec((1, SEQ, LANES), lambda i: (i, 0, 0)),
                  pl.BlockSpec((1, LANES), lambda i: (0, 0)),
                  pl.BlockSpec((1, ML_W), lambda i: (0, 0))],
        out_specs=pl.BlockSpec((1, SEQ, ML_W), lambda i: (i, 0, 0)),
        out_shape=jax.ShapeDtypeStruct((b, SEQ, ML_W), BF16),
        scratch_shapes=[pltpu.VMEM((N_CHUNKS, ML_W, ML_CHUNK), BF16),
                        pltpu.VMEM((N_CHUNKS * ML_COMBOS, ML_CHUNK, ML_CHUNK), F32),
                        pltpu.VMEM((N_CHUNKS, 5, ML_COMBOS, ML_CHUNK), F32),
                        pltpu.VMEM((ML_HEADS, N_CHUNKS, ML_HEAD_DIM, ML_CHUNK), F32),
                        pltpu.VMEM((ML_HEADS, N_CHUNKS, ML_HEAD_DIM, ML_CHUNK), F32)],
        compiler_params=_cparams("parallel"), name="mlstm_mixer",
    )(proj, proj, proj, proj, gates, gb, norm_g.reshape(1, ML_W).astype(F32))


def _router_kernel(x_ref, w_ref, aff_ref):
    x = x_ref[...]
    w = w_ref[...]
    x_hi = x.astype(BF16)
    x_lo = (x - x_hi.astype(F32)).astype(BF16)
    w_hi = w.astype(BF16)
    w_lo = (w - w_hi.astype(F32)).astype(BF16)
    dot = lambda a, b: jnp.dot(a, b, preferred_element_type=F32)
    logits_t = dot(x_hi, w_hi) + dot(x_hi, w_lo) + dot(x_lo, w_hi)
    logits = logits_t.T[0:N_EXPERTS, :]
    z = jnp.exp(logits - jnp.max(logits, axis=0, keepdims=True))
    aff_ref[...] = z / jnp.sum(z, axis=0, keepdims=True)


def router_affinities(x, w_router, tm=1024):
    n, d = x.shape
    w_pad = jnp.pad(w_router.astype(F32), ((0, 0), (0, LANES - N_EXPERTS)))
    return pl.pallas_call(
        _router_kernel, grid=(n // tm,),
        in_specs=[pl.BlockSpec((tm, d), lambda i: (i, 0)), pl.BlockSpec((d, LANES), lambda i: (0, 0))],
        out_specs=pl.BlockSpec((N_EXPERTS, tm), lambda i: (0, i)),
        out_shape=jax.ShapeDtypeStruct((N_EXPERTS, n), F32),
        compiler_params=_cparams("parallel"), name="router_affinities")(x, w_pad)


def _tri_matrices(r):
    li = lax.broadcasted_iota(jnp.int32, (LANES, LANES), 0)
    lj = lax.broadcasted_iota(jnp.int32, (LANES, LANES), 1)
    tri_u = jnp.where(li <= lj, 1.0, 0.0).astype(BF16)
    ri = lax.broadcasted_iota(jnp.int32, (r, r), 0)
    rj = lax.broadcasted_iota(jnp.int32, (r, r), 1)
    tri_l = jnp.where(rj < ri, 1.0, 0.0).astype(BF16)
    return tri_u, tri_l


def _prefix_counts(mask, tri_u, tri_l):
    r = mask.shape[0]
    within = jnp.dot(mask.astype(BF16), tri_u, preferred_element_type=F32)
    rowtot = within[:, LANES - 1:LANES]
    hi = jnp.floor(rowtot * (1.0 / 16.0))
    lo = rowtot - 16.0 * hi
    hi_b = jnp.broadcast_to(hi, (r, LANES)).astype(BF16)
    lo_b = jnp.broadcast_to(lo, (r, LANES)).astype(BF16)
    rowoff = 16.0 * jnp.dot(tri_l, hi_b, preferred_element_type=F32) + jnp.dot(
        tri_l, lo_b, preferred_element_type=F32)
    return within - mask + rowoff, within, rowoff, rowtot


def _select_kernel(aff_ref, sel_ref, *, cap):
    r = aff_ref.shape[1]
    tri_u, tri_l = _tri_matrices(r)
    bits = pltpu.bitcast(aff_ref[0], jnp.int32)

    def count(m):
        c = jnp.sum(jnp.where(m, 1.0, 0.0), axis=1, keepdims=True)
        return jnp.sum(c, axis=0, keepdims=True)

    def bisect(i, prefix):
        cand = prefix | jnp.left_shift(jnp.int32(1), 30 - i)
        return jnp.where(count(bits >= cand) >= cap, cand, prefix)

    thr = lax.fori_loop(0, 31, bisect, jnp.zeros((1, 1), jnp.int32))
    gt = bits > thr
    eq = bits == thr
    need = cap - count(gt)
    rank_eq, _, _, _ = _prefix_counts(jnp.where(eq, 1.0, 0.0), tri_u, tri_l)
    sel_ref[0] = jnp.where(gt | (eq & (rank_eq < need)), 1.0, 0.0)


def select_tokens(aff3, cap):
    e, r, _ = aff3.shape
    blk = pl.BlockSpec((1, r, LANES), lambda i: (i, 0, 0))
    return pl.pallas_call(
        functools.partial(_select_kernel, cap=cap), grid=(e,), in_specs=[blk], out_specs=blk,
        out_shape=jax.ShapeDtypeStruct((e, r, LANES), F32),
        compiler_params=_cparams("parallel"), name="select_tokens")(aff3)


def _lists_kernel(sel_ref, aff_ref, idx_ref, dst_ref, gate_ref, ts_ref, ts_s, er_s, *, cap, st):
    e = pl.program_id(0)
    r = sel_ref.shape[1]
    tri_u, tri_l = _tri_matrices(r)

    @pl.when(e == 0)
    def _():
        cnt = sel_ref[0]
        for k in range(1, N_EXPERTS):
            cnt = cnt + sel_ref[k]
        ts, _, _, _ = _prefix_counts(cnt, tri_u, tri_l)
        ts_s[...] = ts
        ts_ref[...] = ts
        er_s[...] = jnp.zeros_like(er_s)

    sel = sel_ref[e]
    _, within, rowoff, rowtot = _prefix_counts(sel, tri_u, tri_l)
    dst = ts_s[...] + er_s[...]
    er_s[...] = er_s[...] + sel

    d2 = jnp.floor(dst * (1.0 / 65536.0))
    rem = dst - 65536.0 * d2
    d1 = jnp.floor(rem * (1.0 / 256.0))
    d0 = rem - 256.0 * d1
    aff = aff_ref[0]
    a_hi = aff.astype(BF16)
    a_r1 = aff - a_hi.astype(F32)
    a_mid = a_r1.astype(BF16)
    a_lo = (a_r1 - a_mid.astype(F32)).astype(BF16)
    rhs = jnp.concatenate([within.astype(BF16), d0.astype(BF16), d1.astype(BF16), d2.astype(BF16),
                           a_hi, a_mid, a_lo], axis=1)
    rowoff_row = rowoff.T[0:1, :]
    rowend_row = rowoff_row + jnp.broadcast_to(rowtot, (r, LANES)).T[0:1, :]
    rho_row = lax.broadcasted_iota(jnp.int32, (1, r), 1).astype(F32)
    lane = lax.broadcasted_iota(jnp.int32, (st, LANES), 1).astype(F32)
    eye = lax.broadcasted_iota(jnp.int32, (LANES, LANES), 0) == lax.broadcasted_iota(
        jnp.int32, (LANES, LANES), 1)

    for t in range(cap // st):
        s_col = (t * st + lax.broadcasted_iota(jnp.int32, (st, 1), 0)).astype(F32)
        in_row = (rowoff_row <= s_col) & (s_col < rowend_row)
        got = jnp.dot(jnp.where(in_row, 1.0, 0.0).astype(BF16), rhs, preferred_element_type=F32)
        base = jnp.sum(jnp.where(in_row, rowoff_row, 0.0), axis=1, keepdims=True)
        rho = jnp.sum(jnp.where(in_row, rho_row, 0.0), axis=1, keepdims=True)
        local = s_col - base
        lam = jnp.sum(jnp.where(got[:, 0:LANES] <= local, 1.0, 0.0), axis=1, keepdims=True)
        plane = lambda k: got[:, k * LANES:(k + 1) * LANES]
        pair = plane(1) + 256.0 * plane(2) + 65536.0 * plane(3)
        at_lam = lane == lam
        dval = jnp.sum(jnp.where(at_lam, pair, 0.0), axis=1, keepdims=True)
        gate_ref[0, t * st:(t + 1) * st, :] = jnp.sum(
            jnp.where(at_lam, plane(4) + plane(5) + plane(6), 0.0), axis=1, keepdims=True)
        ival = rho * float(LANES) + lam
        for j in range(st // LANES):
            seg = slice(j * LANES, (j + 1) * LANES)
            row = t * (st // LANES) + j
            idx_ref[0, row:row + 1, :] = jnp.sum(
                jnp.where(eye, ival[seg], 0.0), axis=0, keepdims=True).astype(jnp.int32)
            dst_ref[0, row:row + 1, :] = jnp.sum(
                jnp.where(eye, dval[seg], 0.0), axis=0, keepdims=True).astype(jnp.int32)


def build_lists(sel3, aff3, cap):
    e, r, _ = sel3.shape
    st = min(512, cap)
    lst = pl.BlockSpec((1, cap // LANES, LANES), lambda i: (i, 0, 0))
    return pl.pallas_call(
        functools.partial(_lists_kernel, cap=cap, st=st), grid=(e,),
        in_specs=[pl.BlockSpec((e, r, LANES), lambda i: (0, 0, 0)),
                  pl.BlockSpec((1, r, LANES), lambda i: (i, 0, 0))],
        out_specs=[lst, lst, pl.BlockSpec((1, cap, 1), lambda i: (i, 0, 0)),
                   pl.BlockSpec((r, LANES), lambda i: (0, 0))],
        out_shape=[jax.ShapeDtypeStruct((e, cap // LANES, LANES), jnp.int32),
                   jax.ShapeDtypeStruct((e, cap // LANES, LANES), jnp.int32),
                   jax.ShapeDtypeStruct((e, cap, 1), F32), jax.ShapeDtypeStruct((r, LANES), F32)],
        scratch_shapes=[pltpu.VMEM((r, LANES), F32), pltpu.VMEM((r, LANES), F32)],
        compiler_params=_cparams("arbitrary"), name="build_lists")(sel3, aff3)


FFN_CHUNKS = tuple((f, min(f + 256, D_FF)) for f in range(0, D_FF, 256))


def _ffn_kernel(idx_first, idx_next_a, idx_next_b, dst_prev_a, dst_prev_b, dst_last,
                x_hbm, gate_a, gate_b, wg_ref, wu_ref, wd_ref, z_hbm,
                xbuf0, xbuf1, ybuf0, ybuf1, acc, gsem, ssem, *, tm, n_grid):
    g = pl.program_id(0)

    def token_tile(t):
        if isinstance(t, int):
            return pl.ds(t * TILE_ROWS, TILE_ROWS)
        return pl.ds(pl.multiple_of(t * TILE_ROWS, TILE_ROWS), TILE_ROWS)

    def gather_row(ids, i, buf, sem):
        pltpu.make_async_copy(x_hbm.at[token_tile(ids[0, 0, i]), :], buf.at[token_tile(i), :],
                              sem).start(priority=1)

    def scatter_row(dsts, i, buf, sem):
        pltpu.make_async_copy(buf.at[token_tile(i), :], z_hbm.at[token_tile(dsts[0, 0, i]), :],
                              sem).start(priority=0)

    def wait_rows(buf, sem):
        pltpu.make_async_copy(x_hbm.at[pl.ds(0, tm * TILE_ROWS), :], buf, sem).wait()

    @pl.when(g == 0)
    def _():
        ybuf1[...] = jnp.zeros_like(ybuf1)

        def first(i, carry):
            gather_row(idx_first, i, xbuf0, gsem.at[0])
            return carry
        lax.fori_loop(0, tm, first, 0, unroll=8)

    def tile(xcur, gcur, xnext, gnext, idx_next, yprev, sprev, dst_prev):
        for i in range(tm):
            gather_row(idx_next, i, xnext, gnext)
            scatter_row(dst_prev, i, yprev, sprev)

        wait_rows(xcur, gcur)
        xb = _load_token_tiles(xcur, tm).astype(BF16)
        for c, (f0, f1) in enumerate(FFN_CHUNKS):
            gate = jnp.dot(xb, wg_ref[0, :, f0:f1], preferred_element_type=F32)
            up = jnp.dot(xb, wu_ref[0, :, f0:f1], preferred_element_type=F32)
            h = (gate / (1.0 + jnp.exp(-gate)) * up).astype(BF16)
            part = jnp.dot(h, wd_ref[0, f0:f1, :], preferred_element_type=F32)
            if c == 0:
                acc[...] = part
            else:
                acc[...] += part

    tile(xbuf0, gsem.at[0], xbuf1, gsem.at[1], idx_next_a, ybuf1, ssem.at[1], dst_prev_a)

    @pl.when(g >= 1)
    def _():
        wait_rows(ybuf0, ssem.at[0])
    _store_token_tiles(ybuf0, acc[...] * gate_a[0])

    tile(xbuf1, gsem.at[1], xbuf0, gsem.at[0], idx_next_b, ybuf0, ssem.at[0], dst_prev_b)
    wait_rows(ybuf1, ssem.at[1])
    _store_token_tiles(ybuf1, acc[...] * gate_b[0])

    @pl.when(g == n_grid - 1)
    def _():
        def last(i, carry):
            scatter_row(dst_last, i, ybuf1, ssem.at[1])
            return carry
        lax.fori_loop(0, tm, last, 0, unroll=8)
        wait_rows(ybuf1, ssem.at[1])
        wait_rows(ybuf0, ssem.at[0])
        wait_rows(xbuf0, gsem.at[0])


def expert_ffn(x_tiles, idx, dst, gate, w_gate, w_up, w_down, tm=512):
    d = D_MODEL
    e, cap = idx.shape
    tm = min(tm, cap // 2)
    nt = cap // tm
    assert nt % 2 == 0
    n_tiles = e * nt
    n_grid = n_tiles // 2
    idx3 = idx.reshape(n_tiles, 1, tm)
    gate3 = gate.reshape(n_tiles, tm, 1)
    spare =(e * cap + jnp.arange(tm, dtype=jnp.int32)).reshape(1, 1, tm)
    dst3 = jnp.concatenate([spare, dst.reshape(n_tiles, 1, tm)])
    smem = lambda imap: pl.BlockSpec((1, 1, tm), imap, memory_space=pltpu.SMEM)
    wspec = lambda w: pl.BlockSpec((1,) + w.shape[1:], lambda i: ((2 * i) // nt, 0, 0))
    return pl.pallas_call(
        functools.partial(_ffn_kernel, tm=tm, n_grid=n_grid), grid=(n_grid,),
        in_specs=[smem(lambda i: (0, 0, 0)), smem(lambda i: (2 * i + 1, 0, 0)),
                  smem(lambda i: (jnp.minimum(2 * i + 2, n_tiles - 1), 0, 0)),
                  smem(lambda i: (2 * i, 0, 0)), smem(lambda i: (2 * i + 1, 0, 0)),
                  smem(lambda i: (n_tiles, 0, 0)),
                  pl.BlockSpec(memory_space=pl.ANY),
                  pl.BlockSpec((1, tm, 1), lambda i: (2 * i, 0, 0)),
                  pl.BlockSpec((1, tm, 1), lambda i: (2 * i + 1, 0, 0)),
                  wspec(w_gate), wspec(w_up), wspec(w_down)],
        out_specs=pl.BlockSpec(memory_space=pl.ANY),
        out_shape=jax.ShapeDtypeStruct(((e * cap + tm) * TILE_ROWS, LANES), F32),
        scratch_shapes=[pltpu.VMEM((tm * TILE_ROWS, LANES), F32)] * 4 + [pltpu.VMEM((tm, d), F32)]
        + [pltpu.SemaphoreType.DMA((2,)), pltpu.SemaphoreType.DMA((2,))],
        compiler_params=_cparams("arbitrary"), name="expert_ffn",
    )(idx3, idx3, idx3, dst3, dst3, dst3, x_tiles, gate3, gate3, w_gate, w_up, w_down)


COMBINE_ZB = 512
COMBINE_SLOTS = 4


def _combine_kernel(ts_ref, x_ref, run_ref, g_ref, b_ref, z_hbm, o_ref, zbuf, sem, used, *, z_rows, n_tiles):
    i = pl.program_id(0)
    tt = x_ref.shape[0]

    def chunk_rows(t, c):
        lo = ts_ref[t] + c * COMBINE_ZB
        return lo, jnp.minimum(lo, z_rows - COMBINE_ZB)

    def piece_copy(start, j, slot):
        return pltpu.make_async_copy(z_hbm.at[pl.ds(start, COMBINE_ZB), j, :], zbuf.at[slot, j], sem.at[slot])

    def chunks_of(t):
        return jnp.maximum((ts_ref[t + 1] - ts_ref[t] + COMBINE_ZB - 1) // COMBINE_ZB, 1)

    def fetch_next():
        t = used[2]

        @pl.when(t < n_tiles)
        def _():
            c = used[3]
            slot = used[1] % COMBINE_SLOTS
            _, start = chunk_rows(t, c)
            for j in range(TILE_ROWS):
                piece_copy(start, j, slot).start()
            used[1] = used[1] + 1
            last = c + 1 >= chunks_of(t)
            used[2] = jnp.where(last, t + 1, t)
            used[3] = jnp.where(last, 0, c + 1)

    @pl.when(i == 0)
    def _():
        for k in range(4):
            used[k] = 0
        for _ in range(COMBINE_SLOTS - 1):
            fetch_next()

    base = used[0]
    n_chunks = chunks_of(i)
    run_lo = jnp.broadcast_to(run_ref[:, 0:1], (tt, COMBINE_ZB))
    run_hi = jnp.broadcast_to(run_ref[:, 1:2], (tt, COMBINE_ZB))
    col = lax.broadcasted_iota(jnp.int32, (1, COMBINE_ZB), 1).astype(F32)

    def chunk(c, acc):
        slot = (base + c) % COMBINE_SLOTS
        lo, start = chunk_rows(i, c)
        pair = col + start.astype(F32)
        pair = jnp.where(pair >= lo.astype(F32), pair, -1.0)
        a = jnp.where((run_lo <= pair) & (pair < run_hi), 1.0, 0.0).astype(BF16)
        for j in range(TILE_ROWS):
            piece_copy(0, j, slot).wait()
        rows = jnp.concatenate([zbuf[slot, j] for j in range(TILE_ROWS)], axis=1).astype(BF16)
        acc = acc + jnp.dot(a, rows, preferred_element_type=F32)
        fetch_next()
        return acc

    ffn = lax.fori_loop(0, n_chunks, chunk, jnp.zeros((tt, D_MODEL), F32))
    used[0] = base + n_chunks
    o_ref[...] = _layer_norm_rows(DN_ALPHA * x_ref[...] + ffn, g_ref[...], b_ref[...])


def combine_ln(x, z, tile_start, runs, g, b, tt=256):
    n, d = x.shape
    n_tiles = n // tt
    row = lambda i, ts: (i, 0)
    fixed = lambda i, ts: (0, 0)
    grid_spec = pltpu.PrefetchScalarGridSpec(
        num_scalar_prefetch=1, grid=(n_tiles,),
        in_specs=[pl.BlockSpec((tt, d), row), pl.BlockSpec((tt, 2), row), pl.BlockSpec((1, d), fixed),
                  pl.BlockSpec((1, d), fixed), pl.BlockSpec(memory_space=pl.ANY)],
        out_specs=pl.BlockSpec((tt, d), row),
        scratch_shapes=[pltpu.VMEM((COMBINE_SLOTS, TILE_ROWS, COMBINE_ZB, LANES), F32),
                        pltpu.SemaphoreType.DMA((COMBINE_SLOTS,)), pltpu.SMEM((4,), jnp.int32)])
    z_rows = z.shape[0] // TILE_ROWS
    return pl.pallas_call(
        functools.partial(_combine_kernel, z_rows=z_rows, n_tiles=n_tiles), grid_spec=grid_spec,
        out_shape=jax.ShapeDtypeStruct((n, d), F32),
        compiler_params=_cparams("arbitrary"), name="combine_ln",
    )(tile_start, x, runs, g.reshape(1, d), b.reshape(1, d), z.reshape(z_rows, TILE_ROWS, LANES))


def moe_layer(x, x_tiles, w_router, w_gate, w_up, w_down, g, b, tt=256):
    n, _ = x.shape
    r = n // LANES
    cap = 2 * n // N_EXPERTS
    aff3 = router_affinities(x, w_router).reshape(N_EXPERTS, r, LANES)
    sel3 = select_tokens(aff3, cap)
    idx, dst, gate, ts = build_lists(sel3, aff3, cap)
    z = expert_ffn(x_tiles, idx.reshape(N_EXPERTS, cap), dst.reshape(N_EXPERTS, cap), gate, w_gate, w_up, w_down)
    ts_ext = jnp.concatenate([ts.reshape(n), jnp.full((1,), N_EXPERTS * cap, F32)])
    tile_start = ts_ext[::tt].astype(jnp.int32)
    runs = jnp.stack([ts_ext[:-1], ts_ext[1:]], axis=1)
    return combine_ln(x, z, tile_start, runs, g, b, tt=tt)


def kernel(x_prompt, x_sample, even_w_in, ml_gate_bias, na_rpb, ml_norm_g, even_w_out, da_w_in, da_w_out,
           ln_mix_g, ln_mix_b, ec_router, ec_w_gate, ec_w_up, ec_w_down, ln_ffn_g, ln_ffn_b):
    w_even = even_w_in[0][:, :3584].astype(BF16)
    w_gates = jnp.pad(even_w_in[0][:, 3584:], ((0, 0), (0, LANES - 16))).astype(BF16)
    w_out_a = even_w_out[0][:NA_W].astype(BF16)
    w_out_b = even_w_out[0][NA_W:].astype(BF16)
    w_odd = da_w_in[0].astype(BF16)
    w_odd_out = da_w_out[0].astype(BF16)
    tbl = na_bias_table(na_rpb[0])
    rope_t = rope_tables()
    mask_t = da_mask_table()
    moe_w = [(ec_router[l], ec_w_gate[l].astype(BF16), ec_w_up[l].astype(BF16), ec_w_down[l].astype(BF16),
              ln_ffn_g[l], ln_ffn_b[l]) for l in range(DEPTH)]

    def trunk(x):
        b = x.shape[0]
        xt = x.reshape(b * SEQ, D_MODEL)
        proj, gates = in_projection(xt, w_even, w_gates)
        proj = proj.reshape(b, SEQ, -1)
        ya = neighbourhood_attention(proj, tbl)
        yb = mlstm_mixer(proj, gates.reshape(b, SEQ, LANES), ml_gate_bias[0], ml_norm_g[0])
        xt, xt_tiles = out_projection_ln(xt, [ya.reshape(b * SEQ, NA_W), yb.reshape(b * SEQ, ML_W)],
                                         [w_out_a, w_out_b], ln_mix_g[0], ln_mix_b[0])
        xt = moe_layer(xt, xt_tiles, *moe_w[0])
        proj = in_projection_rope(xt, w_odd, rope_t, DA_W).reshape(b, SEQ, -1)
        yc = dilated_attention(proj, mask_t)
        xt, xt_tiles = out_projection_ln(xt, [yc.reshape(b * SEQ, DA_W)], [w_odd_out],
                                         ln_mix_g[1], ln_mix_b[1])
        xt = moe_layer(xt, xt_tiles, *moe_w[1])
        return xt.reshape(b, SEQ, D_MODEL)

    return trunk(x_prompt), trunk(x_sample)
```

```python
import functools

import numpy as np
import jax
import jax.numpy as jnp
from jax import lax
from jax.experimental import pallas as pl
from jax.experimental.pallas import tpu as pltpu

F32 = jnp.float32
BF16 = jnp.bfloat16

D_MODEL = 1024
SEQ = 2048
GRID_W = 64
GRID_ROWS = SEQ // GRID_W
NA_HEADS = 8
NA_W = 512
NA_WIN_R = 8
NA_WIN_C = 16
ML_HEADS = 4
ML_HEAD_DIM = 128
ML_W = 512
ML_CHUNK = 128
N_CHUNKS = SEQ // ML_CHUNK
DA_HEADS = 16
DA_W = 1024
DA_HALF = 64
ROPE_THETA = 10000.0
N_EXPERTS = 16
D_FF = 1408
LN_EPS = 1e-5
DEPTH = 2
DN_ALPHA = (2 * DEPTH) ** 0.25
LANES = 128
NEG = -1e30
LOG2E = 1.4426950408889634
LN2 = 0.6931471805599453
VMEM_LIMIT = 56 * 1024 * 1024


def _cparams(*sem):
    return pltpu.CompilerParams(dimension_semantics=sem, vmem_limit_bytes=VMEM_LIMIT)


def _inproj_kernel(x_ref, w_ref, o_ref, *, n_chunk):
    xb = x_ref[...].astype(BF16)
    for c in range(0, w_ref.shape[1], n_chunk):
        o_ref[:, c:c + n_chunk] = jnp.dot(
            xb, w_ref[:, c:c + n_chunk], preferred_element_type=F32).astype(o_ref.dtype)


def _inproj_gates_kernel(x_ref, w_ref, wg_ref, o_ref, g_ref, *, n_chunk):
    xb = x_ref[...].astype(BF16)
    for c in range(0, w_ref.shape[1], n_chunk):
        o_ref[:, c:c + n_chunk] = jnp.dot(
            xb, w_ref[:, c:c + n_chunk], preferred_element_type=F32).astype(o_ref.dtype)
    g_ref[...] = jnp.dot(xb, wg_ref[...], preferred_element_type=F32)


def _rotate_half_pairs(x, cos, sin_signed):
    first_half = lax.broadcasted_iota(jnp.int32, x.shape, 1) % 64 < 32
    swapped = jnp.where(first_half, pltpu.roll(x, 96, 1), pltpu.roll(x, 32, 1))
    return x * cos + swapped * sin_signed


def _inproj_rope_kernel(x_ref, w_ref, rope_ref, o_ref, *, n_chunk, width):
    xb = x_ref[...].astype(BF16)
    for c in range(0, w_ref.shape[1], n_chunk):
        res = jnp.dot(xb, w_ref[:, c:c + n_chunk], preferred_element_type=F32)
        if c < 2 * width:
            t = 0 if c < width else 2
            res = jnp.concatenate(
                [_rotate_half_pairs(res[:, l:l + LANES], rope_ref[t], rope_ref[t + 1])
                 for l in range(0, n_chunk, LANES)], axis=1)
        o_ref[:, c:c + n_chunk] = res.astype(o_ref.dtype)


def in_projection_rope(x, w, rope_tables, width, tm=512, n_chunk=512):
    m, k = x.shape
    n = w.shape[1]
    assert width % n_chunk == 0 and SEQ % tm == 0
    return pl.pallas_call(
        functools.partial(_inproj_rope_kernel, n_chunk=n_chunk, width=width),
        grid=(m // tm,),
        in_specs=[pl.BlockSpec((tm, k), lambda i: (i, 0)), pl.BlockSpec((k, n), lambda i: (0, 0)),
                  pl.BlockSpec((4, tm, LANES), lambda i: (0, i % (SEQ // tm), 0))],
        out_specs=pl.BlockSpec((tm, n), lambda i: (i, 0)),
        out_shape=jax.ShapeDtypeStruct((m, n), BF16),
        compiler_params=_cparams("parallel"), name="in_projection_rope")(x, w, rope_tables)


def in_projection(x, w, wg=None, tm=512, n_chunk=512):
    m, k = x.shape
    n = w.shape[1]
    x_spec = pl.BlockSpec((tm, k), lambda i: (i, 0))
    w_spec = pl.BlockSpec((k, n), lambda i: (0, 0))
    o_spec = pl.BlockSpec((tm, n), lambda i: (i, 0))
    if wg is None:
        return pl.pallas_call(
            functools.partial(_inproj_kernel, n_chunk=n_chunk),
            grid=(m // tm,), in_specs=[x_spec, w_spec], out_specs=o_spec,
            out_shape=jax.ShapeDtypeStruct((m, n), BF16),
            compiler_params=_cparams("parallel"), name="in_projection")(x, w)
    return pl.pallas_call(
        functools.partial(_inproj_gates_kernel, n_chunk=n_chunk),
        grid=(m // tm,),
        in_specs=[x_spec, w_spec, pl.BlockSpec((k, LANES), lambda i: (0, 0))],
        out_specs=[o_spec, pl.BlockSpec((tm, LANES), lambda i: (i, 0))],
        out_shape=[jax.ShapeDtypeStruct((m, n), BF16), jax.ShapeDtypeStruct((m, LANES), F32)],
        compiler_params=_cparams("parallel"), name="in_projection_gates")(x, w, wg)


def _layer_norm_rows(acc, g, b):
    mu = jnp.mean(acc, axis=-1, keepdims=True)
    cen = acc - mu
    var = jnp.mean(cen * cen, axis=-1, keepdims=True)
    return cen * lax.rsqrt(var + LN_EPS) * g + b


def _outproj_ln_kernel(*refs, n_mix):
    x_ref = refs[0]
    mix_refs = refs[1:1 + n_mix]
    w_refs = refs[1 + n_mix:1 + 2 * n_mix]
    g_ref, b_ref, ot_ref = refs[1 + 2 * n_mix:]
    acc = DN_ALPHA * x_ref[...]
    for m_ref, w_ref in zip(mix_refs, w_refs):
        acc = acc + jnp.dot(m_ref[...], w_ref[...], preferred_element_type=F32)
    _store_token_tiles(ot_ref, _layer_norm_rows(acc, g_ref[...], b_ref[...]))


SUBLANES = 8
TILE_ROWS = D_MODEL // LANES


def _store_token_tiles(ref, rows):
    n = rows.shape[0]
    for j in range(TILE_ROWS):
        ref[pl.ds(j, n, stride=TILE_ROWS), :] = rows[:, j * LANES:(j + 1) * LANES]


def _load_token_tiles(ref, n, first_row=0):
    return jnp.concatenate(
        [ref[pl.ds(first_row + j, n, stride=TILE_ROWS), :] for j in range(TILE_ROWS)], axis=1)


def out_projection_ln(x, mixes, ws, g, b, tm=512):
    m, d = x.shape
    n_mix = len(mixes)
    row = lambda i: (i, 0)
    fixed = lambda i: (0, 0)
    in_specs = [pl.BlockSpec((tm, d), row)]
    in_specs += [pl.BlockSpec((tm, mx.shape[1]), row) for mx in mixes]
    in_specs += [pl.BlockSpec(w.shape, fixed) for w in ws]
    in_specs += [pl.BlockSpec((1, d), fixed), pl.BlockSpec((1, d), fixed)]
    return pl.pallas_call(
        functools.partial(_outproj_ln_kernel, n_mix=n_mix),
        grid=(m // tm,), in_specs=in_specs,
        out_specs=pl.BlockSpec((tm * TILE_ROWS, LANES), row),
        out_shape=jax.ShapeDtypeStruct((m * TILE_ROWS, LANES), F32),
        compiler_params=_cparams("parallel"), name="out_projection_ln",
    )(x, *mixes, *ws, g.reshape(1, d), b.reshape(1, d))


NA_PAIR_ROWS = 10
NA_PAIR_KEYS = NA_PAIR_ROWS * GRID_W
NA_PAIR_CASES = 5
NA_N_PAIRS = GRID_ROWS // 2


def _na_pair_window(p):
    start = jnp.minimum(jnp.clip(2 * p - NA_WIN_R // 2, 0, GRID_ROWS - NA_WIN_R), GRID_ROWS - NA_PAIR_ROWS)
    case = jnp.minimum(p, 2) + jnp.maximum(p - (NA_N_PAIRS - 3), 0)
    return start, case


def na_bias_table(rpb):
    j = np.arange(GRID_W)
    kc = np.arange(GRID_W)
    win_c0 = np.clip(j - NA_WIN_C // 2, 0, GRID_W - NA_WIN_C)
    valid = (kc[None, :] >= win_c0[:, None]) & (kc[None, :] < win_c0[:, None] + NA_WIN_C)
    dc = np.clip(kc[None, :] - j[:, None] + NA_WIN_C - 1, 0, 2 * NA_WIN_C - 2)
    n_dc = 2 * NA_WIN_C - 1
    pick = jnp.asarray(dc[:, :, None] == np.arange(n_dc)[None, None, :], F32)
    rows = jnp.einsum('hrd,jkd->hrjk', rpb.astype(F32), pick, precision=lax.Precision.HIGHEST)
    rows = jnp.where(valid[None, None], rows * LOG2E, NEG)
    masked = jnp.full_like(rows[:, 0], NEG)
    representative = (0, 1, 2, NA_N_PAIRS - 2, NA_N_PAIRS - 1)
    cases = []
    for p in representative:
        start = min(max(2 * p - NA_WIN_R // 2, 0), GRID_ROWS - NA_WIN_R, GRID_ROWS - NA_PAIR_ROWS)
        per_row = []
        for r in (2 * p, 2 * p + 1):
            rs = min(max(r - NA_WIN_R // 2, 0), GRID_ROWS - NA_WIN_R)
            per_row.append(jnp.stack(
                [rows[:, start + w - r + NA_WIN_R - 1] if rs <= start + w < rs + NA_WIN_R else masked
                 for w in range(NA_PAIR_ROWS)], axis=1))
        cases.append(jnp.stack(per_row, axis=1))
    t = jnp.stack(cases, axis=1)
    t = t.transpose(0, 1, 2, 4, 3, 5).reshape(NA_HEADS // 2, 2, NA_PAIR_CASES, 2 * GRID_W, NA_PAIR_KEYS)
    return t.transpose(0, 2, 1, 3, 4).reshape(NA_HEADS // 2, NA_PAIR_CASES, 4 * GRID_W, NA_PAIR_KEYS)


NA_SCORE_SCALE = 64 ** -0.5 * LOG2E
NA_UNROLL = 8


def _na_kernel(q_ref, k_ref, v_ref, tbl_ref, o_ref, v2a, v2b):
    nq = 2 * GRID_W
    head0 = lax.broadcasted_iota(jnp.int32, (nq, LANES), 1) < 64

    in0 = lax.broadcasted_iota(jnp.int32, (SEQ, LANES), 1) < 64
    vf = v_ref[0].astype(F32)
    v2a[...] = jnp.concatenate([jnp.where(in0, vf, 0.0), jnp.where(in0, 1.0, 0.0)], axis=1).astype(BF16)
    v2b[...] = jnp.concatenate([jnp.where(in0, 0.0, vf), jnp.where(in0, 0.0, 1.0)], axis=1).astype(BF16)

    def pair(p, carry):
        start, case = _na_pair_window(p)
        rows = pl.ds(pl.multiple_of(p * nq, nq), nq)
        q = q_ref[0, rows, :]
        zero = jnp.zeros_like(q)
        q2 = jnp.concatenate([jnp.where(head0, q, zero), jnp.where(head0, zero, q)], axis=0)
        win = pl.ds(pl.multiple_of(start * GRID_W, GRID_W), NA_PAIR_KEYS)
        s = lax.dot_general(q2, k_ref[0, win, :], (((1,), (1,)), ((), ())), preferred_element_type=F32)
        s = s * NA_SCORE_SCALE + tbl_ref[0, case]
        pr = jnp.exp2(s - jnp.max(s, axis=-1, keepdims=True)).astype(BF16)
        p2 = jnp.concatenate([pr[:nq], pr[nq:]], axis=1)
        ol = jnp.dot(p2, jnp.concatenate([v2a[win, :], v2b[win, :]], axis=0), preferred_element_type=F32)
        o_ref[0, rows, :] = (ol[:, :LANES] / ol[:, LANES:]).astype(o_ref.dtype)
        return carry

    lax.fori_loop(0, NA_N_PAIRS, pair, 0, unroll=NA_UNROLL)


def neighbourhood_attention(proj, tbl):
    b = proj.shape[0]
    n_hp = NA_HEADS // 2
    blk = lambda off: pl.BlockSpec((1, SEQ, LANES), lambda hp, i, off=off: (i, 0, off + hp))
    return pl.pallas_call(
        _na_kernel, grid=(n_hp, b),
        in_specs=[blk(0), blk(n_hp), blk(2 * n_hp),
                  pl.BlockSpec((1, NA_PAIR_CASES, 4 * GRID_W, NA_PAIR_KEYS), lambda hp, i: (hp, 0, 0, 0))],
        out_specs=pl.BlockSpec((1, SEQ, LANES), lambda hp, i: (i, 0, hp)),
        out_shape=jax.ShapeDtypeStruct((b, SEQ, NA_W), BF16),
        scratch_shapes=[pltpu.VMEM((SEQ, 2 * LANES), BF16)] * 2,
        compiler_params=_cparams("parallel", "parallel"), name="neighbourhood_attention",
    )(proj, proj, proj, tbl)


DA_BRANCH_DIL = (1, 4, 16)
DA_QB = 128


DA_Q_SCALE = 64 ** -0.5 * LOG2E


def rope_tables():
    lane = np.arange(LANES) % 64
    inv = ROPE_THETA ** (-(2.0 * (lane % 32)) / 64.0)
    ang = jnp.arange(SEQ, dtype=F32)[:, None] * jnp.asarray(inv, F32)[None, :]
    sign = jnp.asarray(np.where(lane < 32, -1.0, 1.0), F32)[None, :]
    cos, sin = jnp.cos(ang), jnp.sin(ang) * sign
    return jnp.stack([cos * DA_Q_SCALE, sin * DA_Q_SCALE, cos, sin])
DA_MASK_CASES = 3
DA_UNROLL = 8


def da_mask_table():
    i = np.arange(2 * DA_QB)[None, :, None] % DA_QB
    j = np.arange(2 * DA_QB)[None, None, :]
    c = np.arange(DA_MASK_CASES)[:, None, None]
    return jnp.asarray(np.where(np.abs(i + DA_HALF * c - j) <= DA_HALF, 0.0, NEG), F32)


def _da_kernel(q_ref, k_ref, v_ref, mask_ref, o_ref, qs, ks, vs, acc_s, lse_s):
    qs[...] = q_ref[0].astype(F32)
    ks[...] = k_ref[0].astype(F32)
    vs[...] = v_ref[0].astype(F32)

    head0 = lax.broadcasted_iota(jnp.int32, (DA_QB, LANES), 1) < 64

    def block(g, dil, row0, krow0, nk, case):
        qb = qs[pl.ds(row0, DA_QB, stride=dil), :].astype(BF16)
        kb = ks[pl.ds(krow0, nk, stride=dil), :].astype(BF16)
        vb = vs[pl.ds(krow0, nk, stride=dil), :]
        zero = jnp.zeros_like(qb)
        q2 = jnp.concatenate([jnp.where(head0, qb, zero), jnp.where(head0, zero, qb)], axis=0)
        s = lax.dot_general(q2, kb, (((1,), (1,)), ((), ())), preferred_element_type=F32)
        s = s + mask_ref[case, :, 0:nk]
        m = jnp.max(s, axis=-1, keepdims=True)
        p = jnp.exp2(s - m).astype(BF16)
        in0 = lax.broadcasted_iota(jnp.int32, (nk, LANES), 1) < 64
        v2 = jnp.concatenate([
            jnp.concatenate([jnp.where(in0, vb, 0.0), jnp.where(in0, 1.0, 0.0)], axis=1),
            jnp.concatenate([jnp.where(in0, 0.0, vb), jnp.where(in0, 0.0, 1.0)], axis=1)],
            axis=0).astype(BF16)
        p2 = jnp.concatenate([p[:DA_QB], p[DA_QB:]], axis=1)
        ol = jnp.dot(p2, v2, preferred_element_type=F32)
        l = ol[:, LANES:]
        rows = pl.ds(row0, DA_QB, stride=dil)
        acc_s[g, rows, :] = ol[:, :LANES] / l
        lse_s[g, rows, :] = jnp.where(head0, m[:DA_QB], m[DA_QB:]) + jnp.log(l) * (1.0 / LN2)

    for g, dil in enumerate(DA_BRANCH_DIL):
        n_sub = SEQ // dil
        if n_sub == DA_QB:
            def body(r, carry, g=g, dil=dil):
                block(g, dil, r, r, DA_QB, 0)
                return carry
            lax.fori_loop(0, dil, body, 0, unroll=DA_UNROLL)
        else:
            nb = n_sub // DA_QB
            nk = 2 * DA_QB

            def body(j, carry, g=g, dil=dil, nb=nb, nk=nk, n_sub=n_sub):
                r = j // nb
                q0 = (j % nb) * DA_QB
                k0 = jnp.clip(q0 - DA_HALF, 0, n_sub - nk)
                block(g, dil, r + dil * q0, r + dil * k0, nk, (q0 - k0) // DA_HALF)
                return carry
            lax.fori_loop(0, dil * nb, body, 0, unroll=DA_UNROLL)

    lse_all = jnp.maximum(jnp.maximum(lse_s[0], lse_s[1]), lse_s[2])
    num = jnp.zeros((SEQ, LANES), F32)
    den = jnp.zeros((SEQ, LANES), F32)
    for g in range(len(DA_BRANCH_DIL)):
        w = jnp.exp2(lse_s[g] - lse_all)
        num = num + w * acc_s[g]
        den = den + w
    o_ref[0] = (num / den).astype(o_ref.dtype)


def dilated_attention(proj, mask_t):
    b = proj.shape[0]
    n_hp = DA_HEADS // 2
    blk = lambda off: pl.BlockSpec((1, SEQ, LANES), lambda i, hp, off=off: (i, 0, off + hp))
    nbr = len(DA_BRANCH_DIL)
    return pl.pallas_call(
        _da_kernel, grid=(b, n_hp),
        in_specs=[blk(0), blk(n_hp), blk(2 * n_hp),
                  pl.BlockSpec(mask_t.shape, lambda i, hp: (0, 0, 0))],
        out_specs=pl.BlockSpec((1, SEQ, LANES), lambda i, hp: (i, 0, hp)),
        out_shape=jax.ShapeDtypeStruct((b, SEQ, DA_W), BF16),
        scratch_shapes=[pltpu.VMEM((SEQ, LANES), F32)] * 3 + [pltpu.VMEM((nbr, SEQ, LANES), F32)] * 2,
        compiler_params=_cparams("parallel", "parallel"), name="dilated_attention",
    )(proj, proj, proj, mask_t)


ML_SCALE = ML_HEAD_DIM ** -0.5
GATE_I_FWD, GATE_I_BWD, GATE_F_FWD, GATE_F_BWD = 0, 4, 8, 12


def _exact_ones_matmul(ones_bf16, x):
    hi = x.astype(BF16)
    r1 = x - hi.astype(F32)
    mid = r1.astype(BF16)
    lo = (r1 - mid.astype(F32)).astype(BF16)
    dot = lambda t: jnp.dot(ones_bf16, t, preferred_element_type=F32)
    return dot(hi) + dot(mid) + dot(lo)


_NT = (((1,), (1,)), ((), ()))
ML_COMBOS = 2 * ML_HEADS
ML_HEADS_PER_TRIP = 4
ROW_CUM, ROW_RMAX, ROW_LOGW, ROW_TOT, ROW_LWMAX = range(5)


def _hi_lo_rows(row):
    hi = row.astype(BF16)
    lo = (row - hi.astype(F32)).astype(BF16)
    return jnp.concatenate([hi, lo, jnp.zeros((SUBLANES - 2, row.shape[1]), BF16)], axis=0)


def _mlstm_chunk(q, k, v_t, log_d_t, rows, state):
    c_t, n_vec, m_run = state
    cum, rmax, logw, tot, lwmax = rows
    log_inter = cum + m_run
    m_t = jnp.maximum(log_inter, rmax)
    w_inter = jnp.exp(log_inter - m_t)
    s_t = lax.dot_general(k, q, _NT, preferred_element_type=F32) * ML_SCALE * jnp.exp(log_d_t - m_t)
    num = w_inter * lax.dot_general(c_t.astype(BF16), q, _NT, preferred_element_type=F32)
    num = num + jnp.dot(v_t, s_t.astype(BF16), preferred_element_type=F32)
    nq = lax.dot_general(_hi_lo_rows(n_vec), q, _NT, preferred_element_type=F32)
    den = w_inter * (nq[0:1] + nq[1:2]) + jnp.sum(s_t, axis=0, keepdims=True)
    h_t = num / jnp.maximum(jnp.abs(den), jnp.exp(-m_t))
    m_new = jnp.maximum(tot + m_run, lwmax)
    w_row = jnp.exp(logw - m_new) * ML_SCALE
    decay = jnp.exp(tot + m_run - m_new)
    vw_t = (v_t.astype(F32) * w_row).astype(BF16)
    c_new = decay * c_t + jnp.dot(vw_t, k, preferred_element_type=F32)
    nk = jnp.dot(_hi_lo_rows(w_row), k, preferred_element_type=F32)
    n_new = decay * n_vec + nk[0:1] + nk[1:2]
    return h_t, (c_new, n_new, m_new)


def _mlstm_kernel(q_ref, k_ref, v_ref, og_ref, g_ref, gb_ref, ng_ref, out_ref, vt_s, ldt_s, rows_s, hf_s, hb_s):
    L = ML_CHUNK
    assert ML_HEAD_DIM == L == LANES
    ri = lax.broadcasted_iota(jnp.int32, (L, L), 0)
    ci = lax.broadcasted_iota(jnp.int32, (L, L), 1)
    lower = ci <= ri
    upper = ci >= ri
    lower_m = jnp.where(lower, 1.0, 0.0).astype(BF16)
    upper_m = jnp.where(upper, 1.0, 0.0).astype(BF16)
    lane = lax.broadcasted_iota(jnp.int32, (L, LANES), 1)
    fwd_rows = lax.broadcasted_iota(jnp.int32, (ML_COMBOS, L), 0) < ML_HEADS

    def prep(c, carry):
        rows = pl.ds(pl.multiple_of(c * L, L), L)
        vt_s[c] = v_ref[0, rows, :].T
        g = g_ref[0, rows, :] + gb_ref[...]
        lf = jnp.minimum(g, 0.0) - jnp.log1p(jnp.exp(-jnp.abs(g)))
        cum_f = _exact_ones_matmul(lower_m, lf)
        suf_b = _exact_ones_matmul(upper_m, lf)
        colv = jnp.where(lane < GATE_F_FWD, g, jnp.where(lane < GATE_F_BWD, cum_f, suf_b))
        rowv = colv.T
        i8 = rowv[0:ML_COMBOS]
        cum8 = rowv[ML_COMBOS:2 * ML_COMBOS]
        r_all = colv - pltpu.roll(colv, LANES - ML_COMBOS, 1)
        tot8 = jnp.where(fwd_rows, jnp.broadcast_to(cum8[:, L - 1:L], (ML_COMBOS, L)),
                         jnp.broadcast_to(cum8[:, 0:1], (ML_COMBOS, L)))
        logw8 = tot8 - cum8 + i8
        lwmax8 = jnp.broadcast_to(jnp.max(logw8, axis=-1, keepdims=True), (ML_COMBOS, L))
        rmax = []
        for j in range(ML_COMBOS):
            valid = upper if j < ML_HEADS else lower
            ldt = jnp.where(valid, jnp.broadcast_to(r_all[:, j:j + 1], (L, L)) + cum8[j:j + 1, :], NEG)
            ldt_s[c * ML_COMBOS + j] = ldt
            rmax.append(jnp.max(ldt, axis=0, keepdims=True))
        rows_s[c, ROW_CUM] = cum8
        rows_s[c, ROW_RMAX] = jnp.concatenate(rmax, axis=0)
        rows_s[c, ROW_LOGW] = logw8
        rows_s[c, ROW_TOT] = tot8
        rows_s[c, ROW_LWMAX] = lwmax8
        return carry

    lax.fori_loop(0, N_CHUNKS, prep, 0)

    zero_state = (jnp.zeros((ML_HEAD_DIM, ML_HEAD_DIM), F32), jnp.zeros((1, ML_HEAD_DIM), F32),
                  jnp.zeros((1, L), F32))
    def one(c, h, backward, state):
        hl = slice(ML_HEAD_DIM * h, ML_HEAD_DIM * (h + 1))
        j = ML_HEADS + h if backward else h
        rows = pl.ds(pl.multiple_of(c * L, L), L)
        prepared = tuple(rows_s[c, k, j:j + 1, :] for k in range(5))
        h_t, state = _mlstm_chunk(q_ref[0, rows, hl], k_ref[0, rows, hl], vt_s[c, hl, :],
                                  ldt_s[c * ML_COMBOS + j], prepared, state)
        (hb_s if backward else hf_s)[h, c] = h_t
        return state

    for h0 in range(0, ML_HEADS, ML_HEADS_PER_TRIP):
        def body(c, states, h0=h0):
            out = []
            for k in range(ML_HEADS_PER_TRIP):
                out.append(one(c, h0 + k, False, states[2 * k]))
                out.append(one(N_CHUNKS - 1 - c, h0 + k, True, states[2 * k + 1]))
            return tuple(out)

        lax.fori_loop(0, N_CHUNKS, body, (zero_state,) * (2 * ML_HEADS_PER_TRIP))

    for h in range(ML_HEADS):
        hl = slice(ML_HEAD_DIM * h, ML_HEAD_DIM * (h + 1))
        gain_t = jnp.broadcast_to(ng_ref[:, hl], (L, ML_HEAD_DIM)).T
        for c in range(N_CHUNKS):
            r0 = c * L
            hh = hf_s[h, c] + hb_s[h, c]
            mu = jnp.mean(hh, axis=0, keepdims=True)
            cen = hh - mu
            var = jnp.mean(cen * cen, axis=0, keepdims=True)
            hn = (cen * lax.rsqrt(var + LN_EPS) * gain_t).T
            og = og_ref[0, r0:r0 + L, hl].astype(F32)
            out_ref[0, r0:r0 + L, hl] = (hn / (1.0 + jnp.exp(-og))).astype(out_ref.dtype)


def mlstm_mixer(proj, gates, gate_bias, norm_g):
    b = proj.shape[0]
    c0 = (3 * NA_W) // ML_W
    blk = lambda off: pl.BlockSpec((1, SEQ, ML_W), lambda i, off=off: (i, 0, off))
    gb = jnp.pad(gate_bias.astype(F32), (0, LANES - gate_bias.shape[0])).reshape(1, LANES)
    return pl.pallas_call(
        _mlstm_kernel, grid=(b,),
        in_specs=[blk(c0), blk(c0 + 1), blk(c0 + 2), blk(c0 + 3),
                  pl.BlockSpec((1, SEQ, LANES), lambda i: (i, 0, 0)),
                  pl.BlockSpec((1, LANES), lambda i: (0, 0)),
                  pl.BlockSpec((1, ML_W), lambda i: (0, 0))],
        out_specs=pl.BlockSpec((1, SEQ, ML_W), lambda i: (i, 0, 0)),
        out_shape=jax.ShapeDtypeStruct((b, SEQ, ML_W), BF16),
        scratch_shapes=[pltpu.VMEM((N_CHUNKS, ML_W, ML_CHUNK), BF16),
                        pltpu.VMEM((N_CHUNKS * ML_COMBOS, ML_CHUNK, ML_CHUNK), F32),
                        pltpu.VMEM((N_CHUNKS, 5, ML_COMBOS, ML_CHUNK), F32),
                        pltpu.VMEM((ML_HEADS, N_CHUNKS, ML_HEAD_DIM, ML_CHUNK), F32),
                        pltpu.VMEM((ML_HEADS, N_CHUNKS, ML_HEAD_DIM, ML_CHUNK), F32)],
        compiler_params=_cparams("parallel"), name="mlstm_mixer",
    )(proj, proj, proj, proj, gates, gb, norm_g.reshape(1, ML_W).astype(F32))


def _router_kernel(x_ref, w_ref, aff_ref):
    x = _load_token_tiles(x_ref, x_ref.shape[0] // TILE_ROWS)
    w = w_ref[...]
    x_hi = x.astype(BF16)
    x_lo = (x - x_hi.astype(F32)).astype(BF16)
    w_hi = w.astype(BF16)
    w_lo = (w - w_hi.astype(F32)).astype(BF16)
    dot = lambda a, b: jnp.dot(a, b, preferred_element_type=F32)
    logits_t = dot(x_hi, w_hi) + dot(x_hi, w_lo) + dot(x_lo, w_hi)
    logits = logits_t.T[0:N_EXPERTS, :]
    z = jnp.exp(logits - jnp.max(logits, axis=0, keepdims=True))
    aff_ref[...] = z / jnp.sum(z, axis=0, keepdims=True)


def router_affinities(x_tiles, w_router, tm=1024):
    n, d = x_tiles.shape[0] // TILE_ROWS, D_MODEL
    w_pad = jnp.pad(w_router.astype(F32), ((0, 0), (0, LANES - N_EXPERTS)))
    return pl.pallas_call(
        _router_kernel, grid=(n // tm,),
        in_specs=[pl.BlockSpec((tm * TILE_ROWS, LANES), lambda i: (i, 0)),
                  pl.BlockSpec((d, LANES), lambda i: (0, 0))],
        out_specs=pl.BlockSpec((N_EXPERTS, tm), lambda i: (0, i)),
        out_shape=jax.ShapeDtypeStruct((N_EXPERTS, n), F32),
        compiler_params=_cparams("parallel"), name="router_affinities")(x_tiles, w_pad)


def _tri_matrices(r):
    li = lax.broadcasted_iota(jnp.int32, (LANES, LANES), 0)
    lj = lax.broadcasted_iota(jnp.int32, (LANES, LANES), 1)
    tri_u = jnp.where(li <= lj, 1.0, 0.0).astype(BF16)
    ri = lax.broadcasted_iota(jnp.int32, (r, r), 0)
    rj = lax.broadcasted_iota(jnp.int32, (r, r), 1)
    tri_l = jnp.where(rj < ri, 1.0, 0.0).astype(BF16)
    return tri_u, tri_l


def _prefix_counts(mask, tri_u, tri_l):
    r = mask.shape[0]
    within = jnp.dot(mask.astype(BF16), tri_u, preferred_element_type=F32)
    rowtot = within[:, LANES - 1:LANES]
    hi = jnp.floor(rowtot * (1.0 / 16.0))
    lo = rowtot - 16.0 * hi
    hi_b = jnp.broadcast_to(hi, (r, LANES)).astype(BF16)
    lo_b = jnp.broadcast_to(lo, (r, LANES)).astype(BF16)
    rowoff = 16.0 * jnp.dot(tri_l, hi_b, preferred_element_type=F32) + jnp.dot(
        tri_l, lo_b, preferred_element_type=F32)
    return within - mask + rowoff, within, rowoff, rowtot


def _select_kernel(aff_ref, sel_ref, *, cap):
    n_e, r, _ = aff_ref.shape
    tri_u, tri_l = _tri_matrices(r)

    def bits(k):
        return pltpu.bitcast(aff_ref[k], jnp.int32)

    def count(m):
        c = jnp.sum(jnp.where(m, 1.0, 0.0), axis=0, keepdims=True)
        return jnp.sum(c, axis=1, keepdims=True)

    def bisect(i, prefixes):
        bit = jnp.left_shift(jnp.int32(1), 30 - i)
        return tuple(jnp.where(count(bits(k) >= (p | bit)) >= cap, p | bit, p)
                     for k, p in enumerate(prefixes))

    thr = lax.fori_loop(0, 31, bisect, (jnp.zeros((1, 1), jnp.int32),) * n_e)
    for k in range(n_e):
        gt = bits(k) > thr[k]
        eq = bits(k) == thr[k]
        need = cap - count(gt)
        rank_eq, _, _, _ = _prefix_counts(jnp.where(eq, 1.0, 0.0), tri_u, tri_l)
        sel_ref[k] = jnp.where(gt | (eq & (rank_eq < need)), 1.0, 0.0)


def select_tokens(aff3, cap):
    e, r, _ = aff3.shape
    blk = pl.BlockSpec((e, r, LANES), lambda i: (0, 0, 0))
    return pl.pallas_call(
        functools.partial(_select_kernel, cap=cap), grid=(1,), in_specs=[blk], out_specs=blk,
        out_shape=jax.ShapeDtypeStruct((e, r, LANES), F32),
        compiler_params=_cparams("arbitrary"), name="select_tokens")(aff3)


def _lists_kernel(sel_ref, aff_ref, idx_ref, dst_ref, gate_ref, ts_ref, ts_s, er_s, *, cap, st):
    e = pl.program_id(0)
    r = sel_ref.shape[1]
    tri_u, tri_l = _tri_matrices(r)

    @pl.when(e == 0)
    def _():
        cnt = sel_ref[0]
        for k in range(1, N_EXPERTS):
            cnt = cnt + sel_ref[k]
        ts, _, _, _ = _prefix_counts(cnt, tri_u, tri_l)
        ts_s[...] = ts
        ts_ref[...] = ts
        er_s[...] = jnp.zeros_like(er_s)

    sel = sel_ref[e]
    _, within, rowoff, rowtot = _prefix_counts(sel, tri_u, tri_l)
    dst = ts_s[...] + er_s[...]
    er_s[...] = er_s[...] + sel

    d2 = jnp.floor(dst * (1.0 / 65536.0))
    rem = dst - 65536.0 * d2
    d1 = jnp.floor(rem * (1.0 / 256.0))
    d0 = rem - 256.0 * d1
    aff = aff_ref[0]
    a_hi = aff.astype(BF16)
    a_r1 = aff - a_hi.astype(F32)
    a_mid = a_r1.astype(BF16)
    a_lo = (a_r1 - a_mid.astype(F32)).astype(BF16)
    rhs = jnp.concatenate([within.astype(BF16), d0.astype(BF16), d1.astype(BF16), d2.astype(BF16),
                           a_hi, a_mid, a_lo], axis=1)
    rowoff_row = rowoff.T[0:1, :]
    rowend_row = rowoff_row + jnp.broadcast_to(rowtot, (r, LANES)).T[0:1, :]
    rho_row = lax.broadcasted_iota(jnp.int32, (1, r), 1).astype(F32)
    lane = lax.broadcasted_iota(jnp.int32, (st, LANES), 1).astype(F32)
    eye = lax.broadcasted_iota(jnp.int32, (LANES, LANES), 0) == lax.broadcasted_iota(
        jnp.int32, (LANES, LANES), 1)

    for t in range(cap // st):
        s_col = (t * st + lax.broadcasted_iota(jnp.int32, (st, 1), 0)).astype(F32)
        in_row = (rowoff_row <= s_col) & (s_col < rowend_row)
        got = jnp.dot(jnp.where(in_row, 1.0, 0.0).astype(BF16), rhs, preferred_element_type=F32)
        base = jnp.sum(jnp.where(in_row, rowoff_row, 0.0), axis=1, keepdims=True)
        rho = jnp.sum(jnp.where(in_row, rho_row, 0.0), axis=1, keepdims=True)
        local = s_col - base
        lam = jnp.sum(jnp.where(got[:, 0:LANES] <= local, 1.0, 0.0), axis=1, keepdims=True)
        plane = lambda k: got[:, k * LANES:(k + 1) * LANES]
        pair = plane(1) + 256.0 * plane(2) + 65536.0 * plane(3)
        at_lam = lane == lam
        dval = jnp.sum(jnp.where(at_lam, pair, 0.0), axis=1, keepdims=True)
        gate_ref[0, t * st:(t + 1) * st, :] = jnp.sum(
            jnp.where(at_lam, plane(4) + plane(5) + plane(6), 0.0), axis=1, keepdims=True)
        ival = rho * float(LANES) + lam
        for j in range(st // LANES):
            seg = slice(j * LANES, (j + 1) * LANES)
            row = t * (st // LANES) + j
            idx_ref[0, row:row + 1, :] = jnp.sum(
                jnp.where(eye, ival[seg], 0.0), axis=0, keepdims=True).astype(jnp.int32)
            dst_ref[0, row:row + 1, :] = jnp.sum(
                jnp.where(eye, dval[seg], 0.0), axis=0, keepdims=True).astype(jnp.int32)


def build_lists(sel3, aff3, cap):
    e, r, _ = sel3.shape
    st = min(512, cap)
    lst = pl.BlockSpec((1, cap // LANES, LANES), lambda i: (i, 0, 0))
    return pl.pallas_call(
        functools.partial(_lists_kernel, cap=cap, st=st), grid=(e,),
        in_specs=[pl.BlockSpec((e, r, LANES), lambda i: (0, 0, 0)),
                  pl.BlockSpec((1, r, LANES), lambda i: (i, 0, 0))],
        out_specs=[lst, lst, pl.BlockSpec((1, cap, 1), lambda i: (i, 0, 0)),
                   pl.BlockSpec((r, LANES), lambda i: (0, 0))],
        out_shape=[jax.ShapeDtypeStruct((e, cap // LANES, LANES), jnp.int32),
                   jax.ShapeDtypeStruct((e, cap // LANES, LANES), jnp.int32),
                   jax.ShapeDtypeStruct((e, cap, 1), F32), jax.ShapeDtypeStruct((r, LANES), F32)],
        scratch_shapes=[pltpu.VMEM((r, LANES), F32), pltpu.VMEM((r, LANES), F32)],
        compiler_params=_cparams("arbitrary"), name="build_lists")(sel3, aff3)


FFN_CHUNKS = tuple((f, min(f + 256, D_FF)) for f in range(0, D_FF, 256))


def _ffn_kernel(idx_first, idx_next_a, idx_next_b, dst_prev_a, dst_prev_b, dst_last,
                x_hbm, gate_a, gate_b, wg_ref, wu_ref, wd_ref, z_hbm,
                xbuf0, xbuf1, ybuf0, ybuf1, acc, gsem, ssem, *, tm, n_grid):
    g = pl.program_id(0)

    def token_tile(t):
        if isinstance(t, int):
            return pl.ds(t * TILE_ROWS, TILE_ROWS)
        return pl.ds(pl.multiple_of(t * TILE_ROWS, TILE_ROWS), TILE_ROWS)

    def gather_row(ids, i, buf, sem):
        pltpu.make_async_copy(x_hbm.at[token_tile(ids[0, 0, i]), :], buf.at[token_tile(i), :],
                              sem).start(priority=1)

    def scatter_row(dsts, i, buf, sem):
        pltpu.make_async_copy(buf.at[token_tile(i), :], z_hbm.at[token_tile(dsts[0, 0, i]), :],
                              sem).start(priority=0)

    def wait_rows(buf, sem):
        pltpu.make_async_copy(x_hbm.at[pl.ds(0, tm * TILE_ROWS), :], buf, sem).wait()

    @pl.when(g == 0)
    def _():
        ybuf1[...] = jnp.zeros_like(ybuf1)

        def first(i, carry):
            gather_row(idx_first, i, xbuf0, gsem.at[0])
            return carry
        lax.fori_loop(0, tm, first, 0, unroll=8)

    def tile(xcur, gcur, xnext, gnext, idx_next, yprev, sprev, dst_prev):
        for i in range(tm):
            gather_row(idx_next, i, xnext, gnext)
            scatter_row(dst_prev, i, yprev, sprev)

        wait_rows(xcur, gcur)
        xb = _load_token_tiles(xcur, tm).astype(BF16)
        for c, (f0, f1) in enumerate(FFN_CHUNKS):
            gate = jnp.dot(xb, wg_ref[0, :, f0:f1], preferred_element_type=F32)
            up = jnp.dot(xb, wu_ref[0, :, f0:f1], preferred_element_type=F32)
            h = (gate / (1.0 + jnp.exp(-gate)) * up).astype(BF16)
            part = jnp.dot(h, wd_ref[0, f0:f1, :], preferred_element_type=F32)
            if c == 0:
                acc[...] = part
            else:
                acc[...] += part

    tile(xbuf0, gsem.at[0], xbuf1, gsem.at[1], idx_next_a, ybuf1, ssem.at[1], dst_prev_a)

    @pl.when(g >= 1)
    def _():
        wait_rows(ybuf0, ssem.at[0])
    _store_token_tiles(ybuf0, acc[...] * gate_a[0])

    tile(xbuf1, gsem.at[1], xbuf0, gsem.at[0], idx_next_b, ybuf0, ssem.at[0], dst_prev_b)
    wait_rows(ybuf1, ssem.at[1])
    _store_token_tiles(ybuf1, acc[...] * gate_b[0])

    @pl.when(g == n_grid - 1)
    def _():
        def last(i, carry):
            scatter_row(dst_last, i, ybuf1, ssem.at[1])
            return carry
        lax.fori_loop(0, tm, last, 0, unroll=8)
        wait_rows(ybuf1, ssem.at[1])
        wait_rows(ybuf0, ssem.at[0])
        wait_rows(xbuf0, gsem.at[0])


def expert_ffn(x_tiles, idx, dst, gate, w_gate, w_up, w_down, tm=512):
    d = D_MODEL
    e, cap = idx.shape
    tm = min(tm, cap // 2)
    nt = cap // tm
    assert nt % 2 == 0
    n_tiles = e * nt
    n_grid = n_tiles // 2
    idx3 = idx.reshape(n_tiles, 1, tm)
    gate3 = gate.reshape(n_tiles, tm, 1)
    spare =(e * cap + jnp.arange(tm, dtype=jnp.int32)).reshape(1, 1, tm)
    dst3 = jnp.concatenate([spare, dst.reshape(n_tiles, 1, tm)])
    smem = lambda imap: pl.BlockSpec((1, 1, tm), imap, memory_space=pltpu.SMEM)
    wspec = lambda w: pl.BlockSpec((1,) + w.shape[1:], lambda i: ((2 * i) // nt, 0, 0))
    return pl.pallas_call(
        functools.partial(_ffn_kernel, tm=tm, n_grid=n_grid), grid=(n_grid,),
        in_specs=[smem(lambda i: (0, 0, 0)), smem(lambda i: (2 * i + 1, 0, 0)),
                  smem(lambda i: (jnp.minimum(2 * i + 2, n_tiles - 1), 0, 0)),
                  smem(lambda i: (2 * i, 0, 0)), smem(lambda i: (2 * i + 1, 0, 0)),
                  smem(lambda i: (n_tiles, 0, 0)),
                  pl.BlockSpec(memory_space=pl.ANY),
                  pl.BlockSpec((1, tm, 1), lambda i: (2 * i, 0, 0)),
                  pl.BlockSpec((1, tm, 1), lambda i: (2 * i + 1, 0, 0)),
                  wspec(w_gate), wspec(w_up), wspec(w_down)],
        out_specs=pl.BlockSpec(memory_space=pl.ANY),
        out_shape=jax.ShapeDtypeStruct(((e * cap + tm) * TILE_ROWS, LANES), F32),
        scratch_shapes=[pltpu.VMEM((tm * TILE_ROWS, LANES), F32)] * 4 + [pltpu.VMEM((tm, d), F32)]
        + [pltpu.SemaphoreType.DMA((2,)), pltpu.SemaphoreType.DMA((2,))],
        compiler_params=_cparams("arbitrary"), name="expert_ffn",
    )(idx3, idx3, idx3, dst3, dst3, dst3, x_tiles, gate3, gate3, w_gate, w_up, w_down)


COMBINE_ZB = 512
COMBINE_SLOTS = 4


def _combine_kernel(ts_ref, x_ref, run_ref, g_ref, b_ref, z_hbm, o_ref, zbuf, sem, used, *, z_rows, n_tiles):
    i = pl.program_id(0)
    tt = x_ref.shape[0] // TILE_ROWS

    def chunk_rows(t, c):
        lo = ts_ref[t] + c * COMBINE_ZB
        return lo, jnp.minimum(lo, z_rows - COMBINE_ZB)

    def piece_copy(start, j, slot):
        return pltpu.make_async_copy(z_hbm.at[pl.ds(start, COMBINE_ZB), j, :], zbuf.at[slot, j], sem.at[slot])

    def chunks_of(t):
        return jnp.maximum((ts_ref[t + 1] - ts_ref[t] + COMBINE_ZB - 1) // COMBINE_ZB, 1)

    def fetch_next():
        t = used[2]

        @pl.when(t < n_tiles)
        def _():
            c = used[3]
            slot = used[1] % COMBINE_SLOTS
            _, start = chunk_rows(t, c)
            for j in range(TILE_ROWS):
                piece_copy(start, j, slot).start()
            used[1] = used[1] + 1
            last = c + 1 >= chunks_of(t)
            used[2] = jnp.where(last, t + 1, t)
            used[3] = jnp.where(last, 0, c + 1)

    @pl.when(i == 0)
    def _():
        for k in range(4):
            used[k] = 0
        for _ in range(COMBINE_SLOTS - 1):
            fetch_next()

    base = used[0]
    n_chunks = chunks_of(i)
    run_lo = jnp.broadcast_to(run_ref[:, 0:1], (tt, COMBINE_ZB))
    run_hi = jnp.broadcast_to(run_ref[:, 1:2], (tt, COMBINE_ZB))
    col = lax.broadcasted_iota(jnp.int32, (1, COMBINE_ZB), 1).astype(F32)

    def chunk(c, acc):
        slot = (base + c) % COMBINE_SLOTS
        lo, start = chunk_rows(i, c)
        pair = col + start.astype(F32)
        pair = jnp.where(pair >= lo.astype(F32), pair, -1.0)
        a = jnp.where((run_lo <= pair) & (pair < run_hi), 1.0, 0.0).astype(BF16)
        for j in range(TILE_ROWS):
            piece_copy(0, j, slot).wait()
        rows = jnp.concatenate([zbuf[slot, j] for j in range(TILE_ROWS)], axis=1).astype(BF16)
        acc = acc + jnp.dot(a, rows, preferred_element_type=F32)
        fetch_next()
        return acc

    ffn = lax.fori_loop(0, n_chunks, chunk, jnp.zeros((tt, D_MODEL), F32))
    used[0] = base + n_chunks
    o_ref[...] = _layer_norm_rows(DN_ALPHA * _load_token_tiles(x_ref, tt) + ffn, g_ref[...], b_ref[...])


def combine_ln(x_tiles, z, tile_start, runs, g, b, tt=256):
    n, d = x_tiles.shape[0] // TILE_ROWS, D_MODEL
    n_tiles = n // tt
    row = lambda i, ts: (i, 0)
    fixed = lambda i, ts: (0, 0)
    grid_spec = pltpu.PrefetchScalarGridSpec(
        num_scalar_prefetch=1, grid=(n_tiles,),
        in_specs=[pl.BlockSpec((tt * TILE_ROWS, LANES), row), pl.BlockSpec((tt, 2), row),
                  pl.BlockSpec((1, d), fixed),
                  pl.BlockSpec((1, d), fixed), pl.BlockSpec(memory_space=pl.ANY)],
        out_specs=pl.BlockSpec((tt, d), row),
        scratch_shapes=[pltpu.VMEM((COMBINE_SLOTS, TILE_ROWS, COMBINE_ZB, LANES), F32),
                        pltpu.SemaphoreType.DMA((COMBINE_SLOTS,)), pltpu.SMEM((4,), jnp.int32)])
    z_rows = z.shape[0] // TILE_ROWS
    return pl.pallas_call(
        functools.partial(_combine_kernel, z_rows=z_rows, n_tiles=n_tiles), grid_spec=grid_spec,
        out_shape=jax.ShapeDtypeStruct((n, d), F32),
        compiler_params=_cparams("arbitrary"), name="combine_ln",
    )(tile_start, x_tiles, runs, g.reshape(1, d), b.reshape(1, d), z.reshape(z_rows, TILE_ROWS, LANES))


def moe_layer(x_tiles, w_router, w_gate, w_up, w_down, g, b, tt=256):
    n = x_tiles.shape[0] // TILE_ROWS
    r = n // LANES
    cap = 2 * n // N_EXPERTS
    aff3 = router_affinities(x_tiles, w_router).reshape(N_EXPERTS, r, LANES)
    sel3 = select_tokens(aff3, cap)
    idx, dst, gate, ts = build_lists(sel3, aff3, cap)
    z = expert_ffn(x_tiles, idx.reshape(N_EXPERTS, cap), dst.reshape(N_EXPERTS, cap), gate, w_gate, w_up, w_down)
    ts_ext = jnp.concatenate([ts.reshape(n), jnp.full((1,), N_EXPERTS * cap, F32)])
    tile_start = ts_ext[::tt].astype(jnp.int32)
    runs = jnp.stack([ts_ext[:-1], ts_ext[1:]], axis=1)
    return combine_ln(x_tiles, z, tile_start, runs, g, b, tt=tt)


def kernel(x_prompt, x_sample, even_w_in, ml_gate_bias, na_rpb, ml_norm_g, even_w_out, da_w_in, da_w_out,
           ln_mix_g, ln_mix_b, ec_router, ec_w_gate, ec_w_up, ec_w_down, ln_ffn_g, ln_ffn_b):
    n_mix_cols = 3 * NA_W + 4 * ML_W
    n_gates = 4 * ML_HEADS
    w_even = even_w_in[0][:, :n_mix_cols].astype(BF16)
    w_gates = jnp.pad(even_w_in[0][:, n_mix_cols:], ((0, 0), (0, LANES - n_gates))).astype(BF16)
    w_out_a = even_w_out[0][:NA_W].astype(BF16)
    w_out_b = even_w_out[0][NA_W:].astype(BF16)
    w_odd = da_w_in[0].astype(BF16)
    w_odd_out = da_w_out[0].astype(BF16)
    tbl = na_bias_table(na_rpb[0])
    rope_t = rope_tables()
    mask_t = da_mask_table()
    moe_w = [(ec_router[l], ec_w_gate[l].astype(BF16), ec_w_up[l].astype(BF16), ec_w_down[l].astype(BF16),
              ln_ffn_g[l], ln_ffn_b[l]) for l in range(DEPTH)]

    def trunk(x):
        b = x.shape[0]
        xt = x.reshape(b * SEQ, D_MODEL)
        proj, gates = in_projection(xt, w_even, w_gates)
        proj = proj.reshape(b, SEQ, -1)
        ya = neighbourhood_attention(proj, tbl)
        yb = mlstm_mixer(proj, gates.reshape(b, SEQ, LANES), ml_gate_bias[0], ml_norm_g[0])
        xt_tiles = out_projection_ln(xt, [ya.reshape(b * SEQ, NA_W), yb.reshape(b * SEQ, ML_W)],
                                     [w_out_a, w_out_b], ln_mix_g[0], ln_mix_b[0])
        xt = moe_layer(xt_tiles, *moe_w[0])
        proj = in_projection_rope(xt, w_odd, rope_t, DA_W).reshape(b, SEQ, -1)
        yc = dilated_attention(proj, mask_t)
        xt_tiles = out_projection_ln(xt, [yc.reshape(b * SEQ, DA_W)], [w_odd_out], ln_mix_g[1], ln_mix_b[1])
        xt = moe_layer(xt_tiles, *moe_w[1])
        return xt.reshape(b, SEQ, D_MODEL)

    return trunk(x_prompt), trunk(x_sample)
```

```python
import functools

import numpy as np
import jax
import jax.numpy as jnp
from jax import lax
from jax.experimental import pallas as pl
from jax.experimental.pallas import tpu as pltpu

F32 = jnp.float32
BF16 = jnp.bfloat16

D_MODEL = 1024
SEQ = 2048
GRID_W = 64
GRID_ROWS = SEQ // GRID_W
NA_HEADS = 8
NA_W = 512
NA_WIN_R = 8
NA_WIN_C = 16
ML_HEADS = 4
ML_HEAD_DIM = 128
ML_W = 512
ML_CHUNK = 128
N_CHUNKS = SEQ // ML_CHUNK
DA_HEADS = 16
DA_W = 1024
DA_HALF = 64
ROPE_THETA = 10000.0
N_EXPERTS = 16
D_FF = 1408
LN_EPS = 1e-5
DEPTH = 2
DN_ALPHA = (2 * DEPTH) ** 0.25
LANES = 128
ATT_HEAD_DIM = 64
ROT_HALF = ATT_HEAD_DIM // 2
NEG = -1e30
LOG2E = 1.4426950408889634
LN2 = 0.6931471805599453
VMEM_LIMIT = 56 * 1024 * 1024


def _cparams(*sem):
    return pltpu.CompilerParams(dimension_semantics=sem, vmem_limit_bytes=VMEM_LIMIT)


def _inproj_gates_kernel(x_ref, w_ref, wg_ref, o_ref, g_ref, *, n_chunk):
    xb = x_ref[...].astype(BF16)
    for c in range(0, w_ref.shape[1], n_chunk):
        o_ref[:, c:c + n_chunk] = jnp.dot(
            xb, w_ref[:, c:c + n_chunk], preferred_element_type=F32).astype(o_ref.dtype)
    g_ref[...] = jnp.dot(xb, wg_ref[...], preferred_element_type=F32)


def _rotate_half_pairs(x, cos, sin_signed):
    first_half = lax.broadcasted_iota(jnp.int32, x.shape, 1) % ATT_HEAD_DIM < ROT_HALF
    swapped = jnp.where(first_half, pltpu.roll(x, LANES - ROT_HALF, 1), pltpu.roll(x, ROT_HALF, 1))
    return x * cos + swapped * sin_signed


def _inproj_rope_kernel(x_ref, w_ref, rope_ref, o_ref, *, n_chunk, width):
    xb = x_ref[...].astype(BF16)
    for c in range(0, w_ref.shape[1], n_chunk):
        res = jnp.dot(xb, w_ref[:, c:c + n_chunk], preferred_element_type=F32)
        if c < 2 * width:
            t = 0 if c < width else 2
            res = jnp.concatenate(
                [_rotate_half_pairs(res[:, l:l + LANES], rope_ref[t], rope_ref[t + 1])
                 for l in range(0, n_chunk, LANES)], axis=1)
        o_ref[:, c:c + n_chunk] = res.astype(o_ref.dtype)


def in_projection_rope(x, w, rope_tables, width, tm=512, n_chunk=512):
    m, k = x.shape
    n = w.shape[1]
    assert width % n_chunk == 0 and SEQ % tm == 0
    return pl.pallas_call(
        functools.partial(_inproj_rope_kernel, n_chunk=n_chunk, width=width),
        grid=(m // tm,),
        in_specs=[pl.BlockSpec((tm, k), lambda i: (i, 0)), pl.BlockSpec((k, n), lambda i: (0, 0)),
                  pl.BlockSpec((4, tm, LANES), lambda i: (0, i % (SEQ // tm), 0))],
        out_specs=pl.BlockSpec((tm, n), lambda i: (i, 0)),
        out_shape=jax.ShapeDtypeStruct((m, n), BF16),
        compiler_params=_cparams("parallel"), name="in_projection_rope")(x, w, rope_tables)


def in_projection_gates(x, w, wg, tm=512, n_chunk=512):
    m, k = x.shape
    n = w.shape[1]
    x_spec = pl.BlockSpec((tm, k), lambda i: (i, 0))
    w_spec = pl.BlockSpec((k, n), lambda i: (0, 0))
    o_spec = pl.BlockSpec((tm, n), lambda i: (i, 0))
    return pl.pallas_call(
        functools.partial(_inproj_gates_kernel, n_chunk=n_chunk),
        grid=(m // tm,),
        in_specs=[x_spec, w_spec, pl.BlockSpec((k, LANES), lambda i: (0, 0))],
        out_specs=[o_spec, pl.BlockSpec((tm, LANES), lambda i: (i, 0))],
        out_shape=[jax.ShapeDtypeStruct((m, n), BF16), jax.ShapeDtypeStruct((m, LANES), F32)],
        compiler_params=_cparams("parallel"), name="in_projection_gates")(x, w, wg)


def _layer_norm_rows(acc, g, b):
    mu = jnp.mean(acc, axis=-1, keepdims=True)
    cen = acc - mu
    var = jnp.mean(cen * cen, axis=-1, keepdims=True)
    return cen * lax.rsqrt(var + LN_EPS) * g + b


def _outproj_ln_kernel(*refs, n_mix):
    x_ref = refs[0]
    mix_refs = refs[1:1 + n_mix]
    w_refs = refs[1 + n_mix:1 + 2 * n_mix]
    g_ref, b_ref, ot_ref = refs[1 + 2 * n_mix:]
    acc = DN_ALPHA * x_ref[...]
    for m_ref, w_ref in zip(mix_refs, w_refs):
        acc = acc + jnp.dot(m_ref[...], w_ref[...], preferred_element_type=F32)
    _store_token_tiles(ot_ref, _layer_norm_rows(acc, g_ref[...], b_ref[...]))


SUBLANES = 8
TILE_ROWS = D_MODEL // LANES


def _store_token_tiles(ref, rows):
    n = rows.shape[0]
    for j in range(TILE_ROWS):
        ref[pl.ds(j, n, stride=TILE_ROWS), :] = rows[:, j * LANES:(j + 1) * LANES]


def _load_token_tiles(ref, n):
    return jnp.concatenate([ref[pl.ds(j, n, stride=TILE_ROWS), :] for j in range(TILE_ROWS)], axis=1)


def out_projection_ln(x, mixes, ws, g, b, tm=512):
    m, d = x.shape
    n_mix = len(mixes)
    row = lambda i: (i, 0)
    fixed = lambda i: (0, 0)
    in_specs = [pl.BlockSpec((tm, d), row)]
    in_specs += [pl.BlockSpec((tm, mx.shape[1]), row) for mx in mixes]
    in_specs += [pl.BlockSpec(w.shape, fixed) for w in ws]
    in_specs += [pl.BlockSpec((1, d), fixed), pl.BlockSpec((1, d), fixed)]
    return pl.pallas_call(
        functools.partial(_outproj_ln_kernel, n_mix=n_mix),
        grid=(m // tm,), in_specs=in_specs,
        out_specs=pl.BlockSpec((tm * TILE_ROWS, LANES), row),
        out_shape=jax.ShapeDtypeStruct((m * TILE_ROWS, LANES), F32),
        compiler_params=_cparams("parallel"), name="out_projection_ln",
    )(x, *mixes, *ws, g.reshape(1, d), b.reshape(1, d))


NA_PAIR_ROWS = 10
NA_PAIR_KEYS = NA_PAIR_ROWS * GRID_W
NA_PAIR_CASES = 5
NA_N_PAIRS = GRID_ROWS // 2


def _na_pair_window(p):
    start = jnp.minimum(jnp.clip(2 * p - NA_WIN_R // 2, 0, GRID_ROWS - NA_WIN_R), GRID_ROWS - NA_PAIR_ROWS)
    case = jnp.minimum(p, 2) + jnp.maximum(p - (NA_N_PAIRS - 3), 0)
    return start, case


def na_bias_table(rpb):
    j = np.arange(GRID_W)
    kc = np.arange(GRID_W)
    win_c0 = np.clip(j - NA_WIN_C // 2, 0, GRID_W - NA_WIN_C)
    valid = (kc[None, :] >= win_c0[:, None]) & (kc[None, :] < win_c0[:, None] + NA_WIN_C)
    dc = np.clip(kc[None, :] - j[:, None] + NA_WIN_C - 1, 0, 2 * NA_WIN_C - 2)
    n_dc = 2 * NA_WIN_C - 1
    pick = jnp.asarray(dc[:, :, None] == np.arange(n_dc)[None, None, :], F32)
    rows = jnp.einsum('hrd,jkd->hrjk', rpb.astype(F32), pick, precision=lax.Precision.HIGHEST)
    rows = jnp.where(valid[None, None], rows * LOG2E, NEG)
    masked = jnp.full_like(rows[:, 0], NEG)
    representative = (0, 1, 2, NA_N_PAIRS - 2, NA_N_PAIRS - 1)
    cases = []
    for p in representative:
        start = min(max(2 * p - NA_WIN_R // 2, 0), GRID_ROWS - NA_WIN_R, GRID_ROWS - NA_PAIR_ROWS)
        per_row = []
        for r in (2 * p, 2 * p + 1):
            rs = min(max(r - NA_WIN_R // 2, 0), GRID_ROWS - NA_WIN_R)
            per_row.append(jnp.stack(
                [rows[:, start + w - r + NA_WIN_R - 1] if rs <= start + w < rs + NA_WIN_R else masked
                 for w in range(NA_PAIR_ROWS)], axis=1))
        cases.append(jnp.stack(per_row, axis=1))
    t = jnp.stack(cases, axis=1)
    t = t.transpose(0, 1, 2, 4, 3, 5).reshape(NA_HEADS // 2, 2, NA_PAIR_CASES, 2 * GRID_W, NA_PAIR_KEYS)
    return t.transpose(0, 2, 1, 3, 4).reshape(NA_HEADS // 2, NA_PAIR_CASES, 4 * GRID_W, NA_PAIR_KEYS)


NA_SCORE_SCALE = ATT_HEAD_DIM ** -0.5 * LOG2E
NA_UNROLL = 8


def _na_kernel(q_ref, k_ref, v_ref, tbl_ref, o_ref, v2a, v2b):
    nq = 2 * GRID_W
    head0 = lax.broadcasted_iota(jnp.int32, (nq, LANES), 1) < ATT_HEAD_DIM

    in0 = lax.broadcasted_iota(jnp.int32, (SEQ, LANES), 1) < ATT_HEAD_DIM
    vf = v_ref[0].astype(F32)
    v2a[...] = jnp.concatenate([jnp.where(in0, vf, 0.0), jnp.where(in0, 1.0, 0.0)], axis=1).astype(BF16)
    v2b[...] = jnp.concatenate([jnp.where(in0, 0.0, vf), jnp.where(in0, 0.0, 1.0)], axis=1).astype(BF16)

    def pair(p, carry):
        start, case = _na_pair_window(p)
        rows = pl.ds(pl.multiple_of(p * nq, nq), nq)
        q = q_ref[0, rows, :]
        zero = jnp.zeros_like(q)
        q2 = jnp.concatenate([jnp.where(head0, q, zero), jnp.where(head0, zero, q)], axis=0)
        win = pl.ds(pl.multiple_of(start * GRID_W, GRID_W), NA_PAIR_KEYS)
        s = lax.dot_general(q2, k_ref[0, win, :], (((1,), (1,)), ((), ())), preferred_element_type=F32)
        s = s * NA_SCORE_SCALE + tbl_ref[0, case]
        pr = jnp.exp2(s - jnp.max(s, axis=-1, keepdims=True)).astype(BF16)
        p2 = jnp.concatenate([pr[:nq], pr[nq:]], axis=1)
        ol = jnp.dot(p2, jnp.concatenate([v2a[win, :], v2b[win, :]], axis=0), preferred_element_type=F32)
        o_ref[0, rows, :] = (ol[:, :LANES] / ol[:, LANES:]).astype(o_ref.dtype)
        return carry

    lax.fori_loop(0, NA_N_PAIRS, pair, 0, unroll=NA_UNROLL)


def neighbourhood_attention(proj, tbl):
    b = proj.shape[0]
    n_hp = NA_HEADS // 2
    blk = lambda off: pl.BlockSpec((1, SEQ, LANES), lambda hp, i, off=off: (i, 0, off + hp))
    return pl.pallas_call(
        _na_kernel, grid=(n_hp, b),
        in_specs=[blk(0), blk(n_hp), blk(2 * n_hp),
                  pl.BlockSpec((1, NA_PAIR_CASES, 4 * GRID_W, NA_PAIR_KEYS), lambda hp, i: (hp, 0, 0, 0))],
        out_specs=pl.BlockSpec((1, SEQ, LANES), lambda hp, i: (i, 0, hp)),
        out_shape=jax.ShapeDtypeStruct((b, SEQ, NA_W), BF16),
        scratch_shapes=[pltpu.VMEM((SEQ, 2 * LANES), BF16)] * 2,
        compiler_params=_cparams("parallel", "parallel"), name="neighbourhood_attention",
    )(proj, proj, proj, tbl)


DA_BRANCH_DIL = (1, 4, 16)
DA_QB = 128


DA_Q_SCALE = ATT_HEAD_DIM ** -0.5 * LOG2E


def rope_tables():
    lane = np.arange(LANES) % ATT_HEAD_DIM
    inv = ROPE_THETA ** (-(2.0 * (lane % ROT_HALF)) / ATT_HEAD_DIM)
    ang = jnp.arange(SEQ, dtype=F32)[:, None] * jnp.asarray(inv, F32)[None, :]
    sign = jnp.asarray(np.where(lane < ROT_HALF, -1.0, 1.0), F32)[None, :]
    cos, sin = jnp.cos(ang), jnp.sin(ang) * sign
    return jnp.stack([cos * DA_Q_SCALE, sin * DA_Q_SCALE, cos, sin])
DA_MASK_CASES = 3
DA_UNROLL = 8


def da_mask_table():
    i = np.arange(2 * DA_QB)[None, :, None] % DA_QB
    j = np.arange(2 * DA_QB)[None, None, :]
    c = np.arange(DA_MASK_CASES)[:, None, None]
    return jnp.asarray(np.where(np.abs(i + DA_HALF * c - j) <= DA_HALF, 0.0, NEG), F32)


def _da_kernel(q_ref, k_ref, v_ref, mask_ref, o_ref, qs, ks, vs, acc_s, lse_s):
    qs[...] = q_ref[0].astype(F32)
    ks[...] = k_ref[0].astype(F32)
    vs[...] = v_ref[0].astype(F32)

    head0 = lax.broadcasted_iota(jnp.int32, (DA_QB, LANES), 1) < ATT_HEAD_DIM

    def block(g, dil, row0, krow0, nk, case):
        qb = qs[pl.ds(row0, DA_QB, stride=dil), :].astype(BF16)
        kb = ks[pl.ds(krow0, nk, stride=dil), :].astype(BF16)
        vb = vs[pl.ds(krow0, nk, stride=dil), :]
        zero = jnp.zeros_like(qb)
        q2 = jnp.concatenate([jnp.where(head0, qb, zero), jnp.where(head0, zero, qb)], axis=0)
        s = lax.dot_general(q2, kb, (((1,), (1,)), ((), ())), preferred_element_type=F32)
        s = s + mask_ref[case, :, 0:nk]
        m = jnp.max(s, axis=-1, keepdims=True)
        p = jnp.exp2(s - m).astype(BF16)
        in0 = lax.broadcasted_iota(jnp.int32, (nk, LANES), 1) < ATT_HEAD_DIM
        v2 = jnp.concatenate([
            jnp.concatenate([jnp.where(in0, vb, 0.0), jnp.where(in0, 1.0, 0.0)], axis=1),
            jnp.concatenate([jnp.where(in0, 0.0, vb), jnp.where(in0, 0.0, 1.0)], axis=1)],
            axis=0).astype(BF16)
        p2 = jnp.concatenate([p[:DA_QB], p[DA_QB:]], axis=1)
        ol = jnp.dot(p2, v2, preferred_element_type=F32)
        l = ol[:, LANES:]
        rows = pl.ds(row0, DA_QB, stride=dil)
        acc_s[g, rows, :] = ol[:, :LANES] / l
        lse_s[g, rows, :] = jnp.where(head0, m[:DA_QB], m[DA_QB:]) + jnp.log(l) * (1.0 / LN2)

    for g, dil in enumerate(DA_BRANCH_DIL):
        n_sub = SEQ // dil
        if n_sub == DA_QB:
            def body(r, carry, g=g, dil=dil):
                block(g, dil, r, r, DA_QB, 0)
                return carry
            lax.fori_loop(0, dil, body, 0, unroll=DA_UNROLL)
        else:
            nb = n_sub // DA_QB
            nk = 2 * DA_QB

            def body(j, carry, g=g, dil=dil, nb=nb, nk=nk, n_sub=n_sub):
                r = j // nb
                q0 = (j % nb) * DA_QB
                k0 = jnp.clip(q0 - DA_HALF, 0, n_sub - nk)
                block(g, dil, r + dil * q0, r + dil * k0, nk, (q0 - k0) // DA_HALF)
                return carry
            lax.fori_loop(0, dil * nb, body, 0, unroll=DA_UNROLL)

    lse_all = jnp.maximum(jnp.maximum(lse_s[0], lse_s[1]), lse_s[2])
    num = jnp.zeros((SEQ, LANES), F32)
    den = jnp.zeros((SEQ, LANES), F32)
    for g in range(len(DA_BRANCH_DIL)):
        w = jnp.exp2(lse_s[g] - lse_all)
        num = num + w * acc_s[g]
        den = den + w
    o_ref[0] = (num / den).astype(o_ref.dtype)


def dilated_attention(proj, mask_t):
    b = proj.shape[0]
    n_hp = DA_HEADS // 2
    blk = lambda off: pl.BlockSpec((1, SEQ, LANES), lambda i, hp, off=off: (i, 0, off + hp))
    nbr = len(DA_BRANCH_DIL)
    return pl.pallas_call(
        _da_kernel, grid=(b, n_hp),
        in_specs=[blk(0), blk(n_hp), blk(2 * n_hp),
                  pl.BlockSpec(mask_t.shape, lambda i, hp: (0, 0, 0))],
        out_specs=pl.BlockSpec((1, SEQ, LANES), lambda i, hp: (i, 0, hp)),
        out_shape=jax.ShapeDtypeStruct((b, SEQ, DA_W), BF16),
        scratch_shapes=[pltpu.VMEM((SEQ, LANES), F32)] * 3 + [pltpu.VMEM((nbr, SEQ, LANES), F32)] * 2,
        compiler_params=_cparams("parallel", "parallel"), name="dilated_attention",
    )(proj, proj, proj, mask_t)


ML_SCALE = ML_HEAD_DIM ** -0.5
GATE_I_FWD, GATE_I_BWD, GATE_F_FWD, GATE_F_BWD = 0, 4, 8, 12


def _exact_ones_matmul(ones_bf16, x):
    hi = x.astype(BF16)
    r1 = x - hi.astype(F32)
    mid = r1.astype(BF16)
    lo = (r1 - mid.astype(F32)).astype(BF16)
    dot = lambda t: jnp.dot(ones_bf16, t, preferred_element_type=F32)
    return dot(hi) + dot(mid) + dot(lo)


_NT = (((1,), (1,)), ((), ()))
ML_COMBOS = 2 * ML_HEADS
ML_HEADS_PER_TRIP = 4
ROW_CUM, ROW_RMAX, ROW_LOGW, ROW_TOT, ROW_LWMAX = range(5)


def _hi_lo_rows(row):
    hi = row.astype(BF16)
    lo = (row - hi.astype(F32)).astype(BF16)
    return jnp.concatenate([hi, lo, jnp.zeros((SUBLANES - 2, row.shape[1]), BF16)], axis=0)


def _mlstm_chunk(q, k, v_t, log_d_t, rows, state):
    c_t, n_vec, m_run = state
    cum, rmax, logw, tot, lwmax = rows
    log_inter = cum + m_run
    m_t = jnp.maximum(log_inter, rmax)
    w_inter = jnp.exp(log_inter - m_t)
    s_t = lax.dot_general(k, q, _NT, preferred_element_type=F32) * ML_SCALE * jnp.exp(log_d_t - m_t)
    num = w_inter * lax.dot_general(c_t.astype(BF16), q, _NT, preferred_element_type=F32)
    num = num + jnp.dot(v_t, s_t.astype(BF16), preferred_element_type=F32)
    nq = lax.dot_general(_hi_lo_rows(n_vec), q, _NT, preferred_element_type=F32)
    den = w_inter * (nq[0:1] + nq[1:2]) + jnp.sum(s_t, axis=0, keepdims=True)
    h_t = num / jnp.maximum(jnp.abs(den), jnp.exp(-m_t))
    m_new = jnp.maximum(tot + m_run, lwmax)
    w_row = jnp.exp(logw - m_new) * ML_SCALE
    decay = jnp.exp(tot + m_run - m_new)
    vw_t = (v_t.astype(F32) * w_row).astype(BF16)
    c_new = decay * c_t + jnp.dot(vw_t, k, preferred_element_type=F32)
    nk = jnp.dot(_hi_lo_rows(w_row), k, preferred_element_type=F32)
    n_new = decay * n_vec + nk[0:1] + nk[1:2]
    return h_t, (c_new, n_new, m_new)


def _mlstm_kernel(q_ref, k_ref, v_ref, og_ref, g_ref, gb_ref, ng_ref, out_ref, vt_s, ldt_s, rows_s, hf_s, hb_s):
    L = ML_CHUNK
    assert ML_HEAD_DIM == L == LANES
    ri = lax.broadcasted_iota(jnp.int32, (L, L), 0)
    ci = lax.broadcasted_iota(jnp.int32, (L, L), 1)
    lower = ci <= ri
    upper = ci >= ri
    lower_m = jnp.where(lower, 1.0, 0.0).astype(BF16)
    upper_m = jnp.where(upper, 1.0, 0.0).astype(BF16)
    lane = lax.broadcasted_iota(jnp.int32, (L, LANES), 1)
    fwd_rows = lax.broadcasted_iota(jnp.int32, (ML_COMBOS, L), 0) < ML_HEADS

    def prep(c, carry):
        rows = pl.ds(pl.multiple_of(c * L, L), L)
        vt_s[c] = v_ref[0, rows, :].T
        g = g_ref[0, rows, :] + gb_ref[...]
        lf = jnp.minimum(g, 0.0) - jnp.log1p(jnp.exp(-jnp.abs(g)))
        cum_f = _exact_ones_matmul(lower_m, lf)
        suf_b = _exact_ones_matmul(upper_m, lf)
        colv = jnp.where(lane < GATE_F_FWD, g, jnp.where(lane < GATE_F_BWD, cum_f, suf_b))
        rowv = colv.T
        i8 = rowv[0:ML_COMBOS]
        cum8 = rowv[ML_COMBOS:2 * ML_COMBOS]
        r_all = colv - pltpu.roll(colv, LANES - ML_COMBOS, 1)
        tot8 = jnp.where(fwd_rows, jnp.broadcast_to(cum8[:, L - 1:L], (ML_COMBOS, L)),
                         jnp.broadcast_to(cum8[:, 0:1], (ML_COMBOS, L)))
        logw8 = tot8 - cum8 + i8
        lwmax8 = jnp.broadcast_to(jnp.max(logw8, axis=-1, keepdims=True), (ML_COMBOS, L))
        rmax = []
        for j in range(ML_COMBOS):
            valid = upper if j < ML_HEADS else lower
            ldt = jnp.where(valid, jnp.broadcast_to(r_all[:, j:j + 1], (L, L)) + cum8[j:j + 1, :], NEG)
            ldt_s[c * ML_COMBOS + j] = ldt
            rmax.append(jnp.max(ldt, axis=0, keepdims=True))
        rows_s[c, ROW_CUM] = cum8
        rows_s[c, ROW_RMAX] = jnp.concatenate(rmax, axis=0)
        rows_s[c, ROW_LOGW] = logw8
        rows_s[c, ROW_TOT] = tot8
        rows_s[c, ROW_LWMAX] = lwmax8
        return carry

    lax.fori_loop(0, N_CHUNKS, prep, 0)

    zero_state = (jnp.zeros((ML_HEAD_DIM, ML_HEAD_DIM), F32), jnp.zeros((1, ML_HEAD_DIM), F32),
                  jnp.zeros((1, L), F32))
    def one(c, h, backward, state):
        hl = slice(ML_HEAD_DIM * h, ML_HEAD_DIM * (h + 1))
        j = ML_HEADS + h if backward else h
        rows = pl.ds(pl.multiple_of(c * L, L), L)
        prepared = tuple(rows_s[c, k, j:j + 1, :] for k in range(5))
        h_t, state = _mlstm_chunk(q_ref[0, rows, hl], k_ref[0, rows, hl], vt_s[c, hl, :],
                                  ldt_s[c * ML_COMBOS + j], prepared, state)
        (hb_s if backward else hf_s)[h, c] = h_t
        return state

    for h0 in range(0, ML_HEADS, ML_HEADS_PER_TRIP):
        def body(c, states, h0=h0):
            out = []
            for k in range(ML_HEADS_PER_TRIP):
                out.append(one(c, h0 + k, False, states[2 * k]))
                out.append(one(N_CHUNKS - 1 - c, h0 + k, True, states[2 * k + 1]))
            return tuple(out)

        lax.fori_loop(0, N_CHUNKS, body, (zero_state,) * (2 * ML_HEADS_PER_TRIP))

    for h in range(ML_HEADS):
        hl = slice(ML_HEAD_DIM * h, ML_HEAD_DIM * (h + 1))
        gain_t = jnp.broadcast_to(ng_ref[:, hl], (L, ML_HEAD_DIM)).T
        for c in range(N_CHUNKS):
            r0 = c * L
            hh = hf_s[h, c] + hb_s[h, c]
            mu = jnp.mean(hh, axis=0, keepdims=True)
            cen = hh - mu
            var = jnp.mean(cen * cen, axis=0, keepdims=True)
            hn = (cen * lax.rsqrt(var + LN_EPS) * gain_t).T
            og = og_ref[0, r0:r0 + L, hl].astype(F32)
            out_ref[0, r0:r0 + L, hl] = (hn / (1.0 + jnp.exp(-og))).astype(out_ref.dtype)


def mlstm_mixer(proj, gates, gate_bias, norm_g):
    b = proj.shape[0]
    c0 = (3 * NA_W) // ML_W
    blk = lambda off: pl.BlockSpec((1, SEQ, ML_W), lambda i, off=off: (i, 0, off))
    gb = jnp.pad(gate_bias.astype(F32), (0, LANES - gate_bias.shape[0])).reshape(1, LANES)
    return pl.pallas_call(
        _mlstm_kernel, grid=(b,),
        in_specs=[blk(c0), blk(c0 + 1), blk(c0 + 2), blk(c0 + 3),
                  pl.BlockSpec((1, SEQ, LANES), lambda i: (i, 0, 0)),
                  pl.BlockSpec((1, LANES), lambda i: (0, 0)),
                  pl.BlockSpec((1, ML_W), lambda i: (0, 0))],
        out_specs=pl.BlockSpec((1, SEQ, ML_W), lambda i: (i, 0, 0)),
        out_shape=jax.ShapeDtypeStruct((b, SEQ, ML_W), BF16),
        scratch_shapes=[pltpu.VMEM((N_CHUNKS, ML_W, ML_CHUNK), BF16),
                        pltpu.VMEM((N_CHUNKS * ML_COMBOS, ML_CHUNK, ML_CHUNK), F32),
                        pltpu.VMEM((N_CHUNKS, 5, ML_COMBOS, ML_CHUNK), F32),
                        pltpu.VMEM((ML_HEADS, N_CHUNKS, ML_HEAD_DIM, ML_CHUNK), F32),
                        pltpu.VMEM((ML_HEADS, N_CHUNKS, ML_HEAD_DIM, ML_CHUNK), F32)],
        compiler_params=_cparams("parallel"), name="mlstm_mixer",
    )(proj, proj, proj, proj, gates, gb, norm_g.reshape(1, ML_W).astype(F32))


def _router_kernel(x_ref, w_ref, aff_ref):
    x = _load_token_tiles(x_ref, x_ref.shape[0] // TILE_ROWS)
    w = w_ref[...]
    x_hi = x.astype(BF16)
    x_lo = (x - x_hi.astype(F32)).astype(BF16)
    w_hi = w.astype(BF16)
    w_lo = (w - w_hi.astype(F32)).astype(BF16)
    dot = lambda a, b: jnp.dot(a, b, preferred_element_type=F32)
    logits_t = dot(x_hi, w_hi) + dot(x_hi, w_lo) + dot(x_lo, w_hi)
    logits = logits_t.T[0:N_EXPERTS, :]
    z = jnp.exp(logits - jnp.max(logits, axis=0, keepdims=True))
    aff_ref[...] = z / jnp.sum(z, axis=0, keepdims=True)


def router_affinities(x_tiles, w_router, tm=1024):
    n, d = x_tiles.shape[0] // TILE_ROWS, D_MODEL
    w_pad = jnp.pad(w_router.astype(F32), ((0, 0), (0, LANES - N_EXPERTS)))
    return pl.pallas_call(
        _router_kernel, grid=(n // tm,),
        in_specs=[pl.BlockSpec((tm * TILE_ROWS, LANES), lambda i: (i, 0)),
                  pl.BlockSpec((d, LANES), lambda i: (0, 0))],
        out_specs=pl.BlockSpec((N_EXPERTS, tm), lambda i: (0, i)),
        out_shape=jax.ShapeDtypeStruct((N_EXPERTS, n), F32),
        compiler_params=_cparams("parallel"), name="router_affinities")(x_tiles, w_pad)


def _tri_matrices(r):
    li = lax.broadcasted_iota(jnp.int32, (LANES, LANES), 0)
    lj = lax.broadcasted_iota(jnp.int32, (LANES, LANES), 1)
    tri_u = jnp.where(li <= lj, 1.0, 0.0).astype(BF16)
    ri = lax.broadcasted_iota(jnp.int32, (r, r), 0)
    rj = lax.broadcasted_iota(jnp.int32, (r, r), 1)
    tri_l = jnp.where(rj < ri, 1.0, 0.0).astype(BF16)
    return tri_u, tri_l


def _prefix_counts(mask, tri_u, tri_l):
    r = mask.shape[0]
    within = jnp.dot(mask.astype(BF16), tri_u, preferred_element_type=F32)
    rowtot = within[:, LANES - 1:LANES]
    hi = jnp.floor(rowtot * (1.0 / 16.0))
    lo = rowtot - 16.0 * hi
    hi_b = jnp.broadcast_to(hi, (r, LANES)).astype(BF16)
    lo_b = jnp.broadcast_to(lo, (r, LANES)).astype(BF16)
    rowoff = 16.0 * jnp.dot(tri_l, hi_b, preferred_element_type=F32) + jnp.dot(
        tri_l, lo_b, preferred_element_type=F32)
    return within - mask + rowoff, within, rowoff, rowtot


def _select_kernel(aff_ref, sel_ref, *, cap):
    n_e, r, _ = aff_ref.shape
    tri_u, tri_l = _tri_matrices(r)

    def bits(k):
        return pltpu.bitcast(aff_ref[k], jnp.int32)

    def count(m):
        c = jnp.sum(jnp.where(m, 1.0, 0.0), axis=0, keepdims=True)
        return jnp.sum(c, axis=1, keepdims=True)

    def bisect(i, prefixes):
        bit = jnp.left_shift(jnp.int32(1), 30 - i)
        return tuple(jnp.where(count(bits(k) >= (p | bit)) >= cap, p | bit, p)
                     for k, p in enumerate(prefixes))

    thr = lax.fori_loop(0, 31, bisect, (jnp.zeros((1, 1), jnp.int32),) * n_e)
    for k in range(n_e):
        gt = bits(k) > thr[k]
        eq = bits(k) == thr[k]
        need = cap - count(gt)
        rank_eq, _, _, _ = _prefix_counts(jnp.where(eq, 1.0, 0.0), tri_u, tri_l)
        sel_ref[k] = jnp.where(gt | (eq & (rank_eq < need)), 1.0, 0.0)


def select_tokens(aff3, cap):
    e, r, _ = aff3.shape
    blk = pl.BlockSpec((e, r, LANES), lambda i: (0, 0, 0))
    return pl.pallas_call(
        functools.partial(_select_kernel, cap=cap), grid=(1,), in_specs=[blk], out_specs=blk,
        out_shape=jax.ShapeDtypeStruct((e, r, LANES), F32),
        compiler_params=_cparams("arbitrary"), name="select_tokens")(aff3)


def _lists_kernel(sel_ref, aff_ref, idx_ref, dst_ref, gate_ref, ts_ref, ts_s, er_s, *, cap, st):
    e = pl.program_id(0)
    r = sel_ref.shape[1]
    tri_u, tri_l = _tri_matrices(r)

    @pl.when(e == 0)
    def _():
        cnt = sel_ref[0]
        for k in range(1, N_EXPERTS):
            cnt = cnt + sel_ref[k]
        ts, _, _, _ = _prefix_counts(cnt, tri_u, tri_l)
        ts_s[...] = ts
        ts_ref[...] = ts
        er_s[...] = jnp.zeros_like(er_s)

    sel = sel_ref[e]
    _, within, rowoff, rowtot = _prefix_counts(sel, tri_u, tri_l)
    dst = ts_s[...] + er_s[...]
    er_s[...] = er_s[...] + sel

    d2 = jnp.floor(dst * (1.0 / 65536.0))
    rem = dst - 65536.0 * d2
    d1 = jnp.floor(rem * (1.0 / 256.0))
    d0 = rem - 256.0 * d1
    aff = aff_ref[0]
    a_hi = aff.astype(BF16)
    a_r1 = aff - a_hi.astype(F32)
    a_mid = a_r1.astype(BF16)
    a_lo = (a_r1 - a_mid.astype(F32)).astype(BF16)
    rhs = jnp.concatenate([within.astype(BF16), d0.astype(BF16), d1.astype(BF16), d2.astype(BF16),
                           a_hi, a_mid, a_lo], axis=1)
    rowoff_row = rowoff.T[0:1, :]
    rowend_row = rowoff_row + jnp.broadcast_to(rowtot, (r, LANES)).T[0:1, :]
    rho_row = lax.broadcasted_iota(jnp.int32, (1, r), 1).astype(F32)
    lane = lax.broadcasted_iota(jnp.int32, (st, LANES), 1).astype(F32)
    eye = lax.broadcasted_iota(jnp.int32, (LANES, LANES), 0) == lax.broadcasted_iota(
        jnp.int32, (LANES, LANES), 1)

    for t in range(cap // st):
        s_col = (t * st + lax.broadcasted_iota(jnp.int32, (st, 1), 0)).astype(F32)
        in_row = (rowoff_row <= s_col) & (s_col < rowend_row)
        got = jnp.dot(jnp.where(in_row, 1.0, 0.0).astype(BF16), rhs, preferred_element_type=F32)
        base = jnp.sum(jnp.where(in_row, rowoff_row, 0.0), axis=1, keepdims=True)
        rho = jnp.sum(jnp.where(in_row, rho_row, 0.0), axis=1, keepdims=True)
        local = s_col - base
        lam = jnp.sum(jnp.where(got[:, 0:LANES] <= local, 1.0, 0.0), axis=1, keepdims=True)
        plane = lambda k: got[:, k * LANES:(k + 1) * LANES]
        pair = plane(1) + 256.0 * plane(2) + 65536.0 * plane(3)
        at_lam = lane == lam
        dval = jnp.sum(jnp.where(at_lam, pair, 0.0), axis=1, keepdims=True)
        gate_ref[0, t * st:(t + 1) * st, :] = jnp.sum(
            jnp.where(at_lam, plane(4) + plane(5) + plane(6), 0.0), axis=1, keepdims=True)
        ival = rho * float(LANES) + lam
        for j in range(st // LANES):
            seg = slice(j * LANES, (j + 1) * LANES)
            row = t * (st // LANES) + j
            idx_ref[0, row:row + 1, :] = jnp.sum(
                jnp.where(eye, ival[seg], 0.0), axis=0, keepdims=True).astype(jnp.int32)
            dst_ref[0, row:row + 1, :] = jnp.sum(
                jnp.where(eye, dval[seg], 0.0), axis=0, keepdims=True).astype(jnp.int32)


def build_lists(sel3, aff3, cap):
    e, r, _ = sel3.shape
    st = min(512, cap)
    lst = pl.BlockSpec((1, cap // LANES, LANES), lambda i: (i, 0, 0))
    return pl.pallas_call(
        functools.partial(_lists_kernel, cap=cap, st=st), grid=(e,),
        in_specs=[pl.BlockSpec((e, r, LANES), lambda i: (0, 0, 0)),
                  pl.BlockSpec((1, r, LANES), lambda i: (i, 0, 0))],
        out_specs=[lst, lst, pl.BlockSpec((1, cap, 1), lambda i: (i, 0, 0)),
                   pl.BlockSpec((r, LANES), lambda i: (0, 0))],
        out_shape=[jax.ShapeDtypeStruct((e, cap // LANES, LANES), jnp.int32),
                   jax.ShapeDtypeStruct((e, cap // LANES, LANES), jnp.int32),
                   jax.ShapeDtypeStruct((e, cap, 1), F32), jax.ShapeDtypeStruct((r, LANES), F32)],
        scratch_shapes=[pltpu.VMEM((r, LANES), F32), pltpu.VMEM((r, LANES), F32)],
        compiler_params=_cparams("arbitrary"), name="build_lists")(sel3, aff3)


FFN_CHUNKS = tuple((f, min(f + 256, D_FF)) for f in range(0, D_FF, 256))


def _ffn_kernel(idx_first, idx_next_a, idx_next_b, dst_prev_a, dst_prev_b, dst_last,
                x_hbm, gate_a, gate_b, wg_ref, wu_ref, wd_ref, z_hbm,
                xbuf0, xbuf1, ybuf0, ybuf1, acc, gsem, ssem, *, tm, n_grid):
    g = pl.program_id(0)

    def token_tile(t):
        if isinstance(t, int):
            return pl.ds(t * TILE_ROWS, TILE_ROWS)
        return pl.ds(pl.multiple_of(t * TILE_ROWS, TILE_ROWS), TILE_ROWS)

    def gather_row(ids, i, buf, sem):
        pltpu.make_async_copy(x_hbm.at[token_tile(ids[0, 0, i]), :], buf.at[token_tile(i), :],
                              sem).start(priority=1)

    def scatter_row(dsts, i, buf, sem):
        pltpu.make_async_copy(buf.at[token_tile(i), :], z_hbm.at[token_tile(dsts[0, 0, i]), :],
                              sem).start(priority=0)

    def wait_rows(buf, sem):
        pltpu.make_async_copy(x_hbm.at[pl.ds(0, tm * TILE_ROWS), :], buf, sem).wait()

    @pl.when(g == 0)
    def _():
        ybuf1[...] = jnp.zeros_like(ybuf1)

        def first(i, carry):
            gather_row(idx_first, i, xbuf0, gsem.at[0])
            return carry
        lax.fori_loop(0, tm, first, 0, unroll=8)

    def tile(xcur, gcur, xnext, gnext, idx_next, yprev, sprev, dst_prev):
        for i in range(tm):
            gather_row(idx_next, i, xnext, gnext)
            scatter_row(dst_prev, i, yprev, sprev)

        wait_rows(xcur, gcur)
        xb = _load_token_tiles(xcur, tm).astype(BF16)
        for c, (f0, f1) in enumerate(FFN_CHUNKS):
            gate = jnp.dot(xb, wg_ref[0, :, f0:f1], preferred_element_type=F32)
            up = jnp.dot(xb, wu_ref[0, :, f0:f1], preferred_element_type=F32)
            h = (gate / (1.0 + jnp.exp(-gate)) * up).astype(BF16)
            part = jnp.dot(h, wd_ref[0, f0:f1, :], preferred_element_type=F32)
            if c == 0:
                acc[...] = part
            else:
                acc[...] += part

    tile(xbuf0, gsem.at[0], xbuf1, gsem.at[1], idx_next_a, ybuf1, ssem.at[1], dst_prev_a)

    @pl.when(g >= 1)
    def _():
        wait_rows(ybuf0, ssem.at[0])
    _store_token_tiles(ybuf0, acc[...] * gate_a[0])

    tile(xbuf1, gsem.at[1], xbuf0, gsem.at[0], idx_next_b, ybuf0, ssem.at[0], dst_prev_b)
    wait_rows(ybuf1, ssem.at[1])
    _store_token_tiles(ybuf1, acc[...] * gate_b[0])

    @pl.when(g == n_grid - 1)
    def _():
        def last(i, carry):
            scatter_row(dst_last, i, ybuf1, ssem.at[1])
            return carry
        lax.fori_loop(0, tm, last, 0, unroll=8)
        wait_rows(ybuf1, ssem.at[1])
        wait_rows(ybuf0, ssem.at[0])
        wait_rows(xbuf0, gsem.at[0])


def expert_ffn(x_tiles, idx, dst, gate, w_gate, w_up, w_down, tm=512):
    d = D_MODEL
    e, cap = idx.shape
    tm = min(tm, cap // 2)
    nt = cap // tm
    assert nt % 2 == 0
    n_tiles = e * nt
    n_grid = n_tiles // 2
    idx3 = idx.reshape(n_tiles, 1, tm)
    gate3 = gate.reshape(n_tiles, tm, 1)
    spare =(e * cap + jnp.arange(tm, dtype=jnp.int32)).reshape(1, 1, tm)
    dst3 = jnp.concatenate([spare, dst.reshape(n_tiles, 1, tm)])
    smem = lambda imap: pl.BlockSpec((1, 1, tm), imap, memory_space=pltpu.SMEM)
    wspec = lambda w: pl.BlockSpec((1,) + w.shape[1:], lambda i: ((2 * i) // nt, 0, 0))
    return pl.pallas_call(
        functools.partial(_ffn_kernel, tm=tm, n_grid=n_grid), grid=(n_grid,),
        in_specs=[smem(lambda i: (0, 0, 0)), smem(lambda i: (2 * i + 1, 0, 0)),
                  smem(lambda i: (jnp.minimum(2 * i + 2, n_tiles - 1), 0, 0)),
                  smem(lambda i: (2 * i, 0, 0)), smem(lambda i: (2 * i + 1, 0, 0)),
                  smem(lambda i: (n_tiles, 0, 0)),
                  pl.BlockSpec(memory_space=pl.ANY),
                  pl.BlockSpec((1, tm, 1), lambda i: (2 * i, 0, 0)),
                  pl.BlockSpec((1, tm, 1), lambda i: (2 * i + 1, 0, 0)),
                  wspec(w_gate), wspec(w_up), wspec(w_down)],
        out_specs=pl.BlockSpec(memory_space=pl.ANY),
        out_shape=jax.ShapeDtypeStruct(((e * cap + tm) * TILE_ROWS, LANES), F32),
        scratch_shapes=[pltpu.VMEM((tm * TILE_ROWS, LANES), F32)] * 4 + [pltpu.VMEM((tm, d), F32)]
        + [pltpu.SemaphoreType.DMA((2,)), pltpu.SemaphoreType.DMA((2,))],
        compiler_params=_cparams("arbitrary"), name="expert_ffn",
    )(idx3, idx3, idx3, dst3, dst3, dst3, x_tiles, gate3, gate3, w_gate, w_up, w_down)


COMBINE_ZB = 512
COMBINE_SLOTS = 4


def _combine_kernel(ts_ref, x_ref, run_ref, g_ref, b_ref, z_hbm, o_ref, zbuf, sem, used, *, z_rows, n_tiles):
    i = pl.program_id(0)
    tt = x_ref.shape[0] // TILE_ROWS

    def chunk_rows(t, c):
        lo = ts_ref[t] + c * COMBINE_ZB
        return lo, jnp.minimum(lo, z_rows - COMBINE_ZB)

    def piece_copy(start, j, slot):
        return pltpu.make_async_copy(z_hbm.at[pl.ds(start, COMBINE_ZB), j, :], zbuf.at[slot, j], sem.at[slot])

    def chunks_of(t):
        return jnp.maximum((ts_ref[t + 1] - ts_ref[t] + COMBINE_ZB - 1) // COMBINE_ZB, 1)

    def fetch_next():
        t = used[2]

        @pl.when(t < n_tiles)
        def _():
            c = used[3]
            slot = used[1] % COMBINE_SLOTS
            _, start = chunk_rows(t, c)
            for j in range(TILE_ROWS):
                piece_copy(start, j, slot).start()
            used[1] = used[1] + 1
            last = c + 1 >= chunks_of(t)
            used[2] = jnp.where(last, t + 1, t)
            used[3] = jnp.where(last, 0, c + 1)

    @pl.when(i == 0)
    def _():
        for k in range(4):
            used[k] = 0
        for _ in range(COMBINE_SLOTS - 1):
            fetch_next()

    base = used[0]
    n_chunks = chunks_of(i)
    run_lo = jnp.broadcast_to(run_ref[:, 0:1], (tt, COMBINE_ZB))
    run_hi = jnp.broadcast_to(run_ref[:, 1:2], (tt, COMBINE_ZB))
    col = lax.broadcasted_iota(jnp.int32, (1, COMBINE_ZB), 1).astype(F32)

    def chunk(c, acc):
        slot = (base + c) % COMBINE_SLOTS
        lo, start = chunk_rows(i, c)
        pair = col + start.astype(F32)
        pair = jnp.where(pair >= lo.astype(F32), pair, -1.0)
        a = jnp.where((run_lo <= pair) & (pair < run_hi), 1.0, 0.0).astype(BF16)
        for j in range(TILE_ROWS):
            piece_copy(0, j, slot).wait()
        rows = jnp.concatenate([zbuf[slot, j] for j in range(TILE_ROWS)], axis=1).astype(BF16)
        acc = acc + jnp.dot(a, rows, preferred_element_type=F32)
        fetch_next()
        return acc

    ffn = lax.fori_loop(0, n_chunks, chunk, jnp.zeros((tt, D_MODEL), F32))
    used[0] = base + n_chunks
    o_ref[...] = _layer_norm_rows(DN_ALPHA * _load_token_tiles(x_ref, tt) + ffn, g_ref[...], b_ref[...])


def combine_ln(x_tiles, z, tile_start, runs, g, b, tt=256):
    n, d = x_tiles.shape[0] // TILE_ROWS, D_MODEL
    n_tiles = n // tt
    row = lambda i, ts: (i, 0)
    fixed = lambda i, ts: (0, 0)
    grid_spec = pltpu.PrefetchScalarGridSpec(
        num_scalar_prefetch=1, grid=(n_tiles,),
        in_specs=[pl.BlockSpec((tt * TILE_ROWS, LANES), row), pl.BlockSpec((tt, 2), row),
                  pl.BlockSpec((1, d), fixed),
                  pl.BlockSpec((1, d), fixed), pl.BlockSpec(memory_space=pl.ANY)],
        out_specs=pl.BlockSpec((tt, d), row),
        scratch_shapes=[pltpu.VMEM((COMBINE_SLOTS, TILE_ROWS, COMBINE_ZB, LANES), F32),
                        pltpu.SemaphoreType.DMA((COMBINE_SLOTS,)), pltpu.SMEM((4,), jnp.int32)])
    z_rows = z.shape[0] // TILE_ROWS
    return pl.pallas_call(
        functools.partial(_combine_kernel, z_rows=z_rows, n_tiles=n_tiles), grid_spec=grid_spec,
        out_shape=jax.ShapeDtypeStruct((n, d), F32),
        compiler_params=_cparams("arbitrary"), name="combine_ln",
    )(tile_start, x_tiles, runs, g.reshape(1, d), b.reshape(1, d), z.reshape(z_rows, TILE_ROWS, LANES))


def moe_layer(x_tiles, w_router, w_gate, w_up, w_down, g, b, tt=256):
    n = x_tiles.shape[0] // TILE_ROWS
    r = n // LANES
    cap = 2 * n // N_EXPERTS
    aff3 = router_affinities(x_tiles, w_router).reshape(N_EXPERTS, r, LANES)
    sel3 = select_tokens(aff3, cap)
    idx, dst, gate, ts = build_lists(sel3, aff3, cap)
    z = expert_ffn(x_tiles, idx.reshape(N_EXPERTS, cap), dst.reshape(N_EXPERTS, cap), gate, w_gate, w_up, w_down)
    ts_ext = jnp.concatenate([ts.reshape(n), jnp.full((1,), N_EXPERTS * cap, F32)])
    tile_start = ts_ext[::tt].astype(jnp.int32)
    runs = jnp.stack([ts_ext[:-1], ts_ext[1:]], axis=1)
    return combine_ln(x_tiles, z, tile_start, runs, g, b, tt=tt)


def kernel(x_prompt, x_sample, even_w_in, ml_gate_bias, na_rpb, ml_norm_g, even_w_out, da_w_in, da_w_out,
           ln_mix_g, ln_mix_b, ec_router, ec_w_gate, ec_w_up, ec_w_down, ln_ffn_g, ln_ffn_b):
    n_mix_cols = 3 * NA_W + 4 * ML_W
    n_gates = 4 * ML_HEADS
    w_even = even_w_in[0][:, :n_mix_cols].astype(BF16)
    w_gates = jnp.pad(even_w_in[0][:, n_mix_cols:], ((0, 0), (0, LANES - n_gates))).astype(BF16)
    w_out_a = even_w_out[0][:NA_W].astype(BF16)
    w_out_b = even_w_out[0][NA_W:].astype(BF16)
    w_odd = da_w_in[0].astype(BF16)
    w_odd_out = da_w_out[0].astype(BF16)
    tbl = na_bias_table(na_rpb[0])
    rope_t = rope_tables()
    mask_t = da_mask_table()
    moe_w = [(ec_router[l], ec_w_gate[l].astype(BF16), ec_w_up[l].astype(BF16), ec_w_down[l].astype(BF16),
              ln_ffn_g[l], ln_ffn_b[l]) for l in range(DEPTH)]

    def trunk(x):
        b = x.shape[0]
        xt = x.reshape(b * SEQ, D_MODEL)
        proj, gates = in_projection_gates(xt, w_even, w_gates)
        proj = proj.reshape(b, SEQ, -1)
        ya = neighbourhood_attention(proj, tbl)
        yb = mlstm_mixer(proj, gates.reshape(b, SEQ, LANES), ml_gate_bias[0], ml_norm_g[0])
        xt_tiles = out_projection_ln(xt, [ya.reshape(b * SEQ, NA_W), yb.reshape(b * SEQ, ML_W)],
                                     [w_out_a, w_out_b], ln_mix_g[0], ln_mix_b[0])
        xt = moe_layer(xt_tiles, *moe_w[0])
        proj = in_projection_rope(xt, w_odd, rope_t, DA_W).reshape(b, SEQ, -1)
        yc = dilated_attention(proj, mask_t)
        xt_tiles = out_projection_ln(xt, [yc.reshape(b * SEQ, DA_W)], [w_odd_out], ln_mix_g[1], ln_mix_b[1])
        xt = moe_layer(xt_tiles, *moe_w[1])
        return xt.reshape(b, SEQ, D_MODEL)

    return trunk(x_prompt), trunk(x_sample)
```

```python
import functools

import numpy as np
import jax
import jax.numpy as jnp
from jax import lax
from jax.experimental import pallas as pl
from jax.experimental.pallas import tpu as pltpu

F32 = jnp.float32
BF16 = jnp.bfloat16

D_MODEL = 1024
SEQ = 2048
GRID_W = 64
GRID_ROWS = SEQ // GRID_W
NA_HEADS = 8
NA_W = 512
NA_WIN_R = 8
NA_WIN_C = 16
ML_HEADS = 4
ML_HEAD_DIM = 128
ML_W = 512
ML_CHUNK = 128
N_CHUNKS = SEQ // ML_CHUNK
DA_HEADS = 16
DA_W = 1024
DA_HALF = 64
ROPE_THETA = 10000.0
N_EXPERTS = 16
D_FF = 1408
LN_EPS = 1e-5
DEPTH = 2
DN_ALPHA = (2 * DEPTH) ** 0.25
LANES = 128
ATT_HEAD_DIM = 64
ROT_HALF = ATT_HEAD_DIM // 2
NEG = -1e30
LOG2E = 1.4426950408889634
LN2 = 0.6931471805599453
VMEM_LIMIT = 56 * 1024 * 1024


def _cparams(*sem):
    return pltpu.CompilerParams(dimension_semantics=sem, vmem_limit_bytes=VMEM_LIMIT)


def _inproj_gates_kernel(x_ref, w_ref, wg_ref, o_ref, g_ref, *, n_chunk):
    xb = x_ref[...].astype(BF16)
    for c in range(0, w_ref.shape[1], n_chunk):
        o_ref[:, c:c + n_chunk] = jnp.dot(
            xb, w_ref[:, c:c + n_chunk], preferred_element_type=F32).astype(o_ref.dtype)
    g_ref[...] = jnp.dot(xb, wg_ref[...], preferred_element_type=F32)


def _rotate_half_pairs(x, cos, sin_signed):
    first_half = lax.broadcasted_iota(jnp.int32, x.shape, 1) % ATT_HEAD_DIM < ROT_HALF
    swapped = jnp.where(first_half, pltpu.roll(x, LANES - ROT_HALF, 1), pltpu.roll(x, ROT_HALF, 1))
    return x * cos + swapped * sin_signed


def _inproj_rope_kernel(x_ref, w_ref, rope_ref, o_ref, *, n_chunk, width):
    xb = x_ref[...].astype(BF16)
    for c in range(0, w_ref.shape[1], n_chunk):
        res = jnp.dot(xb, w_ref[:, c:c + n_chunk], preferred_element_type=F32)
        if c < 2 * width:
            t = 0 if c < width else 2
            res = jnp.concatenate(
                [_rotate_half_pairs(res[:, l:l + LANES], rope_ref[t], rope_ref[t + 1])
                 for l in range(0, n_chunk, LANES)], axis=1)
        o_ref[:, c:c + n_chunk] = res.astype(o_ref.dtype)


def in_projection_rope(x, w, rope_tables, width, tm=512, n_chunk=512):
    m, k = x.shape
    n = w.shape[1]
    assert width % n_chunk == 0 and SEQ % tm == 0
    return pl.pallas_call(
        functools.partial(_inproj_rope_kernel, n_chunk=n_chunk, width=width),
        grid=(m // tm,),
        in_specs=[pl.BlockSpec((tm, k), lambda i: (i, 0)), pl.BlockSpec((k, n), lambda i: (0, 0)),
                  pl.BlockSpec((4, tm, LANES), lambda i: (0, i % (SEQ // tm), 0))],
        out_specs=pl.BlockSpec((tm, n), lambda i: (i, 0)),
        out_shape=jax.ShapeDtypeStruct((m, n), BF16),
        compiler_params=_cparams("parallel"), name="in_projection_rope")(x, w, rope_tables)


def in_projection_gates(x, w, wg, tm=512, n_chunk=512):
    m, k = x.shape
    n = w.shape[1]
    x_spec = pl.BlockSpec((tm, k), lambda i: (i, 0))
    w_spec = pl.BlockSpec((k, n), lambda i: (0, 0))
    o_spec = pl.BlockSpec((tm, n), lambda i: (i, 0))
    return pl.pallas_call(
        functools.partial(_inproj_gates_kernel, n_chunk=n_chunk),
        grid=(m // tm,),
        in_specs=[x_spec, w_spec, pl.BlockSpec((k, LANES), lambda i: (0, 0))],
        out_specs=[o_spec, pl.BlockSpec((tm, LANES), lambda i: (i, 0))],
        out_shape=[jax.ShapeDtypeStruct((m, n), BF16), jax.ShapeDtypeStruct((m, LANES), F32)],
        compiler_params=_cparams("parallel"), name="in_projection_gates")(x, w, wg)


def _layer_norm_rows(acc, g, b):
    mu = jnp.mean(acc, axis=-1, keepdims=True)
    cen = acc - mu
    var = jnp.mean(cen * cen, axis=-1, keepdims=True)
    return cen * lax.rsqrt(var + LN_EPS) * g + b


def _outproj_ln_kernel(*refs, n_mix):
    x_ref = refs[0]
    mix_refs = refs[1:1 + n_mix]
    w_refs = refs[1 + n_mix:1 + 2 * n_mix]
    g_ref, b_ref, ot_ref = refs[1 + 2 * n_mix:]
    acc = DN_ALPHA * x_ref[...]
    for m_ref, w_ref in zip(mix_refs, w_refs):
        acc = acc + jnp.dot(m_ref[...], w_ref[...], preferred_element_type=F32)
    _store_token_tiles(ot_ref, _layer_norm_rows(acc, g_ref[...], b_ref[...]))


SUBLANES = 8
TILE_ROWS = D_MODEL // LANES


def _store_token_tiles(ref, rows):
    n = rows.shape[0]
    for j in range(TILE_ROWS):
        ref[pl.ds(j, n, stride=TILE_ROWS), :] = rows[:, j * LANES:(j + 1) * LANES]


def _load_token_tiles(ref, n):
    return jnp.concatenate([ref[pl.ds(j, n, stride=TILE_ROWS), :] for j in range(TILE_ROWS)], axis=1)


def out_projection_ln(x, mixes, ws, g, b, tm=512):
    m, d = x.shape
    n_mix = len(mixes)
    row = lambda i: (i, 0)
    fixed = lambda i: (0, 0)
    in_specs = [pl.BlockSpec((tm, d), row)]
    in_specs += [pl.BlockSpec((tm, mx.shape[1]), row) for mx in mixes]
    in_specs += [pl.BlockSpec(w.shape, fixed) for w in ws]
    in_specs += [pl.BlockSpec((1, d), fixed), pl.BlockSpec((1, d), fixed)]
    return pl.pallas_call(
        functools.partial(_outproj_ln_kernel, n_mix=n_mix),
        grid=(m // tm,), in_specs=in_specs,
        out_specs=pl.BlockSpec((tm * TILE_ROWS, LANES), row),
        out_shape=jax.ShapeDtypeStruct((m * TILE_ROWS, LANES), F32),
        compiler_params=_cparams("parallel"), name="out_projection_ln",
    )(x, *mixes, *ws, g.reshape(1, d), b.reshape(1, d))


NA_PAIR_ROWS = 10
NA_PAIR_KEYS = NA_PAIR_ROWS * GRID_W
NA_PAIR_CASES = 5
NA_N_PAIRS = GRID_ROWS // 2


def _na_pair_window(p):
    start = jnp.minimum(jnp.clip(2 * p - NA_WIN_R // 2, 0, GRID_ROWS - NA_WIN_R), GRID_ROWS - NA_PAIR_ROWS)
    case = jnp.minimum(p, 2) + jnp.maximum(p - (NA_N_PAIRS - 3), 0)
    return start, case


def na_bias_table(rpb):
    j = np.arange(GRID_W)
    kc = np.arange(GRID_W)
    win_c0 = np.clip(j - NA_WIN_C // 2, 0, GRID_W - NA_WIN_C)
    valid = (kc[None, :] >= win_c0[:, None]) & (kc[None, :] < win_c0[:, None] + NA_WIN_C)
    dc = np.clip(kc[None, :] - j[:, None] + NA_WIN_C - 1, 0, 2 * NA_WIN_C - 2)
    n_dc = 2 * NA_WIN_C - 1
    pick = jnp.asarray(dc[:, :, None] == np.arange(n_dc)[None, None, :], F32)
    rows = jnp.einsum('hrd,jkd->hrjk', rpb.astype(F32), pick, precision=lax.Precision.HIGHEST)
    rows = jnp.where(valid[None, None], rows * LOG2E, NEG)
    masked = jnp.full_like(rows[:, 0], NEG)
    representative = (0, 1, 2, NA_N_PAIRS - 2, NA_N_PAIRS - 1)
    cases = []
    for p in representative:
        start = min(max(2 * p - NA_WIN_R // 2, 0), GRID_ROWS - NA_WIN_R, GRID_ROWS - NA_PAIR_ROWS)
        per_row = []
        for r in (2 * p, 2 * p + 1):
            rs = min(max(r - NA_WIN_R // 2, 0), GRID_ROWS - NA_WIN_R)
            per_row.append(jnp.stack(
                [rows[:, start + w - r + NA_WIN_R - 1] if rs <= start + w < rs + NA_WIN_R else masked
                 for w in range(NA_PAIR_ROWS)], axis=1))
        cases.append(jnp.stack(per_row, axis=1))
    t = jnp.stack(cases, axis=1)
    t = t.transpose(0, 1, 2, 4, 3, 5).reshape(NA_HEADS // 2, 2, NA_PAIR_CASES, 2 * GRID_W, NA_PAIR_KEYS)
    return t.transpose(0, 2, 1, 3, 4).reshape(NA_HEADS // 2, NA_PAIR_CASES, 4 * GRID_W, NA_PAIR_KEYS)


NA_SCORE_SCALE = ATT_HEAD_DIM ** -0.5 * LOG2E
NA_UNROLL = 8


def _na_kernel(q_ref, k_ref, v_ref, tbl_ref, o_ref, v2a, v2b):
    nq = 2 * GRID_W
    head0 = lax.broadcasted_iota(jnp.int32, (nq, LANES), 1) < ATT_HEAD_DIM

    in0 = lax.broadcasted_iota(jnp.int32, (SEQ, LANES), 1) < ATT_HEAD_DIM
    vf = v_ref[0].astype(F32)
    v2a[...] = jnp.concatenate([jnp.where(in0, vf, 0.0), jnp.where(in0, 1.0, 0.0)], axis=1).astype(BF16)
    v2b[...] = jnp.concatenate([jnp.where(in0, 0.0, vf), jnp.where(in0, 0.0, 1.0)], axis=1).astype(BF16)

    def pair(p, carry):
        start, case = _na_pair_window(p)
        rows = pl.ds(pl.multiple_of(p * nq, nq), nq)
        q = q_ref[0, rows, :]
        zero = jnp.zeros_like(q)
        q2 = jnp.concatenate([jnp.where(head0, q, zero), jnp.where(head0, zero, q)], axis=0)
        win = pl.ds(pl.multiple_of(start * GRID_W, GRID_W), NA_PAIR_KEYS)
        s = lax.dot_general(q2, k_ref[0, win, :], (((1,), (1,)), ((), ())), preferred_element_type=F32)
        s = s * NA_SCORE_SCALE + tbl_ref[0, case]
        pr = jnp.exp2(s - jnp.max(s, axis=-1, keepdims=True)).astype(BF16)
        p2 = jnp.concatenate([pr[:nq], pr[nq:]], axis=1)
        ol = jnp.dot(p2, jnp.concatenate([v2a[win, :], v2b[win, :]], axis=0), preferred_element_type=F32)
        o_ref[0, rows, :] = (ol[:, :LANES] / ol[:, LANES:]).astype(o_ref.dtype)
        return carry

    lax.fori_loop(0, NA_N_PAIRS, pair, 0, unroll=NA_UNROLL)


def neighbourhood_attention(proj, tbl):
    b = proj.shape[0]
    n_hp = NA_HEADS // 2
    blk = lambda off: pl.BlockSpec((1, SEQ, LANES), lambda hp, i, off=off: (i, 0, off + hp))
    return pl.pallas_call(
        _na_kernel, grid=(n_hp, b),
        in_specs=[blk(0), blk(n_hp), blk(2 * n_hp),
                  pl.BlockSpec((1, NA_PAIR_CASES, 4 * GRID_W, NA_PAIR_KEYS), lambda hp, i: (hp, 0, 0, 0))],
        out_specs=pl.BlockSpec((1, SEQ, LANES), lambda hp, i: (i, 0, hp)),
        out_shape=jax.ShapeDtypeStruct((b, SEQ, NA_W), BF16),
        scratch_shapes=[pltpu.VMEM((SEQ, 2 * LANES), BF16)] * 2,
        compiler_params=_cparams("parallel", "parallel"), name="neighbourhood_attention",
    )(proj, proj, proj, tbl)


DA_BRANCH_DIL = (1, 4, 16)
DA_QB = 128


DA_Q_SCALE = ATT_HEAD_DIM ** -0.5 * LOG2E


def rope_tables():
    lane = np.arange(LANES) % ATT_HEAD_DIM
    inv = ROPE_THETA ** (-(2.0 * (lane % ROT_HALF)) / ATT_HEAD_DIM)
    ang = jnp.arange(SEQ, dtype=F32)[:, None] * jnp.asarray(inv, F32)[None, :]
    sign = jnp.asarray(np.where(lane < ROT_HALF, -1.0, 1.0), F32)[None, :]
    cos, sin = jnp.cos(ang), jnp.sin(ang) * sign
    return jnp.stack([cos * DA_Q_SCALE, sin * DA_Q_SCALE, cos, sin])
DA_MASK_CASES = 3
DA_UNROLL = 8


def da_mask_table():
    i = np.arange(2 * DA_QB)[None, :, None] % DA_QB
    j = np.arange(2 * DA_QB)[None, None, :]
    c = np.arange(DA_MASK_CASES)[:, None, None]
    return jnp.asarray(np.where(np.abs(i + DA_HALF * c - j) <= DA_HALF, 0.0, NEG), F32)


def _da_kernel(q_ref, k_ref, v_ref, mask_ref, o_ref, qs, ks, vs, acc_s, lse_s):
    qs[...] = q_ref[0].astype(F32)
    ks[...] = k_ref[0].astype(F32)
    vs[...] = v_ref[0].astype(F32)

    head0 = lax.broadcasted_iota(jnp.int32, (DA_QB, LANES), 1) < ATT_HEAD_DIM

    def block(g, dil, row0, krow0, nk, case):
        qb = qs[pl.ds(row0, DA_QB, stride=dil), :].astype(BF16)
        kb = ks[pl.ds(krow0, nk, stride=dil), :].astype(BF16)
        vb = vs[pl.ds(krow0, nk, stride=dil), :]
        zero = jnp.zeros_like(qb)
        q2 = jnp.concatenate([jnp.where(head0, qb, zero), jnp.where(head0, zero, qb)], axis=0)
        s = lax.dot_general(q2, kb, (((1,), (1,)), ((), ())), preferred_element_type=F32)
        s = s + mask_ref[case, :, 0:nk]
        m = jnp.max(s, axis=-1, keepdims=True)
        p = jnp.exp2(s - m).astype(BF16)
        in0 = lax.broadcasted_iota(jnp.int32, (nk, LANES), 1) < ATT_HEAD_DIM
        v2 = jnp.concatenate([
            jnp.concatenate([jnp.where(in0, vb, 0.0), jnp.where(in0, 1.0, 0.0)], axis=1),
            jnp.concatenate([jnp.where(in0, 0.0, vb), jnp.where(in0, 0.0, 1.0)], axis=1)],
            axis=0).astype(BF16)
        p2 = jnp.concatenate([p[:DA_QB], p[DA_QB:]], axis=1)
        ol = jnp.dot(p2, v2, preferred_element_type=F32)
        l = ol[:, LANES:]
        rows = pl.ds(row0, DA_QB, stride=dil)
        acc_s[g, rows, :] = ol[:, :LANES] / l
        lse_s[g, rows, :] = jnp.where(head0, m[:DA_QB], m[DA_QB:]) + jnp.log(l) * (1.0 / LN2)

    for g, dil in enumerate(DA_BRANCH_DIL):
        n_sub = SEQ // dil
        if n_sub == DA_QB:
            def body(r, carry, g=g, dil=dil):
                block(g, dil, r, r, DA_QB, 0)
                return carry
            lax.fori_loop(0, dil, body, 0, unroll=DA_UNROLL)
        else:
            nb = n_sub // DA_QB
            nk = 2 * DA_QB

            def body(j, carry, g=g, dil=dil, nb=nb, nk=nk, n_sub=n_sub):
                r = j // nb
                q0 = (j % nb) * DA_QB
                k0 = jnp.clip(q0 - DA_HALF, 0, n_sub - nk)
                block(g, dil, r + dil * q0, r + dil * k0, nk, (q0 - k0) // DA_HALF)
                return carry
            lax.fori_loop(0, dil * nb, body, 0, unroll=DA_UNROLL)

    lse_all = jnp.maximum(jnp.maximum(lse_s[0], lse_s[1]), lse_s[2])
    num = jnp.zeros((SEQ, LANES), F32)
    den = jnp.zeros((SEQ, LANES), F32)
    for g in range(len(DA_BRANCH_DIL)):
        w = jnp.exp2(lse_s[g] - lse_all)
        num = num + w * acc_s[g]
        den = den + w
    o_ref[0] = (num / den).astype(o_ref.dtype)


def dilated_attention(proj, mask_t):
    b = proj.shape[0]
    n_hp = DA_HEADS // 2
    blk = lambda off: pl.BlockSpec((1, SEQ, LANES), lambda i, hp, off=off: (i, 0, off + hp))
    nbr = len(DA_BRANCH_DIL)
    return pl.pallas_call(
        _da_kernel, grid=(b, n_hp),
        in_specs=[blk(0), blk(n_hp), blk(2 * n_hp),
                  pl.BlockSpec(mask_t.shape, lambda i, hp: (0, 0, 0))],
        out_specs=pl.BlockSpec((1, SEQ, LANES), lambda i, hp: (i, 0, hp)),
        out_shape=jax.ShapeDtypeStruct((b, SEQ, DA_W), BF16),
        scratch_shapes=[pltpu.VMEM((SEQ, LANES), F32)] * 3 + [pltpu.VMEM((nbr, SEQ, LANES), F32)] * 2,
        compiler_params=_cparams("parallel", "parallel"), name="dilated_attention",
    )(proj, proj, proj, mask_t)


ML_SCALE = ML_HEAD_DIM ** -0.5
GATE_I_FWD, GATE_I_BWD, GATE_F_FWD, GATE_F_BWD = 0, 4, 8, 12


def _exact_ones_matmul(ones_bf16, x):
    hi = x.astype(BF16)
    r1 = x - hi.astype(F32)
    mid = r1.astype(BF16)
    lo = (r1 - mid.astype(F32)).astype(BF16)
    dot = lambda t: jnp.dot(ones_bf16, t, preferred_element_type=F32)
    return dot(hi) + dot(mid) + dot(lo)


_NT = (((1,), (1,)), ((), ()))
ML_COMBOS = 2 * ML_HEADS
ML_HEADS_PER_TRIP = 4
ROW_CUM, ROW_RMAX, ROW_LOGW, ROW_TOT, ROW_LWMAX = range(5)


def _hi_lo_rows(row):
    hi = row.astype(BF16)
    lo = (row - hi.astype(F32)).astype(BF16)
    return jnp.concatenate([hi, lo, jnp.zeros((SUBLANES - 2, row.shape[1]), BF16)], axis=0)


def _mlstm_chunk(q, k, v_t, log_d_t, rows, state):
    c_t, n_vec, m_run = state
    cum, rmax, logw, tot, lwmax = rows
    log_inter = cum + m_run
    m_t = jnp.maximum(log_inter, rmax)
    w_inter = jnp.exp(log_inter - m_t)
    s_t = lax.dot_general(k, q, _NT, preferred_element_type=F32) * ML_SCALE * jnp.exp(log_d_t - m_t)
    num = w_inter * lax.dot_general(c_t.astype(BF16), q, _NT, preferred_element_type=F32)
    num = num + jnp.dot(v_t, s_t.astype(BF16), preferred_element_type=F32)
    nq = lax.dot_general(_hi_lo_rows(n_vec), q, _NT, preferred_element_type=F32)
    den = w_inter * (nq[0:1] + nq[1:2]) + jnp.sum(s_t, axis=0, keepdims=True)
    h_t = num / jnp.maximum(jnp.abs(den), jnp.exp(-m_t))
    m_new = jnp.maximum(tot + m_run, lwmax)
    w_row = jnp.exp(logw - m_new) * ML_SCALE
    decay = jnp.exp(tot + m_run - m_new)
    vw_t = (v_t.astype(F32) * w_row).astype(BF16)
    c_new = decay * c_t + jnp.dot(vw_t, k, preferred_element_type=F32)
    nk = jnp.dot(_hi_lo_rows(w_row), k, preferred_element_type=F32)
    n_new = decay * n_vec + nk[0:1] + nk[1:2]
    return h_t, (c_new, n_new, m_new)


def _mlstm_kernel(q_ref, k_ref, v_ref, og_ref, g_ref, gb_ref, ng_ref, out_ref, vt_s, ldt_s, rows_s, hf_s, hb_s):
    L = ML_CHUNK
    assert ML_HEAD_DIM == L == LANES
    ri = lax.broadcasted_iota(jnp.int32, (L, L), 0)
    ci = lax.broadcasted_iota(jnp.int32, (L, L), 1)
    lower = ci <= ri
    upper = ci >= ri
    lower_m = jnp.where(lower, 1.0, 0.0).astype(BF16)
    upper_m = jnp.where(upper, 1.0, 0.0).astype(BF16)
    lane = lax.broadcasted_iota(jnp.int32, (L, LANES), 1)
    fwd_rows = lax.broadcasted_iota(jnp.int32, (ML_COMBOS, L), 0) < ML_HEADS

    def prep(c, carry):
        rows = pl.ds(pl.multiple_of(c * L, L), L)
        vt_s[c] = v_ref[0, rows, :].T
        g = g_ref[0, rows, :] + gb_ref[...]
        lf = jnp.minimum(g, 0.0) - jnp.log1p(jnp.exp(-jnp.abs(g)))
        cum_f = _exact_ones_matmul(lower_m, lf)
        suf_b = _exact_ones_matmul(upper_m, lf)
        colv = jnp.where(lane < GATE_F_FWD, g, jnp.where(lane < GATE_F_BWD, cum_f, suf_b))
        rowv = colv.T
        i8 = rowv[0:ML_COMBOS]
        cum8 = rowv[ML_COMBOS:2 * ML_COMBOS]
        r_all = colv - pltpu.roll(colv, LANES - ML_COMBOS, 1)
        tot8 = jnp.where(fwd_rows, jnp.broadcast_to(cum8[:, L - 1:L], (ML_COMBOS, L)),
                         jnp.broadcast_to(cum8[:, 0:1], (ML_COMBOS, L)))
        logw8 = tot8 - cum8 + i8
        lwmax8 = jnp.broadcast_to(jnp.max(logw8, axis=-1, keepdims=True), (ML_COMBOS, L))
        rmax = []
        for j in range(ML_COMBOS):
            valid = upper if j < ML_HEADS else lower
            ldt = jnp.where(valid, jnp.broadcast_to(r_all[:, j:j + 1], (L, L)) + cum8[j:j + 1, :], NEG)
            ldt_s[c * ML_COMBOS + j] = ldt
            rmax.append(jnp.max(ldt, axis=0, keepdims=True))
        rows_s[c, ROW_CUM] = cum8
        rows_s[c, ROW_RMAX] = jnp.concatenate(rmax, axis=0)
        rows_s[c, ROW_LOGW] = logw8
        rows_s[c, ROW_TOT] = tot8
        rows_s[c, ROW_LWMAX] = lwmax8
        return carry

    lax.fori_loop(0, N_CHUNKS, prep, 0)

    zero_state = (jnp.zeros((ML_HEAD_DIM, ML_HEAD_DIM), F32), jnp.zeros((1, ML_HEAD_DIM), F32),
                  jnp.zeros((1, L), F32))
    def one(c, h, backward, state):
        hl = slice(ML_HEAD_DIM * h, ML_HEAD_DIM * (h + 1))
        j = ML_HEADS + h if backward else h
        rows = pl.ds(pl.multiple_of(c * L, L), L)
        prepared = tuple(rows_s[c, k, j:j + 1, :] for k in range(5))
        h_t, state = _mlstm_chunk(q_ref[0, rows, hl], k_ref[0, rows, hl], vt_s[c, hl, :],
                                  ldt_s[c * ML_COMBOS + j], prepared, state)
        (hb_s if backward else hf_s)[h, c] = h_t
        return state

    for h0 in range(0, ML_HEADS, ML_HEADS_PER_TRIP):
        def body(c, states, h0=h0):
            out = []
            for k in range(ML_HEADS_PER_TRIP):
                out.append(one(c, h0 + k, False, states[2 * k]))
                out.append(one(N_CHUNKS - 1 - c, h0 + k, True, states[2 * k + 1]))
            return tuple(out)

        lax.fori_loop(0, N_CHUNKS, body, (zero_state,) * (2 * ML_HEADS_PER_TRIP), unroll=2)

    for h in range(ML_HEADS):
        hl = slice(ML_HEAD_DIM * h, ML_HEAD_DIM * (h + 1))
        gain_t = jnp.broadcast_to(ng_ref[:, hl], (L, ML_HEAD_DIM)).T
        for c in range(N_CHUNKS):
            r0 = c * L
            hh = hf_s[h, c] + hb_s[h, c]
            mu = jnp.mean(hh, axis=0, keepdims=True)
            cen = hh - mu
            var = jnp.mean(cen * cen, axis=0, keepdims=True)
            hn = (cen * lax.rsqrt(var + LN_EPS) * gain_t).T
            og = og_ref[0, r0:r0 + L, hl].astype(F32)
            out_ref[0, r0:r0 + L, hl] = (hn / (1.0 + jnp.exp(-og))).astype(out_ref.dtype)


def mlstm_mixer(proj, gates, gate_bias, norm_g):
    b = proj.shape[0]
    c0 = (3 * NA_W) // ML_W
    blk = lambda off: pl.BlockSpec((1, SEQ, ML_W), lambda i, off=off: (i, 0, off))
    gb = jnp.pad(gate_bias.astype(F32), (0, LANES - gate_bias.shape[0])).reshape(1, LANES)
    return pl.pallas_call(
        _mlstm_kernel, grid=(b,),
        in_specs=[blk(c0), blk(c0 + 1), blk(c0 + 2), blk(c0 + 3),
                  pl.BlockSpec((1, SEQ, LANES), lambda i: (i, 0, 0)),
                  pl.BlockSpec((1, LANES), lambda i: (0, 0)),
                  pl.BlockSpec((1, ML_W), lambda i: (0, 0))],
        out_specs=pl.BlockSpec((1, SEQ, ML_W), lambda i: (i, 0, 0)),
        out_shape=jax.ShapeDtypeStruct((b, SEQ, ML_W), BF16),
        scratch_shapes=[pltpu.VMEM((N_CHUNKS, ML_W, ML_CHUNK), BF16),
                        pltpu.VMEM((N_CHUNKS * ML_COMBOS, ML_CHUNK, ML_CHUNK), F32),
                        pltpu.VMEM((N_CHUNKS, 5, ML_COMBOS, ML_CHUNK), F32),
                        pltpu.VMEM((ML_HEADS, N_CHUNKS, ML_HEAD_DIM, ML_CHUNK), F32),
                        pltpu.VMEM((ML_HEADS, N_CHUNKS, ML_HEAD_DIM, ML_CHUNK), F32)],
        compiler_params=_cparams("parallel"), name="mlstm_mixer",
    )(proj, proj, proj, proj, gates, gb, norm_g.reshape(1, ML_W).astype(F32))


def _router_kernel(x_ref, w_ref, aff_ref):
    x = _load_token_tiles(x_ref, x_ref.shape[0] // TILE_ROWS)
    w = w_ref[...]
    x_hi = x.astype(BF16)
    x_lo = (x - x_hi.astype(F32)).astype(BF16)
    w_hi = w.astype(BF16)
    w_lo = (w - w_hi.astype(F32)).astype(BF16)
    dot = lambda a, b: jnp.dot(a, b, preferred_element_type=F32)
    logits_t = dot(x_hi, w_hi) + dot(x_hi, w_lo) + dot(x_lo, w_hi)
    logits = logits_t.T[0:N_EXPERTS, :]
    z = jnp.exp(logits - jnp.max(logits, axis=0, keepdims=True))
    aff_ref[...] = z / jnp.sum(z, axis=0, keepdims=True)


def router_affinities(x_tiles, w_router, tm=1024):
    n, d = x_tiles.shape[0] // TILE_ROWS, D_MODEL
    w_pad = jnp.pad(w_router.astype(F32), ((0, 0), (0, LANES - N_EXPERTS)))
    return pl.pallas_call(
        _router_kernel, grid=(n // tm,),
        in_specs=[pl.BlockSpec((tm * TILE_ROWS, LANES), lambda i: (i, 0)),
                  pl.BlockSpec((d, LANES), lambda i: (0, 0))],
        out_specs=pl.BlockSpec((N_EXPERTS, tm), lambda i: (0, i)),
        out_shape=jax.ShapeDtypeStruct((N_EXPERTS, n), F32),
        compiler_params=_cparams("parallel"), name="router_affinities")(x_tiles, w_pad)


def _tri_matrices(r):
    li = lax.broadcasted_iota(jnp.int32, (LANES, LANES), 0)
    lj = lax.broadcasted_iota(jnp.int32, (LANES, LANES), 1)
    tri_u = jnp.where(li <= lj, 1.0, 0.0).astype(BF16)
    ri = lax.broadcasted_iota(jnp.int32, (r, r), 0)
    rj = lax.broadcasted_iota(jnp.int32, (r, r), 1)
    tri_l = jnp.where(rj < ri, 1.0, 0.0).astype(BF16)
    return tri_u, tri_l


def _prefix_counts(mask, tri_u, tri_l):
    r = mask.shape[0]
    within = jnp.dot(mask.astype(BF16), tri_u, preferred_element_type=F32)
    rowtot = within[:, LANES - 1:LANES]
    hi = jnp.floor(rowtot * (1.0 / 16.0))
    lo = rowtot - 16.0 * hi
    hi_b = jnp.broadcast_to(hi, (r, LANES)).astype(BF16)
    lo_b = jnp.broadcast_to(lo, (r, LANES)).astype(BF16)
    rowoff = 16.0 * jnp.dot(tri_l, hi_b, preferred_element_type=F32) + jnp.dot(
        tri_l, lo_b, preferred_element_type=F32)
    return within - mask + rowoff, within, rowoff, rowtot


def _select_kernel(aff_ref, sel_ref, *, cap):
    n_e, r, _ = aff_ref.shape
    tri_u, tri_l = _tri_matrices(r)

    def bits(k):
        return pltpu.bitcast(aff_ref[k], jnp.int32)

    def count(m):
        c = jnp.sum(jnp.where(m, 1.0, 0.0), axis=0, keepdims=True)
        return jnp.sum(c, axis=1, keepdims=True)

    def bisect(i, prefixes):
        bit = jnp.left_shift(jnp.int32(1), 30 - i)
        return tuple(jnp.where(count(bits(k) >= (p | bit)) >= cap, p | bit, p)
                     for k, p in enumerate(prefixes))

    thr = lax.fori_loop(0, 31, bisect, (jnp.zeros((1, 1), jnp.int32),) * n_e)
    for k in range(n_e):
        gt = bits(k) > thr[k]
        eq = bits(k) == thr[k]
        need = cap - count(gt)
        rank_eq, _, _, _ = _prefix_counts(jnp.where(eq, 1.0, 0.0), tri_u, tri_l)
        sel_ref[k] = jnp.where(gt | (eq & (rank_eq < need)), 1.0, 0.0)


def select_tokens(aff3, cap):
    e, r, _ = aff3.shape
    blk = pl.BlockSpec((e, r, LANES), lambda i: (0, 0, 0))
    return pl.pallas_call(
        functools.partial(_select_kernel, cap=cap), grid=(1,), in_specs=[blk], out_specs=blk,
        out_shape=jax.ShapeDtypeStruct((e, r, LANES), F32),
        compiler_params=_cparams("arbitrary"), name="select_tokens")(aff3)


def _lists_kernel(sel_ref, aff_ref, idx_ref, dst_ref, gate_ref, ts_ref, ts_s, er_s, *, cap, st):
    e = pl.program_id(0)
    r = sel_ref.shape[1]
    tri_u, tri_l = _tri_matrices(r)

    @pl.when(e == 0)
    def _():
        cnt = sel_ref[0]
        for k in range(1, N_EXPERTS):
            cnt = cnt + sel_ref[k]
        ts, _, _, _ = _prefix_counts(cnt, tri_u, tri_l)
        ts_s[...] = ts
        ts_ref[...] = ts
        er_s[...] = jnp.zeros_like(er_s)

    sel = sel_ref[e]
    _, within, rowoff, rowtot = _prefix_counts(sel, tri_u, tri_l)
    dst = ts_s[...] + er_s[...]
    er_s[...] = er_s[...] + sel

    d2 = jnp.floor(dst * (1.0 / 65536.0))
    rem = dst - 65536.0 * d2
    d1 = jnp.floor(rem * (1.0 / 256.0))
    d0 = rem - 256.0 * d1
    aff = aff_ref[0]
    a_hi = aff.astype(BF16)
    a_r1 = aff - a_hi.astype(F32)
    a_mid = a_r1.astype(BF16)
    a_lo = (a_r1 - a_mid.astype(F32)).astype(BF16)
    rhs = jnp.concatenate([within.astype(BF16), d0.astype(BF16), d1.astype(BF16), d2.astype(BF16),
                           a_hi, a_mid, a_lo], axis=1)
    rowoff_row = rowoff.T[0:1, :]
    rowend_row = rowoff_row + jnp.broadcast_to(rowtot, (r, LANES)).T[0:1, :]
    rho_row = lax.broadcasted_iota(jnp.int32, (1, r), 1).astype(F32)
    lane = lax.broadcasted_iota(jnp.int32, (st, LANES), 1).astype(F32)
    eye = lax.broadcasted_iota(jnp.int32, (LANES, LANES), 0) == lax.broadcasted_iota(
        jnp.int32, (LANES, LANES), 1)

    for t in range(cap // st):
        s_col = (t * st + lax.broadcasted_iota(jnp.int32, (st, 1), 0)).astype(F32)
        in_row = (rowoff_row <= s_col) & (s_col < rowend_row)
        got = jnp.dot(jnp.where(in_row, 1.0, 0.0).astype(BF16), rhs, preferred_element_type=F32)
        base = jnp.sum(jnp.where(in_row, rowoff_row, 0.0), axis=1, keepdims=True)
        rho = jnp.sum(jnp.where(in_row, rho_row, 0.0), axis=1, keepdims=True)
        local = s_col - base
        lam = jnp.sum(jnp.where(got[:, 0:LANES] <= local, 1.0, 0.0), axis=1, keepdims=True)
        plane = lambda k: got[:, k * LANES:(k + 1) * LANES]
        pair = plane(1) + 256.0 * plane(2) + 65536.0 * plane(3)
        at_lam = lane == lam
        dval = jnp.sum(jnp.where(at_lam, pair, 0.0), axis=1, keepdims=True)
        gate_ref[0, t * st:(t + 1) * st, :] = jnp.sum(
            jnp.where(at_lam, plane(4) + plane(5) + plane(6), 0.0), axis=1, keepdims=True)
        ival = rho * float(LANES) + lam
        for j in range(st // LANES):
            seg = slice(j * LANES, (j + 1) * LANES)
            row = t * (st // LANES) + j
            idx_ref[0, row:row + 1, :] = jnp.sum(
                jnp.where(eye, ival[seg], 0.0), axis=0, keepdims=True).astype(jnp.int32)
            dst_ref[0, row:row + 1, :] = jnp.sum(
                jnp.where(eye, dval[seg], 0.0), axis=0, keepdims=True).astype(jnp.int32)


def build_lists(sel3, aff3, cap):
    e, r, _ = sel3.shape
    st = min(512, cap)
    lst = pl.BlockSpec((1, cap // LANES, LANES), lambda i: (i, 0, 0))
    return pl.pallas_call(
        functools.partial(_lists_kernel, cap=cap, st=st), grid=(e,),
        in_specs=[pl.BlockSpec((e, r, LANES), lambda i: (0, 0, 0)),
                  pl.BlockSpec((1, r, LANES), lambda i: (i, 0, 0))],
        out_specs=[lst, lst, pl.BlockSpec((1, cap, 1), lambda i: (i, 0, 0)),
                   pl.BlockSpec((r, LANES), lambda i: (0, 0))],
        out_shape=[jax.ShapeDtypeStruct((e, cap // LANES, LANES), jnp.int32),
                   jax.ShapeDtypeStruct((e, cap // LANES, LANES), jnp.int32),
                   jax.ShapeDtypeStruct((e, cap, 1), F32), jax.ShapeDtypeStruct((r, LANES), F32)],
        scratch_shapes=[pltpu.VMEM((r, LANES), F32), pltpu.VMEM((r, LANES), F32)],
        compiler_params=_cparams("arbitrary"), name="build_lists")(sel3, aff3)


FFN_CHUNKS = tuple((f, min(f + 256, D_FF)) for f in range(0, D_FF, 256))


def _ffn_kernel(idx_first, idx_next_a, idx_next_b, dst_prev_a, dst_prev_b, dst_last,
                x_hbm, gate_a, gate_b, wg_ref, wu_ref, wd_ref, z_hbm,
                xbuf0, xbuf1, ybuf0, ybuf1, acc, gsem, ssem, *, tm, n_grid):
    g = pl.program_id(0)

    def token_tile(t):
        if isinstance(t, int):
            return pl.ds(t * TILE_ROWS, TILE_ROWS)
        return pl.ds(pl.multiple_of(t * TILE_ROWS, TILE_ROWS), TILE_ROWS)

    def gather_row(ids, i, buf, sem):
        pltpu.make_async_copy(x_hbm.at[token_tile(ids[0, 0, i]), :], buf.at[token_tile(i), :],
                              sem).start(priority=1)

    def scatter_row(dsts, i, buf, sem):
        pltpu.make_async_copy(buf.at[token_tile(i), :], z_hbm.at[token_tile(dsts[0, 0, i]), :],
                              sem).start(priority=0)

    def wait_rows(buf, sem):
        pltpu.make_async_copy(x_hbm.at[pl.ds(0, tm * TILE_ROWS), :], buf, sem).wait()

    @pl.when(g == 0)
    def _():
        ybuf1[...] = jnp.zeros_like(ybuf1)

        def first(i, carry):
            gather_row(idx_first, i, xbuf0, gsem.at[0])
            return carry
        lax.fori_loop(0, tm, first, 0, unroll=8)

    def tile(xcur, gcur, xnext, gnext, idx_next, yprev, sprev, dst_prev):
        for i in range(tm):
            gather_row(idx_next, i, xnext, gnext)
            scatter_row(dst_prev, i, yprev, sprev)

        wait_rows(xcur, gcur)
        xb = _load_token_tiles(xcur, tm).astype(BF16)
        for c, (f0, f1) in enumerate(FFN_CHUNKS):
            gate = jnp.dot(xb, wg_ref[0, :, f0:f1], preferred_element_type=F32)
            up = jnp.dot(xb, wu_ref[0, :, f0:f1], preferred_element_type=F32)
            h = (gate / (1.0 + jnp.exp(-gate)) * up).astype(BF16)
            part = jnp.dot(h, wd_ref[0, f0:f1, :], preferred_element_type=F32)
            if c == 0:
                acc[...] = part
            else:
                acc[...] += part

    tile(xbuf0, gsem.at[0], xbuf1, gsem.at[1], idx_next_a, ybuf1, ssem.at[1], dst_prev_a)

    @pl.when(g >= 1)
    def _():
        wait_rows(ybuf0, ssem.at[0])
    _store_token_tiles(ybuf0, acc[...] * gate_a[0])

    tile(xbuf1, gsem.at[1], xbuf0, gsem.at[0], idx_next_b, ybuf0, ssem.at[0], dst_prev_b)
    wait_rows(ybuf1, ssem.at[1])
    _store_token_tiles(ybuf1, acc[...] * gate_b[0])

    @pl.when(g == n_grid - 1)
    def _():
        def last(i, carry):
            scatter_row(dst_last, i, ybuf1, ssem.at[1])
            return carry
        lax.fori_loop(0, tm, last, 0, unroll=8)
        wait_rows(ybuf1, ssem.at[1])
        wait_rows(ybuf0, ssem.at[0])
        wait_rows(xbuf0, gsem.at[0])


def expert_ffn(x_tiles, idx, dst, gate, w_gate, w_up, w_down, tm=512):
    d = D_MODEL
    e, cap = idx.shape
    tm = min(tm, cap // 2)
    nt = cap // tm
    assert nt % 2 == 0
    n_tiles = e * nt
    n_grid = n_tiles // 2
    idx3 = idx.reshape(n_tiles, 1, tm)
    gate3 = gate.reshape(n_tiles, tm, 1)
    spare =(e * cap + jnp.arange(tm, dtype=jnp.int32)).reshape(1, 1, tm)
    dst3 = jnp.concatenate([spare, dst.reshape(n_tiles, 1, tm)])
    smem = lambda imap: pl.BlockSpec((1, 1, tm), imap, memory_space=pltpu.SMEM)
    wspec = lambda w: pl.BlockSpec((1,) + w.shape[1:], lambda i: ((2 * i) // nt, 0, 0))
    return pl.pallas_call(
        functools.partial(_ffn_kernel, tm=tm, n_grid=n_grid), grid=(n_grid,),
        in_specs=[smem(lambda i: (0, 0, 0)), smem(lambda i: (2 * i + 1, 0, 0)),
                  smem(lambda i: (jnp.minimum(2 * i + 2, n_tiles - 1), 0, 0)),
                  smem(lambda i: (2 * i, 0, 0)), smem(lambda i: (2 * i + 1, 0, 0)),
                  smem(lambda i: (n_tiles, 0, 0)),
                  pl.BlockSpec(memory_space=pl.ANY),
                  pl.BlockSpec((1, tm, 1), lambda i: (2 * i, 0, 0)),
                  pl.BlockSpec((1, tm, 1), lambda i: (2 * i + 1, 0, 0)),
                  wspec(w_gate), wspec(w_up), wspec(w_down)],
        out_specs=pl.BlockSpec(memory_space=pl.ANY),
        out_shape=jax.ShapeDtypeStruct(((e * cap + tm) * TILE_ROWS, LANES), F32),
        scratch_shapes=[pltpu.VMEM((tm * TILE_ROWS, LANES), F32)] * 4 + [pltpu.VMEM((tm, d), F32)]
        + [pltpu.SemaphoreType.DMA((2,)), pltpu.SemaphoreType.DMA((2,))],
        compiler_params=_cparams("arbitrary"), name="expert_ffn",
    )(idx3, idx3, idx3, dst3, dst3, dst3, x_tiles, gate3, gate3, w_gate, w_up, w_down)


COMBINE_ZB = 640
COMBINE_SLOTS = 4


def _combine_kernel(ts_ref, x_ref, run_ref, g_ref, b_ref, z_hbm, o_ref, zbuf, sem, used, *, z_rows, n_tiles):
    i = pl.program_id(0)
    tt = x_ref.shape[0] // TILE_ROWS

    def chunk_rows(t, c):
        lo = ts_ref[t] + c * COMBINE_ZB
        return lo, jnp.minimum(lo, z_rows - COMBINE_ZB)

    def piece_copy(start, j, slot):
        return pltpu.make_async_copy(z_hbm.at[pl.ds(start, COMBINE_ZB), j, :], zbuf.at[slot, j], sem.at[slot])

    def chunks_of(t):
        return jnp.maximum((ts_ref[t + 1] - ts_ref[t] + COMBINE_ZB - 1) // COMBINE_ZB, 1)

    def fetch_next():
        t = used[2]

        @pl.when(t < n_tiles)
        def _():
            c = used[3]
            slot = used[1] % COMBINE_SLOTS
            _, start = chunk_rows(t, c)
            for j in range(TILE_ROWS):
                piece_copy(start, j, slot).start()
            used[1] = used[1] + 1
            last = c + 1 >= chunks_of(t)
            used[2] = jnp.where(last, t + 1, t)
            used[3] = jnp.where(last, 0, c + 1)

    @pl.when(i == 0)
    def _():
        for k in range(4):
            used[k] = 0
        for _ in range(COMBINE_SLOTS - 1):
            fetch_next()

    base = used[0]
    n_chunks = chunks_of(i)
    run_lo = jnp.broadcast_to(run_ref[:, 0:1], (tt, COMBINE_ZB))
    run_hi = jnp.broadcast_to(run_ref[:, 1:2], (tt, COMBINE_ZB))
    col = lax.broadcasted_iota(jnp.int32, (1, COMBINE_ZB), 1).astype(F32)

    def chunk(c, acc):
        slot = (base + c) % COMBINE_SLOTS
        lo, start = chunk_rows(i, c)
        pair = col + start.astype(F32)
        pair = jnp.where(pair >= lo.astype(F32), pair, -1.0)
        a = jnp.where((run_lo <= pair) & (pair < run_hi), 1.0, 0.0).astype(BF16)
        for j in range(TILE_ROWS):
            piece_copy(0, j, slot).wait()
        rows = jnp.concatenate([zbuf[slot, j] for j in range(TILE_ROWS)], axis=1).astype(BF16)
        acc = acc + jnp.dot(a, rows, preferred_element_type=F32)
        fetch_next()
        return acc

    ffn = lax.fori_loop(0, n_chunks, chunk, jnp.zeros((tt, D_MODEL), F32))
    used[0] = base + n_chunks
    o_ref[...] = _layer_norm_rows(DN_ALPHA * _load_token_tiles(x_ref, tt) + ffn, g_ref[...], b_ref[...])


def combine_ln(x_tiles, z, tile_start, runs, g, b, tt=256):
    n, d = x_tiles.shape[0] // TILE_ROWS, D_MODEL
    n_tiles = n // tt
    row = lambda i, ts: (i, 0)
    fixed = lambda i, ts: (0, 0)
    grid_spec = pltpu.PrefetchScalarGridSpec(
        num_scalar_prefetch=1, grid=(n_tiles,),
        in_specs=[pl.BlockSpec((tt * TILE_ROWS, LANES), row), pl.BlockSpec((tt, 2), row),
                  pl.BlockSpec((1, d), fixed),
                  pl.BlockSpec((1, d), fixed), pl.BlockSpec(memory_space=pl.ANY)],
        out_specs=pl.BlockSpec((tt, d), row),
        scratch_shapes=[pltpu.VMEM((COMBINE_SLOTS, TILE_ROWS, COMBINE_ZB, LANES), F32),
                        pltpu.SemaphoreType.DMA((COMBINE_SLOTS,)), pltpu.SMEM((4,), jnp.int32)])
    z_rows = z.shape[0] // TILE_ROWS
    return pl.pallas_call(
        functools.partial(_combine_kernel, z_rows=z_rows, n_tiles=n_tiles), grid_spec=grid_spec,
        out_shape=jax.ShapeDtypeStruct((n, d), F32),
        compiler_params=_cparams("arbitrary"), name="combine_ln",
    )(tile_start, x_tiles, runs, g.reshape(1, d), b.reshape(1, d), z.reshape(z_rows, TILE_ROWS, LANES))


def moe_layer(x_tiles, w_router, w_gate, w_up, w_down, g, b, tt=256):
    n = x_tiles.shape[0] // TILE_ROWS
    r = n // LANES
    cap = 2 * n // N_EXPERTS
    aff3 = router_affinities(x_tiles, w_router).reshape(N_EXPERTS, r, LANES)
    sel3 = select_tokens(aff3, cap)
    idx, dst, gate, ts = build_lists(sel3, aff3, cap)
    z = expert_ffn(x_tiles, idx.reshape(N_EXPERTS, cap), dst.reshape(N_EXPERTS, cap), gate, w_gate, w_up, w_down)
    ts_ext = jnp.concatenate([ts.reshape(n), jnp.full((1,), N_EXPERTS * cap, F32)])
    tile_start = ts_ext[::tt].astype(jnp.int32)
    runs = jnp.stack([ts_ext[:-1], ts_ext[1:]], axis=1)
    return combine_ln(x_tiles, z, tile_start, runs, g, b, tt=tt)


def kernel(x_prompt, x_sample, even_w_in, ml_gate_bias, na_rpb, ml_norm_g, even_w_out, da_w_in, da_w_out,
           ln_mix_g, ln_mix_b, ec_router, ec_w_gate, ec_w_up, ec_w_down, ln_ffn_g, ln_ffn_b):
    n_mix_cols = 3 * NA_W + 4 * ML_W
    n_gates = 4 * ML_HEADS
    w_even = even_w_in[0][:, :n_mix_cols].astype(BF16)
    w_gates = jnp.pad(even_w_in[0][:, n_mix_cols:], ((0, 0), (0, LANES - n_gates))).astype(BF16)
    w_out_a = even_w_out[0][:NA_W].astype(BF16)
    w_out_b = even_w_out[0][NA_W:].astype(BF16)
    w_odd = da_w_in[0].astype(BF16)
    w_odd_out = da_w_out[0].astype(BF16)
    tbl = na_bias_table(na_rpb[0])
    rope_t = rope_tables()
    mask_t = da_mask_table()
    moe_w = [(ec_router[l], ec_w_gate[l].astype(BF16), ec_w_up[l].astype(BF16), ec_w_down[l].astype(BF16),
              ln_ffn_g[l], ln_ffn_b[l]) for l in range(DEPTH)]

    def trunk(x):
        b = x.shape[0]
        xt = x.reshape(b * SEQ, D_MODEL)
        proj, gates = in_projection_gates(xt, w_even, w_gates)
        proj = proj.reshape(b, SEQ, -1)
        ya = neighbourhood_attention(proj, tbl)
        yb = mlstm_mixer(proj, gates.reshape(b, SEQ, LANES), ml_gate_bias[0], ml_norm_g[0])
        xt_tiles = out_projection_ln(xt, [ya.reshape(b * SEQ, NA_W), yb.reshape(b * SEQ, ML_W)],
                                     [w_out_a, w_out_b], ln_mix_g[0], ln_mix_b[0])
        xt = moe_layer(xt_tiles, *moe_w[0])
        proj = in_projection_rope(xt, w_odd, rope_t, DA_W).reshape(b, SEQ, -1)
        yc = dilated_attention(proj, mask_t)
        xt_tiles = out_projection_ln(xt, [yc.reshape(b * SEQ, DA_W)], [w_odd_out], ln_mix_g[1], ln_mix_b[1])
        xt = moe_layer(xt_tiles, *moe_w[1])
        return xt.reshape(b, SEQ, D_MODEL)

    return trunk(x_prompt), trunk(x_sample)
```

```python
import functools

import numpy as np
import jax
import jax.numpy as jnp
from jax import lax
from jax.experimental import pallas as pl
from jax.experimental.pallas import tpu as pltpu

F32 = jnp.float32
BF16 = jnp.bfloat16

D_MODEL = 1024
SEQ = 2048
GRID_W = 64
GRID_ROWS = SEQ // GRID_W
NA_HEADS = 8
NA_W = 512
NA_WIN_R = 8
NA_WIN_C = 16
ML_HEADS = 4
ML_HEAD_DIM = 128
ML_W = 512
ML_CHUNK = 128
N_CHUNKS = SEQ // ML_CHUNK
DA_HEADS = 16
DA_W = 1024
DA_HALF = 64
ROPE_THETA = 10000.0
N_EXPERTS = 16
D_FF = 1408
LN_EPS = 1e-5
DEPTH = 2
DN_ALPHA = (2 * DEPTH) ** 0.25
LANES = 128
ATT_HEAD_DIM = 64
ROT_HALF = ATT_HEAD_DIM // 2
NEG = -1e30
LOG2E = 1.4426950408889634
LN2 = 0.6931471805599453
VMEM_LIMIT = 56 * 1024 * 1024


def _cparams(*sem):
    return pltpu.CompilerParams(dimension_semantics=sem, vmem_limit_bytes=VMEM_LIMIT)


def _inproj_gates_kernel(x_ref, w_ref, wg_ref, o_ref, g_ref, *, n_chunk):
    xb = x_ref[...].astype(BF16)
    for c in range(0, w_ref.shape[1], n_chunk):
        o_ref[:, c:c + n_chunk] = jnp.dot(
            xb, w_ref[:, c:c + n_chunk], preferred_element_type=F32).astype(o_ref.dtype)
    g_ref[...] = jnp.dot(xb, wg_ref[...], preferred_element_type=F32)


def _rotate_half_pairs(x, cos, sin_signed):
    first_half = lax.broadcasted_iota(jnp.int32, x.shape, 1) % ATT_HEAD_DIM < ROT_HALF
    swapped = jnp.where(first_half, pltpu.roll(x, LANES - ROT_HALF, 1), pltpu.roll(x, ROT_HALF, 1))
    return x * cos + swapped * sin_signed


def _inproj_rope_kernel(x_ref, w_ref, rope_ref, o_ref, *, n_chunk, width):
    xb = x_ref[...].astype(BF16)
    for c in range(0, w_ref.shape[1], n_chunk):
        res = jnp.dot(xb, w_ref[:, c:c + n_chunk], preferred_element_type=F32)
        if c < 2 * width:
            t = 0 if c < width else 2
            res = jnp.concatenate(
                [_rotate_half_pairs(res[:, l:l + LANES], rope_ref[t], rope_ref[t + 1])
                 for l in range(0, n_chunk, LANES)], axis=1)
        o_ref[:, c:c + n_chunk] = res.astype(o_ref.dtype)


def in_projection_rope(x, w, rope_tables, width, tm=512, n_chunk=512):
    m, k = x.shape
    n = w.shape[1]
    assert width % n_chunk == 0 and SEQ % tm == 0
    return pl.pallas_call(
        functools.partial(_inproj_rope_kernel, n_chunk=n_chunk, width=width),
        grid=(m // tm,),
        in_specs=[pl.BlockSpec((tm, k), lambda i: (i, 0)), pl.BlockSpec((k, n), lambda i: (0, 0)),
                  pl.BlockSpec((4, tm, LANES), lambda i: (0, i % (SEQ // tm), 0))],
        out_specs=pl.BlockSpec((tm, n), lambda i: (i, 0)),
        out_shape=jax.ShapeDtypeStruct((m, n), BF16),
        compiler_params=_cparams("parallel"), name="in_projection_rope")(x, w, rope_tables)


def in_projection_gates(x, w, wg, tm=512, n_chunk=512):
    m, k = x.shape
    n = w.shape[1]
    x_spec = pl.BlockSpec((tm, k), lambda i: (i, 0))
    w_spec = pl.BlockSpec((k, n), lambda i: (0, 0))
    o_spec = pl.BlockSpec((tm, n), lambda i: (i, 0))
    return pl.pallas_call(
        functools.partial(_inproj_gates_kernel, n_chunk=n_chunk),
        grid=(m // tm,),
        in_specs=[x_spec, w_spec, pl.BlockSpec((k, LANES), lambda i: (0, 0))],
        out_specs=[o_spec, pl.BlockSpec((tm, LANES), lambda i: (i, 0))],
        out_shape=[jax.ShapeDtypeStruct((m, n), BF16), jax.ShapeDtypeStruct((m, LANES), F32)],
        compiler_params=_cparams("parallel"), name="in_projection_gates")(x, w, wg)


def _layer_norm_rows(acc, g, b):
    mu = jnp.mean(acc, axis=-1, keepdims=True)
    cen = acc - mu
    var = jnp.mean(cen * cen, axis=-1, keepdims=True)
    return cen * lax.rsqrt(var + LN_EPS) * g + b


def _outproj_ln_kernel(*refs, n_mix):
    x_ref = refs[0]
    mix_refs = refs[1:1 + n_mix]
    w_refs = refs[1 + n_mix:1 + 2 * n_mix]
    g_ref, b_ref, ot_ref = refs[1 + 2 * n_mix:]
    acc = DN_ALPHA * x_ref[...]
    for m_ref, w_ref in zip(mix_refs, w_refs):
        acc = acc + jnp.dot(m_ref[...], w_ref[...], preferred_element_type=F32)
    _store_token_tiles(ot_ref, _layer_norm_rows(acc, g_ref[...], b_ref[...]))


SUBLANES = 8
TILE_ROWS = D_MODEL // LANES


def _store_token_tiles(ref, rows):
    n = rows.shape[0]
    for j in range(TILE_ROWS):
        ref[pl.ds(j, n, stride=TILE_ROWS), :] = rows[:, j * LANES:(j + 1) * LANES]


def _load_token_tiles(ref, n):
    return jnp.concatenate([ref[pl.ds(j, n, stride=TILE_ROWS), :] for j in range(TILE_ROWS)], axis=1)


def out_projection_ln(x, mixes, ws, g, b, tm=512):
    m, d = x.shape
    n_mix = len(mixes)
    row = lambda i: (i, 0)
    fixed = lambda i: (0, 0)
    in_specs = [pl.BlockSpec((tm, d), row)]
    in_specs += [pl.BlockSpec((tm, mx.shape[1]), row) for mx in mixes]
    in_specs += [pl.BlockSpec(w.shape, fixed) for w in ws]
    in_specs += [pl.BlockSpec((1, d), fixed), pl.BlockSpec((1, d), fixed)]
    return pl.pallas_call(
        functools.partial(_outproj_ln_kernel, n_mix=n_mix),
        grid=(m // tm,), in_specs=in_specs,
        out_specs=pl.BlockSpec((tm * TILE_ROWS, LANES), row),
        out_shape=jax.ShapeDtypeStruct((m * TILE_ROWS, LANES), F32),
        compiler_params=_cparams("parallel"), name="out_projection_ln",
    )(x, *mixes, *ws, g.reshape(1, d), b.reshape(1, d))


NA_PAIR_ROWS = 10
NA_PAIR_KEYS = NA_PAIR_ROWS * GRID_W
NA_PAIR_CASES = 5
NA_N_PAIRS = GRID_ROWS // 2


def _na_pair_window(p):
    start = jnp.minimum(jnp.clip(2 * p - NA_WIN_R // 2, 0, GRID_ROWS - NA_WIN_R), GRID_ROWS - NA_PAIR_ROWS)
    case = jnp.minimum(p, 2) + jnp.maximum(p - (NA_N_PAIRS - 3), 0)
    return start, case


def na_bias_table(rpb):
    j = np.arange(GRID_W)
    kc = np.arange(GRID_W)
    win_c0 = np.clip(j - NA_WIN_C // 2, 0, GRID_W - NA_WIN_C)
    valid = (kc[None, :] >= win_c0[:, None]) & (kc[None, :] < win_c0[:, None] + NA_WIN_C)
    dc = np.clip(kc[None, :] - j[:, None] + NA_WIN_C - 1, 0, 2 * NA_WIN_C - 2)
    n_dc = 2 * NA_WIN_C - 1
    pick = jnp.asarray(dc[:, :, None] == np.arange(n_dc)[None, None, :], F32)
    rows = jnp.einsum('hrd,jkd->hrjk', rpb.astype(F32), pick, precision=lax.Precision.HIGHEST)
    rows = jnp.where(valid[None, None], rows * LOG2E, NEG)
    masked = jnp.full_like(rows[:, 0], NEG)
    representative = (0, 1, 2, NA_N_PAIRS - 2, NA_N_PAIRS - 1)
    cases = []
    for p in representative:
        start = min(max(2 * p - NA_WIN_R // 2, 0), GRID_ROWS - NA_WIN_R, GRID_ROWS - NA_PAIR_ROWS)
        per_row = []
        for r in (2 * p, 2 * p + 1):
            rs = min(max(r - NA_WIN_R // 2, 0), GRID_ROWS - NA_WIN_R)
            per_row.append(jnp.stack(
                [rows[:, start + w - r + NA_WIN_R - 1] if rs <= start + w < rs + NA_WIN_R else masked
                 for w in range(NA_PAIR_ROWS)], axis=1))
        cases.append(jnp.stack(per_row, axis=1))
    t = jnp.stack(cases, axis=1)
    t = t.transpose(0, 1, 2, 4, 3, 5).reshape(NA_HEADS // 2, 2, NA_PAIR_CASES, 2 * GRID_W, NA_PAIR_KEYS)
    return t.transpose(0, 2, 1, 3, 4).reshape(NA_HEADS // 2, NA_PAIR_CASES, 4 * GRID_W, NA_PAIR_KEYS)


NA_SCORE_SCALE = ATT_HEAD_DIM ** -0.5 * LOG2E
NA_UNROLL = 16


def _na_kernel(q_ref, k_ref, v_ref, tbl_ref, o_ref, v2a, v2b):
    nq = 2 * GRID_W
    head0 = lax.broadcasted_iota(jnp.int32, (nq, LANES), 1) < ATT_HEAD_DIM

    in0 = lax.broadcasted_iota(jnp.int32, (SEQ, LANES), 1) < ATT_HEAD_DIM
    vf = v_ref[0].astype(F32)
    v2a[...] = jnp.concatenate([jnp.where(in0, vf, 0.0), jnp.where(in0, 1.0, 0.0)], axis=1).astype(BF16)
    v2b[...] = jnp.concatenate([jnp.where(in0, 0.0, vf), jnp.where(in0, 0.0, 1.0)], axis=1).astype(BF16)

    def pair(p, carry):
        start, case = _na_pair_window(p)
        rows = pl.ds(pl.multiple_of(p * nq, nq), nq)
        q = q_ref[0, rows, :]
        zero = jnp.zeros_like(q)
        q2 = jnp.concatenate([jnp.where(head0, q, zero), jnp.where(head0, zero, q)], axis=0)
        win = pl.ds(pl.multiple_of(start * GRID_W, GRID_W), NA_PAIR_KEYS)
        s = lax.dot_general(q2, k_ref[0, win, :], (((1,), (1,)), ((), ())), preferred_element_type=F32)
        s = s * NA_SCORE_SCALE + tbl_ref[0, case]
        pr = jnp.exp2(s - jnp.max(s, axis=-1, keepdims=True)).astype(BF16)
        p2 = jnp.concatenate([pr[:nq], pr[nq:]], axis=1)
        ol = jnp.dot(p2, jnp.concatenate([v2a[win, :], v2b[win, :]], axis=0), preferred_element_type=F32)
        o_ref[0, rows, :] = (ol[:, :LANES] / ol[:, LANES:]).astype(o_ref.dtype)
        return carry

    lax.fori_loop(0, NA_N_PAIRS, pair, 0, unroll=NA_UNROLL)


def neighbourhood_attention(proj, tbl):
    b = proj.shape[0]
    n_hp = NA_HEADS // 2
    blk = lambda off: pl.BlockSpec((1, SEQ, LANES), lambda hp, i, off=off: (i, 0, off + hp))
    return pl.pallas_call(
        _na_kernel, grid=(n_hp, b),
        in_specs=[blk(0), blk(n_hp), blk(2 * n_hp),
                  pl.BlockSpec((1, NA_PAIR_CASES, 4 * GRID_W, NA_PAIR_KEYS), lambda hp, i: (hp, 0, 0, 0))],
        out_specs=pl.BlockSpec((1, SEQ, LANES), lambda hp, i: (i, 0, hp)),
        out_shape=jax.ShapeDtypeStruct((b, SEQ, NA_W), BF16),
        scratch_shapes=[pltpu.VMEM((SEQ, 2 * LANES), BF16)] * 2,
        compiler_params=_cparams("parallel", "parallel"), name="neighbourhood_attention",
    )(proj, proj, proj, tbl)


DA_BRANCH_DIL = (1, 4, 16)
DA_QB = 128


DA_Q_SCALE = ATT_HEAD_DIM ** -0.5 * LOG2E


def rope_tables():
    lane = np.arange(LANES) % ATT_HEAD_DIM
    inv = ROPE_THETA ** (-(2.0 * (lane % ROT_HALF)) / ATT_HEAD_DIM)
    ang = jnp.arange(SEQ, dtype=F32)[:, None] * jnp.asarray(inv, F32)[None, :]
    sign = jnp.asarray(np.where(lane < ROT_HALF, -1.0, 1.0), F32)[None, :]
    cos, sin = jnp.cos(ang), jnp.sin(ang) * sign
    return jnp.stack([cos * DA_Q_SCALE, sin * DA_Q_SCALE, cos, sin])
DA_MASK_CASES = 3
DA_UNROLL = 8


def da_mask_table():
    i = np.arange(2 * DA_QB)[None, :, None] % DA_QB
    j = np.arange(2 * DA_QB)[None, None, :]
    c = np.arange(DA_MASK_CASES)[:, None, None]
    return jnp.asarray(np.where(np.abs(i + DA_HALF * c - j) <= DA_HALF, 0.0, NEG), F32)


def _da_kernel(q_ref, k_ref, v_ref, mask_ref, o_ref, qs, ks, vs, acc_s, lse_s):
    qs[...] = q_ref[0].astype(F32)
    ks[...] = k_ref[0].astype(F32)
    vs[...] = v_ref[0].astype(F32)

    head0 = lax.broadcasted_iota(jnp.int32, (DA_QB, LANES), 1) < ATT_HEAD_DIM

    def block(g, dil, row0, krow0, nk, case):
        qb = qs[pl.ds(row0, DA_QB, stride=dil), :].astype(BF16)
        kb = ks[pl.ds(krow0, nk, stride=dil), :].astype(BF16)
        vb = vs[pl.ds(krow0, nk, stride=dil), :]
        zero = jnp.zeros_like(qb)
        q2 = jnp.concatenate([jnp.where(head0, qb, zero), jnp.where(head0, zero, qb)], axis=0)
        s = lax.dot_general(q2, kb, (((1,), (1,)), ((), ())), preferred_element_type=F32)
        s = s + mask_ref[case, :, 0:nk]
        m = jnp.max(s, axis=-1, keepdims=True)
        p = jnp.exp2(s - m).astype(BF16)
        in0 = lax.broadcasted_iota(jnp.int32, (nk, LANES), 1) < ATT_HEAD_DIM
        v2 = jnp.concatenate([
            jnp.concatenate([jnp.where(in0, vb, 0.0), jnp.where(in0, 1.0, 0.0)], axis=1),
            jnp.concatenate([jnp.where(in0, 0.0, vb), jnp.where(in0, 0.0, 1.0)], axis=1)],
            axis=0).astype(BF16)
        p2 = jnp.concatenate([p[:DA_QB], p[DA_QB:]], axis=1)
        ol = jnp.dot(p2, v2, preferred_element_type=F32)
        l = ol[:, LANES:]
        rows = pl.ds(row0, DA_QB, stride=dil)
        acc_s[g, rows, :] = ol[:, :LANES] / l
        lse_s[g, rows, :] = jnp.where(head0, m[:DA_QB], m[DA_QB:]) + jnp.log(l) * (1.0 / LN2)

    for g, dil in enumerate(DA_BRANCH_DIL):
        n_sub = SEQ // dil
        if n_sub == DA_QB:
            def body(r, carry, g=g, dil=dil):
                block(g, dil, r, r, DA_QB, 0)
                return carry
            lax.fori_loop(0, dil, body, 0, unroll=DA_UNROLL)
        else:
            nb = n_sub // DA_QB
            nk = 2 * DA_QB

            def body(j, carry, g=g, dil=dil, nb=nb, nk=nk, n_sub=n_sub):
                r = j // nb
                q0 = (j % nb) * DA_QB
                k0 = jnp.clip(q0 - DA_HALF, 0, n_sub - nk)
                block(g, dil, r + dil * q0, r + dil * k0, nk, (q0 - k0) // DA_HALF)
                return carry
            lax.fori_loop(0, dil * nb, body, 0, unroll=DA_UNROLL)

    lse_all = jnp.maximum(jnp.maximum(lse_s[0], lse_s[1]), lse_s[2])
    num = jnp.zeros((SEQ, LANES), F32)
    den = jnp.zeros((SEQ, LANES), F32)
    for g in range(len(DA_BRANCH_DIL)):
        w = jnp.exp2(lse_s[g] - lse_all)
        num = num + w * acc_s[g]
        den = den + w
    o_ref[0] = (num / den).astype(o_ref.dtype)


def dilated_attention(proj, mask_t):
    b = proj.shape[0]
    n_hp = DA_HEADS // 2
    blk = lambda off: pl.BlockSpec((1, SEQ, LANES), lambda i, hp, off=off: (i, 0, off + hp))
    nbr = len(DA_BRANCH_DIL)
    return pl.pallas_call(
        _da_kernel, grid=(b, n_hp),
        in_specs=[blk(0), blk(n_hp), blk(2 * n_hp),
                  pl.BlockSpec(mask_t.shape, lambda i, hp: (0, 0, 0))],
        out_specs=pl.BlockSpec((1, SEQ, LANES), lambda i, hp: (i, 0, hp)),
        out_shape=jax.ShapeDtypeStruct((b, SEQ, DA_W), BF16),
        scratch_shapes=[pltpu.VMEM((SEQ, LANES), F32)] * 3 + [pltpu.VMEM((nbr, SEQ, LANES), F32)] * 2,
        compiler_params=_cparams("parallel", "parallel"), name="dilated_attention",
    )(proj, proj, proj, mask_t)


ML_SCALE = ML_HEAD_DIM ** -0.5
GATE_I_FWD, GATE_I_BWD, GATE_F_FWD, GATE_F_BWD = 0, 4, 8, 12


def _exact_ones_matmul(ones_bf16, x):
    hi = x.astype(BF16)
    r1 = x - hi.astype(F32)
    mid = r1.astype(BF16)
    lo = (r1 - mid.astype(F32)).astype(BF16)
    dot = lambda t: jnp.dot(ones_bf16, t, preferred_element_type=F32)
    return dot(hi) + dot(mid) + dot(lo)


_NT = (((1,), (1,)), ((), ()))
ML_COMBOS = 2 * ML_HEADS
ML_HEADS_PER_TRIP = 4
ROW_CUM, ROW_RMAX, ROW_LOGW, ROW_TOT, ROW_LWMAX = range(5)


def _hi_lo_rows(row):
    hi = row.astype(BF16)
    lo = (row - hi.astype(F32)).astype(BF16)
    return jnp.concatenate([hi, lo, jnp.zeros((SUBLANES - 2, row.shape[1]), BF16)], axis=0)


def _mlstm_chunk(q, k, v_t, log_d_t, rows, state):
    c_t, n_vec, m_run = state
    cum, rmax, logw, tot, lwmax = rows
    log_inter = cum + m_run
    m_t = jnp.maximum(log_inter, rmax)
    w_inter = jnp.exp(log_inter - m_t)
    s_t = lax.dot_general(k, q, _NT, preferred_element_type=F32) * ML_SCALE * jnp.exp(log_d_t - m_t)
    num = w_inter * lax.dot_general(c_t.astype(BF16), q, _NT, preferred_element_type=F32)
    num = num + jnp.dot(v_t, s_t.astype(BF16), preferred_element_type=F32)
    nq = lax.dot_general(_hi_lo_rows(n_vec), q, _NT, preferred_element_type=F32)
    den = w_inter * (nq[0:1] + nq[1:2]) + jnp.sum(s_t, axis=0, keepdims=True)
    h_t = num / jnp.maximum(jnp.abs(den), jnp.exp(-m_t))
    m_new = jnp.maximum(tot + m_run, lwmax)
    w_row = jnp.exp(logw - m_new) * ML_SCALE
    decay = jnp.exp(tot + m_run - m_new)
    vw_t = (v_t.astype(F32) * w_row).astype(BF16)
    c_new = decay * c_t + jnp.dot(vw_t, k, preferred_element_type=F32)
    nk = jnp.dot(_hi_lo_rows(w_row), k, preferred_element_type=F32)
    n_new = decay * n_vec + nk[0:1] + nk[1:2]
    return h_t, (c_new, n_new, m_new)


def _mlstm_kernel(q_ref, k_ref, v_ref, og_ref, g_ref, gb_ref, ng_ref, out_ref, vt_s, ldt_s, rows_s, hf_s, hb_s):
    L = ML_CHUNK
    assert ML_HEAD_DIM == L == LANES
    ri = lax.broadcasted_iota(jnp.int32, (L, L), 0)
    ci = lax.broadcasted_iota(jnp.int32, (L, L), 1)
    lower = ci <= ri
    upper = ci >= ri
    lower_m = jnp.where(lower, 1.0, 0.0).astype(BF16)
    upper_m = jnp.where(upper, 1.0, 0.0).astype(BF16)
    lane = lax.broadcasted_iota(jnp.int32, (L, LANES), 1)
    fwd_rows = lax.broadcasted_iota(jnp.int32, (ML_COMBOS, L), 0) < ML_HEADS

    def prep(c, carry):
        rows = pl.ds(pl.multiple_of(c * L, L), L)
        vt_s[c] = v_ref[0, rows, :].T
        g = g_ref[0, rows, :] + gb_ref[...]
        lf = jnp.minimum(g, 0.0) - jnp.log1p(jnp.exp(-jnp.abs(g)))
        cum_f = _exact_ones_matmul(lower_m, lf)
        suf_b = _exact_ones_matmul(upper_m, lf)
        colv = jnp.where(lane < GATE_F_FWD, g, jnp.where(lane < GATE_F_BWD, cum_f, suf_b))
        rowv = colv.T
        i8 = rowv[0:ML_COMBOS]
        cum8 = rowv[ML_COMBOS:2 * ML_COMBOS]
        r_all = colv - pltpu.roll(colv, LANES - ML_COMBOS, 1)
        tot8 = jnp.where(fwd_rows, jnp.broadcast_to(cum8[:, L - 1:L], (ML_COMBOS, L)),
                         jnp.broadcast_to(cum8[:, 0:1], (ML_COMBOS, L)))
        logw8 = tot8 - cum8 + i8
        lwmax8 = jnp.broadcast_to(jnp.max(logw8, axis=-1, keepdims=True), (ML_COMBOS, L))
        rmax = []
        for j in range(ML_COMBOS):
            valid = upper if j < ML_HEADS else lower
            ldt = jnp.where(valid, jnp.broadcast_to(r_all[:, j:j + 1], (L, L)) + cum8[j:j + 1, :], NEG)
            ldt_s[c * ML_COMBOS + j] = ldt
            rmax.append(jnp.max(ldt, axis=0, keepdims=True))
        rows_s[c, ROW_CUM] = cum8
        rows_s[c, ROW_RMAX] = jnp.concatenate(rmax, axis=0)
        rows_s[c, ROW_LOGW] = logw8
        rows_s[c, ROW_TOT] = tot8
        rows_s[c, ROW_LWMAX] = lwmax8
        return carry

    lax.fori_loop(0, N_CHUNKS, prep, 0)

    zero_state = (jnp.zeros((ML_HEAD_DIM, ML_HEAD_DIM), F32), jnp.zeros((1, ML_HEAD_DIM), F32),
                  jnp.zeros((1, L), F32))
    def one(c, h, backward, state):
        hl = slice(ML_HEAD_DIM * h, ML_HEAD_DIM * (h + 1))
        j = ML_HEADS + h if backward else h
        rows = pl.ds(pl.multiple_of(c * L, L), L)
        prepared = tuple(rows_s[c, k, j:j + 1, :] for k in range(5))
        h_t, state = _mlstm_chunk(q_ref[0, rows, hl], k_ref[0, rows, hl], vt_s[c, hl, :],
                                  ldt_s[c * ML_COMBOS + j], prepared, state)
        (hb_s if backward else hf_s)[h, c] = h_t
        return state

    for h0 in range(0, ML_HEADS, ML_HEADS_PER_TRIP):
        def body(c, states, h0=h0):
            out = []
            for k in range(ML_HEADS_PER_TRIP):
                out.append(one(c, h0 + k, False, states[2 * k]))
                out.append(one(N_CHUNKS - 1 - c, h0 + k, True, states[2 * k + 1]))
            return tuple(out)

        lax.fori_loop(0, N_CHUNKS, body, (zero_state,) * (2 * ML_HEADS_PER_TRIP), unroll=4)

    for h in range(ML_HEADS):
        hl = slice(ML_HEAD_DIM * h, ML_HEAD_DIM * (h + 1))
        gain_t = jnp.broadcast_to(ng_ref[:, hl], (L, ML_HEAD_DIM)).T
        for c in range(N_CHUNKS):
            r0 = c * L
            hh = hf_s[h, c] + hb_s[h, c]
            mu = jnp.mean(hh, axis=0, keepdims=True)
            cen = hh - mu
            var = jnp.mean(cen * cen, axis=0, keepdims=True)
            hn = (cen * lax.rsqrt(var + LN_EPS) * gain_t).T
            og = og_ref[0, r0:r0 + L, hl].astype(F32)
            out_ref[0, r0:r0 + L, hl] = (hn / (1.0 + jnp.exp(-og))).astype(out_ref.dtype)


def mlstm_mixer(proj, gates, gate_bias, norm_g):
    b = proj.shape[0]
    c0 = (3 * NA_W) // ML_W
    blk = lambda off: pl.BlockSpec((1, SEQ, ML_W), lambda i, off=off: (i, 0, off))
    gb = jnp.pad(gate_bias.astype(F32), (0, LANES - gate_bias.shape[0])).reshape(1, LANES)
    return pl.pallas_call(
        _mlstm_kernel, grid=(b,),
        in_specs=[blk(c0), blk(c0 + 1), blk(c0 + 2), blk(c0 + 3),
                  pl.BlockSpec((1, SEQ, LANES), lambda i: (i, 0, 0)),
                  pl.BlockSpec((1, LANES), lambda i: (0, 0)),
                  pl.BlockSpec((1, ML_W), lambda i: (0, 0))],
        out_specs=pl.BlockSpec((1, SEQ, ML_W), lambda i: (i, 0, 0)),
        out_shape=jax.ShapeDtypeStruct((b, SEQ, ML_W), BF16),
        scratch_shapes=[pltpu.VMEM((N_CHUNKS, ML_W, ML_CHUNK), BF16),
                        pltpu.VMEM((N_CHUNKS * ML_COMBOS, ML_CHUNK, ML_CHUNK), F32),
                        pltpu.VMEM((N_CHUNKS, 5, ML_COMBOS, ML_CHUNK), F32),
                        pltpu.VMEM((ML_HEADS, N_CHUNKS, ML_HEAD_DIM, ML_CHUNK), F32),
                        pltpu.VMEM((ML_HEADS, N_CHUNKS, ML_HEAD_DIM, ML_CHUNK), F32)],
        compiler_params=_cparams("parallel"), name="mlstm_mixer",
    )(proj, proj, proj, proj, gates, gb, norm_g.reshape(1, ML_W).astype(F32))


def _router_kernel(x_ref, w_ref, aff_ref):
    x = _load_token_tiles(x_ref, x_ref.shape[0] // TILE_ROWS)
    w = w_ref[...]
    x_hi = x.astype(BF16)
    x_lo = (x - x_hi.astype(F32)).astype(BF16)
    w_hi = w.astype(BF16)
    w_lo = (w - w_hi.astype(F32)).astype(BF16)
    dot = lambda a, b: jnp.dot(a, b, preferred_element_type=F32)
    logits_t = dot(x_hi, w_hi) + dot(x_hi, w_lo) + dot(x_lo, w_hi)
    logits = logits_t.T[0:N_EXPERTS, :]
    z = jnp.exp(logits - jnp.max(logits, axis=0, keepdims=True))
    aff_ref[...] = z / jnp.sum(z, axis=0, keepdims=True)


def router_affinities(x_tiles, w_router, tm=1024):
    n, d = x_tiles.shape[0] // TILE_ROWS, D_MODEL
    w_pad = jnp.pad(w_router.astype(F32), ((0, 0), (0, LANES - N_EXPERTS)))
    return pl.pallas_call(
        _router_kernel, grid=(n // tm,),
        in_specs=[pl.BlockSpec((tm * TILE_ROWS, LANES), lambda i: (i, 0)),
                  pl.BlockSpec((d, LANES), lambda i: (0, 0))],
        out_specs=pl.BlockSpec((N_EXPERTS, tm), lambda i: (0, i)),
        out_shape=jax.ShapeDtypeStruct((N_EXPERTS, n), F32),
        compiler_params=_cparams("parallel"), name="router_affinities")(x_tiles, w_pad)


def _tri_matrices(r):
    li = lax.broadcasted_iota(jnp.int32, (LANES, LANES), 0)
    lj = lax.broadcasted_iota(jnp.int32, (LANES, LANES), 1)
    tri_u = jnp.where(li <= lj, 1.0, 0.0).astype(BF16)
    ri = lax.broadcasted_iota(jnp.int32, (r, r), 0)
    rj = lax.broadcasted_iota(jnp.int32, (r, r), 1)
    tri_l = jnp.where(rj < ri, 1.0, 0.0).astype(BF16)
    return tri_u, tri_l


def _prefix_counts(mask, tri_u, tri_l):
    r = mask.shape[0]
    within = jnp.dot(mask.astype(BF16), tri_u, preferred_element_type=F32)
    rowtot = within[:, LANES - 1:LANES]
    hi = jnp.floor(rowtot * (1.0 / 16.0))
    lo = rowtot - 16.0 * hi
    hi_b = jnp.broadcast_to(hi, (r, LANES)).astype(BF16)
    lo_b = jnp.broadcast_to(lo, (r, LANES)).astype(BF16)
    rowoff = 16.0 * jnp.dot(tri_l, hi_b, preferred_element_type=F32) + jnp.dot(
        tri_l, lo_b, preferred_element_type=F32)
    return within - mask + rowoff, within, rowoff, rowtot


def _select_kernel(aff_ref, sel_ref, *, cap):
    n_e, r, _ = aff_ref.shape
    tri_u, tri_l = _tri_matrices(r)

    def bits(k):
        return pltpu.bitcast(aff_ref[k], jnp.int32)

    def count(m):
        c = jnp.sum(jnp.where(m, 1.0, 0.0), axis=0, keepdims=True)
        return jnp.sum(c, axis=1, keepdims=True)

    def bisect(i, prefixes):
        bit = jnp.left_shift(jnp.int32(1), 30 - i)
        return tuple(jnp.where(count(bits(k) >= (p | bit)) >= cap, p | bit, p)
                     for k, p in enumerate(prefixes))

    thr = lax.fori_loop(0, 31, bisect, (jnp.zeros((1, 1), jnp.int32),) * n_e)
    for k in range(n_e):
        gt = bits(k) > thr[k]
        eq = bits(k) == thr[k]
        need = cap - count(gt)
        rank_eq, _, _, _ = _prefix_counts(jnp.where(eq, 1.0, 0.0), tri_u, tri_l)
        sel_ref[k] = jnp.where(gt | (eq & (rank_eq < need)), 1.0, 0.0)


def select_tokens(aff3, cap):
    e, r, _ = aff3.shape
    blk = pl.BlockSpec((e, r, LANES), lambda i: (0, 0, 0))
    return pl.pallas_call(
        functools.partial(_select_kernel, cap=cap), grid=(1,), in_specs=[blk], out_specs=blk,
        out_shape=jax.ShapeDtypeStruct((e, r, LANES), F32),
        compiler_params=_cparams("arbitrary"), name="select_tokens")(aff3)


def _lists_kernel(sel_ref, aff_ref, idx_ref, dst_ref, gate_ref, ts_ref, ts_s, er_s, *, cap, st):
    e = pl.program_id(0)
    r = sel_ref.shape[1]
    tri_u, tri_l = _tri_matrices(r)

    @pl.when(e == 0)
    def _():
        cnt = sel_ref[0]
        for k in range(1, N_EXPERTS):
            cnt = cnt + sel_ref[k]
        ts, _, _, _ = _prefix_counts(cnt, tri_u, tri_l)
        ts_s[...] = ts
        ts_ref[...] = ts
        er_s[...] = jnp.zeros_like(er_s)

    sel = sel_ref[e]
    _, within, rowoff, rowtot = _prefix_counts(sel, tri_u, tri_l)
    dst = ts_s[...] + er_s[...]
    er_s[...] = er_s[...] + sel

    d2 = jnp.floor(dst * (1.0 / 65536.0))
    rem = dst - 65536.0 * d2
    d1 = jnp.floor(rem * (1.0 / 256.0))
    d0 = rem - 256.0 * d1
    aff = aff_ref[0]
    a_hi = aff.astype(BF16)
    a_r1 = aff - a_hi.astype(F32)
    a_mid = a_r1.astype(BF16)
    a_lo = (a_r1 - a_mid.astype(F32)).astype(BF16)
    rhs = jnp.concatenate([within.astype(BF16), d0.astype(BF16), d1.astype(BF16), d2.astype(BF16),
                           a_hi, a_mid, a_lo], axis=1)
    rowoff_row = rowoff.T[0:1, :]
    rowend_row = rowoff_row + jnp.broadcast_to(rowtot, (r, LANES)).T[0:1, :]
    rho_row = lax.broadcasted_iota(jnp.int32, (1, r), 1).astype(F32)
    lane = lax.broadcasted_iota(jnp.int32, (st, LANES), 1).astype(F32)
    eye = lax.broadcasted_iota(jnp.int32, (LANES, LANES), 0) == lax.broadcasted_iota(
        jnp.int32, (LANES, LANES), 1)

    for t in range(cap // st):
        s_col = (t * st + lax.broadcasted_iota(jnp.int32, (st, 1), 0)).astype(F32)
        in_row = (rowoff_row <= s_col) & (s_col < rowend_row)
        got = jnp.dot(jnp.where(in_row, 1.0, 0.0).astype(BF16), rhs, preferred_element_type=F32)
        base = jnp.sum(jnp.where(in_row, rowoff_row, 0.0), axis=1, keepdims=True)
        rho = jnp.sum(jnp.where(in_row, rho_row, 0.0), axis=1, keepdims=True)
        local = s_col - base
        lam = jnp.sum(jnp.where(got[:, 0:LANES] <= local, 1.0, 0.0), axis=1, keepdims=True)
        plane = lambda k: got[:, k * LANES:(k + 1) * LANES]
        pair = plane(1) + 256.0 * plane(2) + 65536.0 * plane(3)
        at_lam = lane == lam
        dval = jnp.sum(jnp.where(at_lam, pair, 0.0), axis=1, keepdims=True)
        gate_ref[0, t * st:(t + 1) * st, :] = jnp.sum(
            jnp.where(at_lam, plane(4) + plane(5) + plane(6), 0.0), axis=1, keepdims=True)
        ival = rho * float(LANES) + lam
        for j in range(st // LANES):
            seg = slice(j * LANES, (j + 1) * LANES)
            row = t * (st // LANES) + j
            idx_ref[0, row:row + 1, :] = jnp.sum(
                jnp.where(eye, ival[seg], 0.0), axis=0, keepdims=True).astype(jnp.int32)
            dst_ref[0, row:row + 1, :] = jnp.sum(
                jnp.where(eye, dval[seg], 0.0), axis=0, keepdims=True).astype(jnp.int32)


def build_lists(sel3, aff3, cap):
    e, r, _ = sel3.shape
    st = min(512, cap)
    lst = pl.BlockSpec((1, cap // LANES, LANES), lambda i: (i, 0, 0))
    return pl.pallas_call(
        functools.partial(_lists_kernel, cap=cap, st=st), grid=(e,),
        in_specs=[pl.BlockSpec((e, r, LANES), lambda i: (0, 0, 0)),
                  pl.BlockSpec((1, r, LANES), lambda i: (i, 0, 0))],
        out_specs=[lst, lst, pl.BlockSpec((1, cap, 1), lambda i: (i, 0, 0)),
                   pl.BlockSpec((r, LANES), lambda i: (0, 0))],
        out_shape=[jax.ShapeDtypeStruct((e, cap // LANES, LANES), jnp.int32),
                   jax.ShapeDtypeStruct((e, cap // LANES, LANES), jnp.int32),
                   jax.ShapeDtypeStruct((e, cap, 1), F32), jax.ShapeDtypeStruct((r, LANES), F32)],
        scratch_shapes=[pltpu.VMEM((r, LANES), F32), pltpu.VMEM((r, LANES), F32)],
        compiler_params=_cparams("arbitrary"), name="build_lists")(sel3, aff3)


FFN_CHUNKS = tuple((f, min(f + 256, D_FF)) for f in range(0, D_FF, 256))


def _ffn_kernel(idx_first, idx_next_a, idx_next_b, dst_prev_a, dst_prev_b, dst_last,
                x_hbm, gate_a, gate_b, wg_ref, wu_ref, wd_ref, z_hbm,
                xbuf0, xbuf1, ybuf0, ybuf1, acc, gsem, ssem, *, tm, n_grid):
    g = pl.program_id(0)

    def token_tile(t):
        if isinstance(t, int):
            return pl.ds(t * TILE_ROWS, TILE_ROWS)
        return pl.ds(pl.multiple_of(t * TILE_ROWS, TILE_ROWS), TILE_ROWS)

    def gather_row(ids, i, buf, sem):
        pltpu.make_async_copy(x_hbm.at[token_tile(ids[0, 0, i]), :], buf.at[token_tile(i), :],
                              sem).start(priority=1)

    def scatter_row(dsts, i, buf, sem):
        pltpu.make_async_copy(buf.at[token_tile(i), :], z_hbm.at[token_tile(dsts[0, 0, i]), :],
                              sem).start(priority=0)

    def wait_rows(buf, sem):
        pltpu.make_async_copy(x_hbm.at[pl.ds(0, tm * TILE_ROWS), :], buf, sem).wait()

    @pl.when(g == 0)
    def _():
        ybuf1[...] = jnp.zeros_like(ybuf1)

        def first(i, carry):
            gather_row(idx_first, i, xbuf0, gsem.at[0])
            return carry
        lax.fori_loop(0, tm, first, 0, unroll=8)

    def tile(xcur, gcur, xnext, gnext, idx_next, yprev, sprev, dst_prev):
        for i in range(tm):
            gather_row(idx_next, i, xnext, gnext)
            scatter_row(dst_prev, i, yprev, sprev)

        wait_rows(xcur, gcur)
        xb = _load_token_tiles(xcur, tm).astype(BF16)
        for c, (f0, f1) in enumerate(FFN_CHUNKS):
            gate = jnp.dot(xb, wg_ref[0, :, f0:f1], preferred_element_type=F32)
            up = jnp.dot(xb, wu_ref[0, :, f0:f1], preferred_element_type=F32)
            h = (gate / (1.0 + jnp.exp(-gate)) * up).astype(BF16)
            part = jnp.dot(h, wd_ref[0, f0:f1, :], preferred_element_type=F32)
            if c == 0:
                acc[...] = part
            else:
                acc[...] += part

    tile(xbuf0, gsem.at[0], xbuf1, gsem.at[1], idx_next_a, ybuf1, ssem.at[1], dst_prev_a)

    @pl.when(g >= 1)
    def _():
        wait_rows(ybuf0, ssem.at[0])
    _store_token_tiles(ybuf0, acc[...] * gate_a[0])

    tile(xbuf1, gsem.at[1], xbuf0, gsem.at[0], idx_next_b, ybuf0, ssem.at[0], dst_prev_b)
    wait_rows(ybuf1, ssem.at[1])
    _store_token_tiles(ybuf1, acc[...] * gate_b[0])

    @pl.when(g == n_grid - 1)
    def _():
        def last(i, carry):
            scatter_row(dst_last, i, ybuf1, ssem.at[1])
            return carry
        lax.fori_loop(0, tm, last, 0, unroll=8)
        wait_rows(ybuf1, ssem.at[1])
        wait_rows(ybuf0, ssem.at[0])
        wait_rows(xbuf0, gsem.at[0])


def expert_ffn(x_tiles, idx, dst, gate, w_gate, w_up, w_down, tm=512):
    d = D_MODEL
    e, cap = idx.shape
    tm = min(tm, cap // 2)
    nt = cap // tm
    assert nt % 2 == 0
    n_tiles = e * nt
    n_grid = n_tiles // 2
    idx3 = idx.reshape(n_tiles, 1, tm)
    gate3 = gate.reshape(n_tiles, tm, 1)
    spare =(e * cap + jnp.arange(tm, dtype=jnp.int32)).reshape(1, 1, tm)
    dst3 = jnp.concatenate([spare, dst.reshape(n_tiles, 1, tm)])
    smem = lambda imap: pl.BlockSpec((1, 1, tm), imap, memory_space=pltpu.SMEM)
    wspec = lambda w: pl.BlockSpec((1,) + w.shape[1:], lambda i: ((2 * i) // nt, 0, 0))
    return pl.pallas_call(
        functools.partial(_ffn_kernel, tm=tm, n_grid=n_grid), grid=(n_grid,),
        in_specs=[smem(lambda i: (0, 0, 0)), smem(lambda i: (2 * i + 1, 0, 0)),
                  smem(lambda i: (jnp.minimum(2 * i + 2, n_tiles - 1), 0, 0)),
                  smem(lambda i: (2 * i, 0, 0)), smem(lambda i: (2 * i + 1, 0, 0)),
                  smem(lambda i: (n_tiles, 0, 0)),
                  pl.BlockSpec(memory_space=pl.ANY),
                  pl.BlockSpec((1, tm, 1), lambda i: (2 * i, 0, 0)),
                  pl.BlockSpec((1, tm, 1), lambda i: (2 * i + 1, 0, 0)),
                  wspec(w_gate), wspec(w_up), wspec(w_down)],
        out_specs=pl.BlockSpec(memory_space=pl.ANY),
        out_shape=jax.ShapeDtypeStruct(((e * cap + tm) * TILE_ROWS, LANES), F32),
        scratch_shapes=[pltpu.VMEM((tm * TILE_ROWS, LANES), F32)] * 4 + [pltpu.VMEM((tm, d), F32)]
        + [pltpu.SemaphoreType.DMA((2,)), pltpu.SemaphoreType.DMA((2,))],
        compiler_params=_cparams("arbitrary"), name="expert_ffn",
    )(idx3, idx3, idx3, dst3, dst3, dst3, x_tiles, gate3, gate3, w_gate, w_up, w_down)


COMBINE_ZB = 640
COMBINE_SLOTS = 4


def _combine_kernel(ts_ref, x_ref, run_ref, g_ref, b_ref, z_hbm, o_ref, zbuf, sem, used, *, z_rows, n_tiles):
    i = pl.program_id(0)
    tt = x_ref.shape[0] // TILE_ROWS

    def chunk_rows(t, c):
        lo = ts_ref[t] + c * COMBINE_ZB
        return lo, jnp.minimum(lo, z_rows - COMBINE_ZB)

    def piece_copy(start, j, slot):
        return pltpu.make_async_copy(z_hbm.at[pl.ds(start, COMBINE_ZB), j, :], zbuf.at[slot, j], sem.at[slot])

    def chunks_of(t):
        return jnp.maximum((ts_ref[t + 1] - ts_ref[t] + COMBINE_ZB - 1) // COMBINE_ZB, 1)

    def fetch_next():
        t = used[2]

        @pl.when(t < n_tiles)
        def _():
            c = used[3]
            slot = used[1] % COMBINE_SLOTS
            _, start = chunk_rows(t, c)
            for j in range(TILE_ROWS):
                piece_copy(start, j, slot).start()
            used[1] = used[1] + 1
            last = c + 1 >= chunks_of(t)
            used[2] = jnp.where(last, t + 1, t)
            used[3] = jnp.where(last, 0, c + 1)

    @pl.when(i == 0)
    def _():
        for k in range(4):
            used[k] = 0
        for _ in range(COMBINE_SLOTS - 1):
            fetch_next()

    base = used[0]
    n_chunks = chunks_of(i)
    run_lo = jnp.broadcast_to(run_ref[:, 0:1], (tt, COMBINE_ZB))
    run_hi = jnp.broadcast_to(run_ref[:, 1:2], (tt, COMBINE_ZB))
    col = lax.broadcasted_iota(jnp.int32, (1, COMBINE_ZB), 1).astype(F32)

    def chunk(c, acc):
        slot = (base + c) % COMBINE_SLOTS
        lo, start = chunk_rows(i, c)
        pair = col + start.astype(F32)
        pair = jnp.where(pair >= lo.astype(F32), pair, -1.0)
        a = jnp.where((run_lo <= pair) & (pair < run_hi), 1.0, 0.0).astype(BF16)
        for j in range(TILE_ROWS):
            piece_copy(0, j, slot).wait()
        rows = jnp.concatenate([zbuf[slot, j] for j in range(TILE_ROWS)], axis=1).astype(BF16)
        acc = acc + jnp.dot(a, rows, preferred_element_type=F32)
        fetch_next()
        return acc

    ffn = lax.fori_loop(0, n_chunks, chunk, jnp.zeros((tt, D_MODEL), F32))
    used[0] = base + n_chunks
    o_ref[...] = _layer_norm_rows(DN_ALPHA * _load_token_tiles(x_ref, tt) + ffn, g_ref[...], b_ref[...])


def combine_ln(x_tiles, z, tile_start, runs, g, b, tt=256):
    n, d = x_tiles.shape[0] // TILE_ROWS, D_MODEL
    n_tiles = n // tt
    row = lambda i, ts: (i, 0)
    fixed = lambda i, ts: (0, 0)
    grid_spec = pltpu.PrefetchScalarGridSpec(
        num_scalar_prefetch=1, grid=(n_tiles,),
        in_specs=[pl.BlockSpec((tt * TILE_ROWS, LANES), row), pl.BlockSpec((tt, 2), row),
                  pl.BlockSpec((1, d), fixed),
                  pl.BlockSpec((1, d), fixed), pl.BlockSpec(memory_space=pl.ANY)],
        out_specs=pl.BlockSpec((tt, d), row),
        scratch_shapes=[pltpu.VMEM((COMBINE_SLOTS, TILE_ROWS, COMBINE_ZB, LANES), F32),
                        pltpu.SemaphoreType.DMA((COMBINE_SLOTS,)), pltpu.SMEM((4,), jnp.int32)])
    z_rows = z.shape[0] // TILE_ROWS
    return pl.pallas_call(
        functools.partial(_combine_kernel, z_rows=z_rows, n_tiles=n_tiles), grid_spec=grid_spec,
        out_shape=jax.ShapeDtypeStruct((n, d), F32),
        compiler_params=_cparams("arbitrary"), name="combine_ln",
    )(tile_start, x_tiles, runs, g.reshape(1, d), b.reshape(1, d), z.reshape(z_rows, TILE_ROWS, LANES))


def moe_layer(x_tiles, w_router, w_gate, w_up, w_down, g, b, tt=256):
    n = x_tiles.shape[0] // TILE_ROWS
    r = n // LANES
    cap = 2 * n // N_EXPERTS
    aff3 = router_affinities(x_tiles, w_router).reshape(N_EXPERTS, r, LANES)
    sel3 = select_tokens(aff3, cap)
    idx, dst, gate, ts = build_lists(sel3, aff3, cap)
    z = expert_ffn(x_tiles, idx.reshape(N_EXPERTS, cap), dst.reshape(N_EXPERTS, cap), gate, w_gate, w_up, w_down)
    ts_ext = jnp.concatenate([ts.reshape(n), jnp.full((1,), N_EXPERTS * cap, F32)])
    tile_start = ts_ext[::tt].astype(jnp.int32)
    runs = jnp.stack([ts_ext[:-1], ts_ext[1:]], axis=1)
    return combine_ln(x_tiles, z, tile_start, runs, g, b, tt=tt)


def kernel(x_prompt, x_sample, even_w_in, ml_gate_bias, na_rpb, ml_norm_g, even_w_out, da_w_in, da_w_out,
           ln_mix_g, ln_mix_b, ec_router, ec_w_gate, ec_w_up, ec_w_down, ln_ffn_g, ln_ffn_b):
    n_mix_cols = 3 * NA_W + 4 * ML_W
    n_gates = 4 * ML_HEADS
    w_even = even_w_in[0][:, :n_mix_cols].astype(BF16)
    w_gates = jnp.pad(even_w_in[0][:, n_mix_cols:], ((0, 0), (0, LANES - n_gates))).astype(BF16)
    w_out_a = even_w_out[0][:NA_W].astype(BF16)
    w_out_b = even_w_out[0][NA_W:].astype(BF16)
    w_odd = da_w_in[0].astype(BF16)
    w_odd_out = da_w_out[0].astype(BF16)
    tbl = na_bias_table(na_rpb[0])
    rope_t = rope_tables()
    mask_t = da_mask_table()
    moe_w = [(ec_router[l], ec_w_gate[l].astype(BF16), ec_w_up[l].astype(BF16), ec_w_down[l].astype(BF16),
              ln_ffn_g[l], ln_ffn_b[l]) for l in range(DEPTH)]

    def trunk(x):
        b = x.shape[0]
        xt = x.reshape(b * SEQ, D_MODEL)
        proj, gates = in_projection_gates(xt, w_even, w_gates)
        proj = proj.reshape(b, SEQ, -1)
        ya = neighbourhood_attention(proj, tbl)
        yb = mlstm_mixer(proj, gates.reshape(b, SEQ, LANES), ml_gate_bias[0], ml_norm_g[0])
        xt_tiles = out_projection_ln(xt, [ya.reshape(b * SEQ, NA_W), yb.reshape(b * SEQ, ML_W)],
                                     [w_out_a, w_out_b], ln_mix_g[0], ln_mix_b[0])
        xt = moe_layer(xt_tiles, *moe_w[0])
        proj = in_projection_rope(xt, w_odd, rope_t, DA_W).reshape(b, SEQ, -1)
        yc = dilated_attention(proj, mask_t)
        xt_tiles = out_projection_ln(xt, [yc.reshape(b * SEQ, DA_W)], [w_odd_out], ln_mix_g[1], ln_mix_b[1])
        xt = moe_layer(xt_tiles, *moe_w[1])
        return xt.reshape(b, SEQ, D_MODEL)

    return trunk(x_prompt), trunk(x_sample)
```
